```python
import jax, jax.numpy as jnp
from jax import lax
import numpy as np

D_MODEL = 2048
BATCH = 2
SEQ = 4096
DEPTH = 2

MLSTM_HEADS = 4
MLSTM_DHEAD = D_MODEL // (2 * MLSTM_HEADS)
MLSTM_WIDTH = MLSTM_HEADS * MLSTM_DHEAD
RET_HEADS = 4
RET_DHEAD = D_MODEL // (2 * RET_HEADS)
RET_WIDTH = RET_HEADS * RET_DHEAD
MIX_WIDTH = MLSTM_WIDTH + RET_WIDTH
IN_COLS = 4 * MLSTM_WIDTH + 2 * MLSTM_HEADS + 4 * RET_WIDTH
CONV_WIDTH = 4
MLSTM_CHUNK = 64
RET_CHUNK = 128
MLA_HEADS = 16
Q_LORA = 512
KV_LORA = 512
QK_NOPE = 128
QK_ROPE = 64
V_HEAD = 128
ATTN_BLOCK = 128
D_FF = ((8 * D_MODEL // 3 + 255) // 256) * 256
N_EXPERTS = 8
TOP_K = 2
D_FF_EXPERT = D_FF
ROPE_BASE = 10000.0
EPS = 1e-6

kernel_name = 'mlstm_retention_mla_moe_hybrid'


def rmsnorm(x, g):
    xf = x.astype(jnp.float32)
    y = xf * lax.rsqrt(jnp.mean(xf * xf, axis=-1, keepdims=True) + EPS)
    return (y * g.astype(jnp.float32)).astype(x.dtype)


def rope(x, pos):
    half = x.shape[-1] // 2
    inv = ROPE_BASE ** (-jnp.arange(half, dtype=jnp.float32) / half)
    ang = pos.astype(jnp.float32)[..., None] * inv
    cos = jnp.cos(ang)[:, :, None, :]
    sin = jnp.sin(ang)[:, :, None, :]
    x1 = x[..., :half].astype(jnp.float32)
    x2 = x[..., half:].astype(jnp.float32)
    return jnp.concatenate([x1 * cos - x2 * sin, x2 * cos + x1 * sin], axis=-1).astype(x.dtype)


def causal_dwconv(x, w):
    C = x.shape[-1]
    xp = jnp.pad(x, ((0, 0), (w.shape[0] - 1, 0), (0, 0)))
    return lax.conv_general_dilated(xp, w[:, None, :].astype(x.dtype), window_strides=(1,), padding='VALID',
                                    dimension_numbers=('NWC', 'WIO', 'NWC'), feature_group_count=C)


def to_chunks(x, L):
    B, S, H = x.shape[:3]
    x = x.reshape((B, S // L, L, H) + x.shape[3:])
    return x.transpose((1, 0, 3, 2) + tuple(range(4, x.ndim)))


def from_chunks(y):
    NC, B, H, L, d = y.shape
    return y.transpose(1, 0, 3, 2, 4).reshape(B, NC * L, H, d)


def mlstm_chunkwise(q, k, v, i_pre, logf):
    L = MLSTM_CHUNK
    B, S, H, dk = q.shape
    dv = v.shape[-1]
    qc = to_chunks(q.astype(jnp.float32), L)
    kc = to_chunks(k.astype(jnp.float32), L) * (dk ** -0.5)
    vc = to_chunks(v.astype(jnp.float32), L)
    ic = to_chunks(i_pre.astype(jnp.float32), L)
    fc = to_chunks(logf.astype(jnp.float32), L)
    causal = jnp.tril(jnp.ones((L, L), dtype=bool))

    def step(carry, xs):
        C, n, m = carry
        q_, k_, v_, i_, f_ = xs
        b = jnp.cumsum(f_, axis=-1)
        g = b[..., -1]
        logD = jnp.where(causal, b[..., :, None] - b[..., None, :] + i_[..., None, :], -jnp.inf)
        m_inter = b + m[..., None]
        m_t = jnp.maximum(logD.max(axis=-1), m_inter)
        Dm = jnp.exp(logD - m_t[..., None])
        inter = jnp.exp(m_inter - m_t)
        Sm = jnp.einsum('bhld,bhsd->bhls', q_, k_) * Dm
        num = jnp.einsum('bhls,bhsv->bhlv', Sm, v_) + inter[..., None] * jnp.einsum('bhld,bhdv->bhlv', q_, C)
        den = Sm.sum(axis=-1) + inter * jnp.einsum('bhld,bhd->bhl', q_, n)
        h = num / jnp.maximum(jnp.abs(den), jnp.exp(-m_t))[..., None]
        log_w = g[..., None] - b + i_
        m_new = jnp.maximum(g + m, log_w.max(axis=-1))
        w = jnp.exp(log_w - m_new[..., None])
        decay = jnp.exp(g + m - m_new)
        C = decay[..., None, None] * C + jnp.einsum('bhl,bhld,bhlv->bhdv', w, k_, v_)
        n = decay[..., None] * n + jnp.einsum('bhl,bhld->bhd', w, k_)
        return (C, n, m_new), h

    init = (jnp.zeros((B, H, dk, dv), jnp.float32), jnp.zeros((B, H, dk), jnp.float32),
            jnp.full((B, H), -jnp.inf, jnp.float32))
    _, hs = lax.scan(step, init, (qc, kc, vc, ic, fc))
    return from_chunks(hs)


def retention_chunkwise(q, k, v, log_gamma):
    L = RET_CHUNK
    B, S, H, dk = q.shape
    dv = v.shape[-1]
    qc = to_chunks(q.astype(jnp.float32), L)
    kc = to_chunks(k.astype(jnp.float32), L) * (dk ** -0.5)
    vc = to_chunks(v.astype(jnp.float32), L)
    j = jnp.arange(L, dtype=jnp.float32)
    lg = log_gamma[:, None]
    dist = j[:, None] - j[None, :]
    D = jnp.where(dist >= 0, jnp.exp(lg[..., None] * jnp.maximum(dist, 0.0)), 0.0)
    q_decay = jnp.exp(lg * (j + 1.0))
    k_decay = jnp.exp(lg * (L - 1.0 - j))
    chunk_decay = jnp.exp(log_gamma * L)

    def step(R, xs):
        q_, k_, v_ = xs
        Sm = jnp.einsum('bhld,bhsd->bhls', q_, k_) * D
        out = jnp.einsum('bhls,bhsv->bhlv', Sm, v_) + jnp.einsum('bhld,bhdv->bhlv', q_, R) * q_decay[..., None]
        R = chunk_decay[:, None, None] * R + jnp.einsum('bhld,hl,bhlv->bhdv', k_, k_decay, v_)
        return R, out

    _, outs = lax.scan(step, jnp.zeros((B, H, dk, dv), jnp.float32), (qc, kc, vc))
    return from_chunks(outs)


def recurrent_hybrid_mixer(h, pos, w_in, conv_w, b_gates, g_mlstm_out, g_ret_out, w_out):
    B, S, _ = h.shape
    Hm, Hr = MLSTM_HEADS, RET_HEADS
    sizes = [2 * MLSTM_WIDTH, MLSTM_WIDTH, MLSTM_WIDTH, 2 * Hm, RET_WIDTH, RET_WIDTH, RET_WIDTH, RET_WIDTH]
    offs = np.cumsum(sizes)[:-1].tolist()
    m_qk, m_v, m_o, m_if, r_q, r_k, r_v, r_g = jnp.split(h @ w_in, offs, axis=-1)
    m_qk = jax.nn.silu(causal_dwconv(m_qk, conv_w))
    m_q = m_qk[..., :MLSTM_WIDTH].reshape(B, S, Hm, MLSTM_DHEAD)
    m_k = m_qk[..., MLSTM_WIDTH:].reshape(B, S, Hm, MLSTM_DHEAD)
    gates = m_if.astype(jnp.float32) + b_gates.astype(jnp.float32)
    i_pre = gates[..., :Hm]
    logf = jax.nn.log_sigmoid(gates[..., Hm:])
    hm = mlstm_chunkwise(m_q, m_k, m_v.reshape(B, S, Hm, MLSTM_DHEAD), i_pre, logf)
    hm = jax.nn.sigmoid(m_o.astype(jnp.float32)).reshape(B, S, Hm, MLSTM_DHEAD) * hm
    hm = rmsnorm(hm, g_mlstm_out.reshape(Hm, MLSTM_DHEAD)).reshape(B, S, MLSTM_WIDTH).astype(h.dtype)
    log_gamma = jnp.log1p(-jnp.exp2(-5.0 - jnp.arange(Hr, dtype=jnp.float32)))
    r_q = rope(r_q.reshape(B, S, Hr, RET_DHEAD), pos)
    r_k = rope(r_k.reshape(B, S, Hr, RET_DHEAD), pos)
    hr = retention_chunkwise(r_q, r_k, r_v.reshape(B, S, Hr, RET_DHEAD), log_gamma)
    hr = rmsnorm(hr, g_ret_out.reshape(Hr, RET_DHEAD)).reshape(B, S, RET_WIDTH)
    hr = (jax.nn.silu(r_g.astype(jnp.float32)) * hr).astype(h.dtype)
    return jnp.concatenate([hm, hr], axis=-1) @ w_out


def mla_mixer(h, pos, w_dqkv, g_cq, g_ckv, w_uq, w_ukv, g_qn, g_qr, g_kn, g_kr, w_o):
    B, S, _ = h.shape
    H = MLA_HEADS
    c_q, c_kv, k_r = jnp.split(h @ w_dqkv, [Q_LORA, Q_LORA + KV_LORA], axis=-1)
    c_q = rmsnorm(c_q, g_cq)
    c_kv = rmsnorm(c_kv, g_ckv)
    q = (c_q @ w_uq).reshape(B, S, H, QK_NOPE + QK_ROPE)
    kv = (c_kv @ w_ukv).reshape(B, S, H, QK_NOPE + V_HEAD)
    q_n = rmsnorm(q[..., :QK_NOPE], g_qn)
    q_r = rope(rmsnorm(q[..., QK_NOPE:], g_qr), pos)
    k_n = rmsnorm(kv[..., :QK_NOPE], g_kn)
    v = kv[..., QK_NOPE:]
    k_r = rope(rmsnorm(k_r, g_kr)[:, :, None, :], pos)[:, :, 0, :]
    scale = (QK_NOPE + QK_ROPE) ** -0.5
    nb = S // ATTN_BLOCK
    qn_b = q_n.reshape(B, nb, ATTN_BLOCK, H, QK_NOPE).transpose(1, 0, 2, 3, 4)
    qr_b = q_r.reshape(B, nb, ATTN_BLOCK, H, QK_ROPE).transpose(1, 0, 2, 3, 4)
    kpos = jnp.arange(S)

    def block(args):
        qn, qr, bi = args
        s = jnp.einsum('bqhd,bkhd->bhqk', qn, k_n) + jnp.einsum('bqhd,bkd->bhqk', qr, k_r)
        s = s.astype(jnp.float32) * scale
        qpos = bi * ATTN_BLOCK + jnp.arange(ATTN_BLOCK)
        s = jnp.where(kpos[None, :] <= qpos[:, None], s, -jnp.inf)
        p = jax.nn.softmax(s, axis=-1).astype(v.dtype)
        return jnp.einsum('bhqk,bkhd->bqhd', p, v)

    o = lax.map(block, (qn_b, qr_b, jnp.arange(nb)))
    o = o.transpose(1, 0, 2, 3, 4).reshape(B, S, H * V_HEAD)
    return o @ w_o


def swiglu(t, wg, wu, wd):
    return (jax.nn.silu(t @ wg) * (t @ wu)) @ wd


def moe_swiglu(h, w_router, we_gate, we_up, we_down):
    B, S, D = h.shape
    t = h.reshape(B * S, D)
    logits = (t @ w_router).astype(jnp.float32)
    top_val, top_idx = lax.top_k(logits, TOP_K)
    top_w = jax.nn.softmax(top_val, axis=-1)
    combine = jnp.einsum('tk,tke->te', top_w, jax.nn.one_hot(top_idx, N_EXPERTS, dtype=jnp.float32)).astype(t.dtype)
    y = jnp.zeros_like(t)
    for e in range(N_EXPERTS):
        y = y + combine[:, e:e + 1] * swiglu(t, we_gate[e], we_up[e], we_down[e])
    return y.reshape(B, S, D)


def setup_inputs(seed: int = 0) -> dict:
    key = jax.random.key(seed)
    ks = iter(jax.random.split(key, 40))
    f32 = jnp.float32
    ne, no = (DEPTH + 1) // 2, DEPTH // 2
    out_scale = (2 * DEPTH) ** -0.5

    def dense(shape, fan_in, scale=1.0):
        return jax.random.normal(next(ks), shape, f32) * (scale * fan_in ** -0.5)

    def gain(shape):
        return 1.0 + 0.02 * jax.random.normal(next(ks), shape, f32)

    x = jax.random.normal(next(ks), (BATCH, SEQ, D_MODEL), f32)
    positions = (jax.random.randint(next(ks), (BATCH, 1), 0, 2048, jnp.int32)
                 + jnp.arange(SEQ, dtype=jnp.int32)[None, :])
    b_base = jnp.concatenate([jnp.zeros((MLSTM_HEADS,), f32), jnp.linspace(3.0, 6.0, MLSTM_HEADS, dtype=f32)])
    b_gates = b_base[None, :] + 0.1 * jax.random.normal(next(ks), (ne, 2 * MLSTM_HEADS), f32)
    return {
        'x': x,
        'positions': positions,
        'g_mix_norm': gain((DEPTH, D_MODEL)),
        'g_ffn_norm': gain((DEPTH, D_MODEL)),
        'w_in': dense((ne, D_MODEL, IN_COLS), D_MODEL),
        'conv_w': dense((ne, CONV_WIDTH, 2 * MLSTM_WIDTH), CONV_WIDTH),
        'b_gates': b_gates,
        'g_mlstm_out': gain((ne, MLSTM_WIDTH)),
        'g_ret_out': gain((ne, RET_WIDTH)),
        'w_mix_out': dense((ne, MIX_WIDTH, D_MODEL), MIX_WIDTH, out_scale),
        'ffn_gate': dense((ne, D_MODEL, D_FF), D_MODEL),
        'ffn_up': dense((ne, D_MODEL, D_FF), D_MODEL),
        'ffn_down': dense((ne, D_FF, D_MODEL), D_FF, out_scale),
        'w_dqkv': dense((no, D_MODEL, Q_LORA + KV_LORA + QK_ROPE), D_MODEL),
        'g_cq': gain((no, Q_LORA)),
        'g_ckv': gain((no, KV_LORA)),
        'w_uq': dense((no, Q_LORA, MLA_HEADS * (QK_NOPE + QK_ROPE)), Q_LORA),
        'w_ukv': dense((no, KV_LORA, MLA_HEADS * (QK_NOPE + V_HEAD)), KV_LORA),
        'g_qn': gain((no, QK_NOPE)),
        'g_qr': gain((no, QK_ROPE)),
        'g_kn': gain((no, QK_NOPE)),
        'g_kr': gain((no, QK_ROPE)),
        'w_o': dense((no, MLA_HEADS * V_HEAD, D_MODEL), MLA_HEADS * V_HEAD, out_scale),
        'w_router': dense((no, D_MODEL, N_EXPERTS), D_MODEL),
        'we_gate': dense((no, N_EXPERTS, D_MODEL, D_FF_EXPERT), D_MODEL),
        'we_up': dense((no, N_EXPERTS, D_MODEL, D_FF_EXPERT), D_MODEL),
        'we_down': dense((no, N_EXPERTS, D_FF_EXPERT, D_MODEL), D_FF_EXPERT, out_scale),
    }


def reference(x, positions, g_mix_norm, g_ffn_norm, w_in, conv_w, b_gates, g_mlstm_out, g_ret_out,
              w_mix_out, ffn_gate, ffn_up, ffn_down, w_dqkv, g_cq, g_ckv, w_uq, w_ukv, g_qn, g_qr,
              g_kn, g_kr, w_o, w_router, we_gate, we_up, we_down):
    for layer in range(DEPTH):
        j = layer // 2
        h = rmsnorm(x, g_mix_norm[layer])
        if layer % 2 == 0:
            x = x + recurrent_hybrid_mixer(h, positions, w_in[j], conv_w[j], b_gates[j], g_mlstm_out[j],
                                           g_ret_out[j], w_mix_out[j]).astype(x.dtype)
            h = rmsnorm(x, g_ffn_norm[layer])
            x = x + swiglu(h, ffn_gate[j], ffn_up[j], ffn_down[j]).astype(x.dtype)
        else:
            x = x + mla_mixer(h, positions, w_dqkv[j], g_cq[j], g_ckv[j], w_uq[j], w_ukv[j], g_qn[j],
                              g_qr[j], g_kn[j], g_kr[j], w_o[j]).astype(x.dtype)
            h = rmsnorm(x, g_ffn_norm[layer])
            x = x + moe_swiglu(h, w_router[j], we_gate[j], we_up[j], we_down[j]).astype(x.dtype)
    return x
```

```python
import functools

import jax
import jax.numpy as jnp
from jax import lax
from jax.experimental import pallas as pl
from jax.experimental.pallas import tpu as pltpu

EPS = 1e-6
ROPE_BASE = 10000.0
CONV_WIDTH = 4
MLSTM_HEADS = 4
RET_HEADS = 4
HEAD_DIM = 256
REC_CHUNK = 256
MLA_HEADS = 16
Q_LORA = 512
KV_LORA = 512
QK_NOPE = 128
QK_ROPE = 64
V_HEAD = 128
N_EXPERTS = 8
TOP_K = 2

LANES = 128
SUBLANES = 8
VMEM_LIMIT_BYTES = 48 * 1024 * 1024

ROW_TILE = 512
MOE_ROW_TILE = 256
ATTN_TILE = 512

_F32 = jnp.float32
_BF16 = jnp.bfloat16
_NT = (((1,), (1,)), ((), ()))
_TN = (((0,), (0,)), ((), ()))


def _params(*semantics):
    return pltpu.CompilerParams(dimension_semantics=semantics, vmem_limit_bytes=VMEM_LIMIT_BYTES)


def _sigmoid(x):
    return 1.0 / (1.0 + jnp.exp(-x))


def _rms(x, g):
    return x * lax.rsqrt(jnp.mean(x * x, axis=-1, keepdims=True) + EPS) * g


def _rmsnorm_kernel(x_ref, g_ref, o_ref):
    o_ref[...] = _rms(x_ref[...], g_ref[...]).astype(o_ref.dtype)


def rmsnorm_bf16(x, g):
    n, d = x.shape
    return pl.pallas_call(
        _rmsnorm_kernel,
        out_shape=jax.ShapeDtypeStruct((n, d), _BF16),
        grid=(n // ROW_TILE,),
        in_specs=[pl.BlockSpec((ROW_TILE, d), lambda i: (i, 0)),
                  pl.BlockSpec((1, d), lambda i: (0, 0))],
        out_specs=pl.BlockSpec((ROW_TILE, d), lambda i: (i, 0)),
        compiler_params=_params("parallel"),
        name="rmsnorm",
    )(x, g.reshape(1, d))


def _gmm_kernel(te_ref, nu_ref, a_ref, w_ref, *rest, has_scale, has_res):
    del te_ref
    o_ref = rest[-1]

    @pl.when(pl.program_id(1) < nu_ref[0])
    def _():
        acc = jnp.dot(a_ref[...], w_ref[...], preferred_element_type=_F32)
        k = 0
        if has_scale:
            acc = acc * rest[k][...]
            k += 1
        if has_res:
            acc = acc + rest[k][...]
        o_ref[...] = acc.astype(o_ref.dtype)

    @pl.when(pl.program_id(1) >= nu_ref[0])
    def _():
        o_ref[...] = jnp.zeros_like(o_ref)


def grouped_matmul(a, w, tile_expert, n_used, *, tm, tn, out_dtype, row_scale=None, residual=None):
    m, k = a.shape
    _, _, n = w.shape
    n_tiles = m // tm

    def row(j, i, te, nu):
        return jnp.minimum(i, nu[0] - 1)

    in_specs = [pl.BlockSpec((tm, k), lambda j, i, te, nu: (row(j, i, te, nu), 0)),
                pl.BlockSpec((None, k, tn), lambda j, i, te, nu: (te[row(j, i, te, nu)], 0, j))]
    args = [a, w]
    if row_scale is not None:
        in_specs.append(pl.BlockSpec((tm, 1), lambda j, i, te, nu: (row(j, i, te, nu), 0)))
        args.append(row_scale)
    if residual is not None:
        in_specs.append(pl.BlockSpec((tm, tn), lambda j, i, te, nu: (row(j, i, te, nu), j)))
        args.append(residual)
    return pl.pallas_call(
        functools.partial(_gmm_kernel, has_scale=row_scale is not None, has_res=residual is not None),
        out_shape=jax.ShapeDtypeStruct((m, n), out_dtype),
        grid_spec=pltpu.PrefetchScalarGridSpec(
            num_scalar_prefetch=2,
            grid=(n // tn, n_tiles),
            in_specs=in_specs,
            out_specs=pl.BlockSpec((tm, tn), lambda j, i, te, nu: (i, j)),
        ),
        compiler_params=_params("arbitrary", "arbitrary"),
        name="grouped_matmul",
    )(tile_expert, n_used, *args)


def _gswiglu_kernel(te_ref, nu_ref, a_ref, wg_ref, wu_ref, o_ref):
    del te_ref

    @pl.when(pl.program_id(1) < nu_ref[0])
    def _():
        a = a_ref[...]
        g = jnp.dot(a, wg_ref[...], preferred_element_type=_F32)
        u = jnp.dot(a, wu_ref[...], preferred_element_type=_F32)
        o_ref[...] = (g * _sigmoid(g) * u).astype(o_ref.dtype)

    @pl.when(pl.program_id(1) >= nu_ref[0])
    def _():
        o_ref[...] = jnp.zeros_like(o_ref)


def grouped_swiglu_up(a, wg, wu, tile_expert, n_used, *, tm, tn):
    m, k = a.shape
    _, _, n = wg.shape

    def row(j, i, te, nu):
        return jnp.minimum(i, nu[0] - 1)

    wspec = pl.BlockSpec((None, k, tn), lambda j, i, te, nu: (te[row(j, i, te, nu)], 0, j))
    return pl.pallas_call(
        _gswiglu_kernel,
        out_shape=jax.ShapeDtypeStruct((m, n), _BF16),
        grid_spec=pltpu.PrefetchScalarGridSpec(
            num_scalar_prefetch=2,
            grid=(n // tn, m // tm),
            in_specs=[pl.BlockSpec((tm, k), lambda j, i, te, nu: (row(j, i, te, nu), 0)), wspec, wspec],
            out_specs=pl.BlockSpec((tm, tn), lambda j, i, te, nu: (i, j)),
        ),
        compiler_params=_params("arbitrary", "arbitrary"),
        name="grouped_swiglu_up",
    )(tile_expert, n_used, a, wg, wu)


def _dense_plan(m, tm):
    n_tiles = m // tm
    return jnp.zeros((n_tiles,), jnp.int32), jnp.full((1,), n_tiles, jnp.int32)


def dense_matmul(a, w, *, tn, out_dtype, residual=None):
    te, nu = _dense_plan(a.shape[0], ROW_TILE)
    return grouped_matmul(a, w[None], te, nu, tm=ROW_TILE, tn=tn, out_dtype=out_dtype, residual=residual)


def _recurrent_kernel(q_ref, k_ref, v_ref, o_ref, rq_ref, rk_ref, rv_ref, rg_ref,
                      gt_ref, gc_ref, cos_ref, sin_ref, cwq_ref, cwk_ref, gm_ref, gr_ref, lg_ref,
                      hm_ref, hr_ref,
                      c_s, n_s, m_s, r_s, qbuf, kbuf):
    L, dh = q_ref.shape
    halo = SUBLANES
    inv_sqrt_d = dh ** -0.5

    @pl.when(pl.program_id(2) == 0)
    def _():
        c_s[...] = jnp.zeros_like(c_s)
        n_s[...] = jnp.zeros_like(n_s)
        m_s[...] = jnp.full_like(m_s, -jnp.inf)
        r_s[...] = jnp.zeros_like(r_s)
        qbuf[0:halo, :] = jnp.zeros((halo, dh), _F32)
        kbuf[0:halo, :] = jnp.zeros((halo, dh), _F32)

    def conv_silu(x_ref, buf, w_ref):
        buf[halo:halo + L, :] = x_ref[...]
        w = w_ref[...]
        y = w[CONV_WIDTH - 1:CONV_WIDTH, :] * buf[halo:halo + L, :]
        for j in range(CONV_WIDTH - 1):
            off = halo - (CONV_WIDTH - 1) + j
            y = y + w[j:j + 1, :] * buf[off:off + L, :]
        buf[0:halo, :] = buf[L:L + halo, :]
        return y * _sigmoid(y)

    row = lax.broadcasted_iota(jnp.int32, (L, L), 0)
    col = lax.broadcasted_iota(jnp.int32, (L, L), 1)
    causal = col <= row

    q = conv_silu(q_ref, qbuf, cwq_ref)
    k = conv_silu(k_ref, kbuf, cwk_ref)
    qb = q.astype(_BF16)
    vb = v_ref[...].astype(_BF16)

    def log_sigmoid(x):
        return jnp.minimum(x, 0.0) - jnp.log1p(jnp.exp(-jnp.abs(x)))

    gt = gt_ref[...]
    gc = gc_ref[...]
    i_row = gt[0:1, :]
    f_row = log_sigmoid(gt[1:2, :])
    i_col = gc[:, 0:1]
    f_col = log_sigmoid(gc[:, 1:2])
    hi = lax.Precision.HIGHEST
    b_col = jnp.dot(causal.astype(_F32), f_col, precision=hi, preferred_element_type=_F32)
    b_row = jnp.dot(f_row, (row <= col).astype(_F32), precision=hi, preferred_element_type=_F32)
    g_tot = b_col[L - 1:L, :]
    m_prev = m_s[...]

    log_d = jnp.where(causal, b_col - b_row + i_row, -jnp.inf)
    m_inter = b_col + m_prev
    m_t = jnp.maximum(jnp.max(log_d, axis=1, keepdims=True), m_inter)
    d_m = jnp.exp(log_d - m_t)
    inter = jnp.exp(m_inter - m_t)
    s = lax.dot_general(qb, k.astype(_BF16), _NT, preferred_element_type=_F32) * inv_sqrt_d
    s_m = s * d_m
    num = (jnp.dot(s_m.astype(_BF16), vb, preferred_element_type=_F32)
           + inter * jnp.dot(qb, c_s[...].astype(_BF16), preferred_element_type=_F32))
    den = jnp.sum(s_m, axis=1, keepdims=True) + inter * jnp.sum(q * n_s[...], axis=1, keepdims=True)
    h = num / jnp.maximum(jnp.abs(den), jnp.exp(-m_t))

    log_w = g_tot - b_col + i_col
    m_new = jnp.maximum(g_tot + m_prev, jnp.max(log_w, axis=0, keepdims=True))
    w_col = jnp.exp(log_w - m_new)
    decay = jnp.exp(g_tot + m_prev - m_new)
    kw = k * (w_col * inv_sqrt_d)
    c_s[...] = decay * c_s[...] + lax.dot_general(kw.astype(_BF16), vb, _TN, preferred_element_type=_F32)
    n_s[...] = decay * n_s[...] + jnp.sum(kw, axis=0, keepdims=True)
    m_s[...] = m_new

    hm_ref[...] = _rms(_sigmoid(o_ref[...]) * h, gm_ref[...]).astype(hm_ref.dtype)

    half = dh // 2
    cos = cos_ref[...]
    sin = sin_ref[...]

    def rope(x):
        x1 = x[:, :half]
        x2 = x[:, half:]
        return jnp.concatenate([x1 * cos - x2 * sin, x2 * cos + x1 * sin], axis=1)

    rq = rope(rq_ref[...]).astype(_BF16)
    rk = rope(rk_ref[...])
    rvb = rv_ref[...].astype(_BF16)
    lg = lg_ref[...]
    dist = (row - col).astype(_F32)
    d_r = jnp.where(causal, jnp.exp(lg * jnp.maximum(dist, 0.0)), 0.0)
    t_col = lax.broadcasted_iota(jnp.int32, (L, 1), 0).astype(_F32)
    q_decay = jnp.exp(lg * (t_col + 1.0))
    k_decay = jnp.exp(lg * (L - 1.0 - t_col))
    chunk_decay = jnp.exp(lg * L)
    s_r = lax.dot_general(rq, rk.astype(_BF16), _NT, preferred_element_type=_F32) * inv_sqrt_d * d_r
    out_r = (jnp.dot(s_r.astype(_BF16), rvb, preferred_element_type=_F32)
             + jnp.dot(rq, r_s[...].astype(_BF16), preferred_element_type=_F32) * q_decay)
    r_s[...] = chunk_decay * r_s[...] + lax.dot_general(
        (rk * (k_decay * inv_sqrt_d)).astype(_BF16), rvb, _TN, preferred_element_type=_F32)
    rg = rg_ref[...]
    hr_ref[...] = (rg * _sigmoid(rg) * _rms(out_r, gr_ref[...])).astype(hr_ref.dtype)


def recurrent_mixer(proj, gates, cos, sin, conv_w, g_mlstm_out, g_ret_out, log_gamma):
    B, S, _ = proj.shape
    H, dh, L = MLSTM_HEADS, HEAD_DIM, REC_CHUNK
    g4 = gates.reshape(B, S, 2, H)
    gt = g4.transpose(0, 3, 2, 1)
    gc = g4.transpose(0, 3, 1, 2)

    def pspec(group):
        return pl.BlockSpec((None, L, dh), lambda b, h, c: (b, c, group * H + h))

    in_specs = [pspec(g) for g in range(8)] + [
        pl.BlockSpec((None, None, 2, L), lambda b, h, c: (b, h, 0, c)),
        pl.BlockSpec((None, None, L, 2), lambda b, h, c: (b, h, c, 0)),
        pl.BlockSpec((None, L, dh // 2), lambda b, h, c: (b, c, 0)),
        pl.BlockSpec((None, L, dh // 2), lambda b, h, c: (b, c, 0)),
        pl.BlockSpec((CONV_WIDTH, dh), lambda b, h, c: (0, h)),
        pl.BlockSpec((CONV_WIDTH, dh), lambda b, h, c: (0, H + h)),
        pl.BlockSpec((1, dh), lambda b, h, c: (0, h)),
        pl.BlockSpec((1, dh), lambda b, h, c: (0, h)),
        pl.BlockSpec((None, 1, 1), lambda b, h, c: (h, 0, 0)),
    ]
    out_specs = [pl.BlockSpec((None, L, dh), lambda b, h, c: (b, c, h)),
                 pl.BlockSpec((None, L, dh), lambda b, h, c: (b, c, h))]
    hm, hr = pl.pallas_call(
        _recurrent_kernel,
        out_shape=[jax.ShapeDtypeStruct((B, S, H * dh), _BF16)] * 2,
        grid=(B, H, S // L),
        in_specs=in_specs,
        out_specs=out_specs,
        scratch_shapes=[pltpu.VMEM((dh, dh), _F32), pltpu.VMEM((1, dh), _F32), pltpu.VMEM((1, 1), _F32),
                        pltpu.VMEM((dh, dh), _F32),
                        pltpu.VMEM((L + SUBLANES, dh), _F32), pltpu.VMEM((L + SUBLANES, dh), _F32)],
        compiler_params=_params("parallel", "parallel", "arbitrary"),
        name="recurrent_mixer",
    )(*([proj] * 8), gt, gc, cos, sin, conv_w, conv_w,
      g_mlstm_out.reshape(1, H * dh), g_ret_out.reshape(1, H * dh), log_gamma.reshape(H, 1, 1))
    return hm, hr


def _rope_pair(z2, cs, gvec):
    lane = lax.broadcasted_iota(jnp.int32, z2.shape, 1)
    first = lane < QK_ROPE
    ms = jnp.sum(jnp.where(first, z2 * z2, 0.0), axis=1, keepdims=True) * (1.0 / QK_ROPE)
    t = z2 * lax.rsqrt(ms + EPS) * (cs * gvec)
    return jnp.where(first, t + pltpu.roll(t, QK_ROPE, 1), 0.0)


def _mla_q_kernel(c_ref, gcq_ref, w_ref, gqn_ref, gqr_ref, cs_ref, q_ref, cn_s, *, scale):
    @pl.when(pl.program_id(1) == 0)
    def _():
        cn_s[...] = _rms(c_ref[...], gcq_ref[...]).astype(_BF16)

    z = jnp.dot(cn_s[...], w_ref[...], preferred_element_type=_F32)
    qn = _rms(z[:, :QK_NOPE], gqn_ref[...])
    qr = _rope_pair(z[:, QK_NOPE:], cs_ref[...], gqr_ref[...])
    q_ref[...] = (jnp.concatenate([qn, qr], axis=1) * scale).astype(q_ref.dtype)


def _mla_kv_kernel(c_ref, kr_ref, gckv_ref, w_ref, gkn_ref, gkr_ref, cs_ref, k_ref, v_ref, cn_s):
    @pl.when(pl.program_id(1) == 0)
    def _():
        cn_s[...] = _rms(c_ref[...], gckv_ref[...]).astype(_BF16)

    z = jnp.dot(cn_s[...], w_ref[...], preferred_element_type=_F32)
    kn = _rms(z[:, :QK_NOPE], gkn_ref[...])
    kr = _rope_pair(kr_ref[...], cs_ref[...], gkr_ref[...])
    k_ref[...] = jnp.concatenate([kn, kr], axis=1).astype(k_ref.dtype)
    v_ref[...] = z[:, QK_NOPE:].astype(v_ref.dtype)


def _mla_attn_kernel(q_ref, k_ref, v_ref, o_ref, m_s, l_s, acc_s):
    qi = pl.program_id(2)
    tq = q_ref.shape[0]
    q = q_ref[...]
    m_s[...] = jnp.full_like(m_s, -jnp.inf)
    l_s[...] = jnp.zeros_like(l_s)
    acc_s[...] = jnp.zeros_like(acc_s)

    def step(j, diagonal):
        start = pl.multiple_of(j * tq, tq)
        kb = k_ref[pl.ds(start, tq), :]
        vb = v_ref[pl.ds(start, tq), :]
        s = lax.dot_general(q, kb, _NT, preferred_element_type=_F32)
        if diagonal:
            row = lax.broadcasted_iota(jnp.int32, s.shape, 0)
            col = lax.broadcasted_iota(jnp.int32, s.shape, 1)
            s = jnp.where(col <= row, s, -jnp.inf)
        m_prev = m_s[...]
        m_new = jnp.maximum(m_prev, jnp.max(s, axis=1, keepdims=True))
        p = jnp.exp(s - m_new)
        alpha = jnp.exp(m_prev - m_new)
        l_s[...] = alpha * l_s[...] + jnp.sum(p, axis=1, keepdims=True)
        acc_s[...] = alpha * acc_s[...] + jnp.dot(p.astype(_BF16), vb, preferred_element_type=_F32)
        m_s[...] = m_new

    def body(j, carry):
        step(j, False)
        return carry

    lax.fori_loop(0, qi, body, 0)
    step(qi, True)
    o_ref[...] = (acc_s[...] / l_s[...]).astype(o_ref.dtype)


def mla_attention(c, cs, w_q, w_kv, g_cq, g_ckv, g_qn, gv_q, g_kn, gv_k, B, S):
    T = c.shape[0]
    H, tm = MLA_HEADS, ROW_TILE
    dq = QK_NOPE + 2 * QK_ROPE
    scale = (QK_NOPE + QK_ROPE) ** -0.5
    nt = S // tm
    vec = lambda n: pl.BlockSpec((1, n), lambda i, h: (0, 0))
    q = pl.pallas_call(
        functools.partial(_mla_q_kernel, scale=scale),
        out_shape=jax.ShapeDtypeStruct((B, H, S, dq), _BF16),
        grid=(T // tm, H),
        in_specs=[pl.BlockSpec((tm, Q_LORA), lambda i, h: (i, 0)), vec(Q_LORA),
                  pl.BlockSpec((None, Q_LORA, dq), lambda i, h: (h, 0, 0)),
                  vec(QK_NOPE), vec(LANES),
                  pl.BlockSpec((tm, LANES), lambda i, h: (i, 0))],
        out_specs=pl.BlockSpec((None, None, tm, dq), lambda i, h: (i // nt, h, i % nt, 0)),
        scratch_shapes=[pltpu.VMEM((tm, Q_LORA), _BF16)],
        compiler_params=_params("parallel", "arbitrary"),
        name="mla_q_proj",
    )(c, g_cq.reshape(1, -1), w_q, g_qn.reshape(1, -1), gv_q, cs)
    k, v = pl.pallas_call(
        _mla_kv_kernel,
        out_shape=[jax.ShapeDtypeStruct((B, H, S, dq), _BF16), jax.ShapeDtypeStruct((B, H, S, V_HEAD), _BF16)],
        grid=(T // tm, H),
        in_specs=[pl.BlockSpec((tm, KV_LORA), lambda i, h: (i, 1)),
                  pl.BlockSpec((tm, LANES), lambda i, h: (i, (Q_LORA + KV_LORA) // LANES)),
                  vec(KV_LORA),
                  pl.BlockSpec((KV_LORA, QK_NOPE + V_HEAD), lambda i, h: (0, h)),
                  vec(QK_NOPE), vec(LANES),
                  pl.BlockSpec((tm, LANES), lambda i, h: (i, 0))],
        out_specs=[pl.BlockSpec((None, None, tm, dq), lambda i, h: (i // nt, h, i % nt, 0)),
                   pl.BlockSpec((None, None, tm, V_HEAD), lambda i, h: (i // nt, h, i % nt, 0))],
        scratch_shapes=[pltpu.VMEM((tm, KV_LORA), _BF16)],
        compiler_params=_params("parallel", "arbitrary"),
        name="mla_kv_proj",
    )(c, c, g_ckv.reshape(1, -1), w_kv, g_kn.reshape(1, -1), gv_k, cs)
    tq = ATTN_TILE
    o = pl.pallas_call(
        _mla_attn_kernel,
        out_shape=jax.ShapeDtypeStruct((B, S, H * V_HEAD), _BF16),
        grid=(B, H, S // tq),
        in_specs=[pl.BlockSpec((None, None, tq, dq), lambda b, h, i: (b, h, i, 0)),
                  pl.BlockSpec((None, None, S, dq), lambda b, h, i: (b, h, 0, 0)),
                  pl.BlockSpec((None, None, S, V_HEAD), lambda b, h, i: (b, h, 0, 0))],
        out_specs=pl.BlockSpec((None, tq, V_HEAD), lambda b, h, i: (b, i, h)),
        scratch_shapes=[pltpu.VMEM((tq, 1), _F32), pltpu.VMEM((tq, 1), _F32), pltpu.VMEM((tq, V_HEAD), _F32)],
        compiler_params=_params("parallel", "parallel", "arbitrary"),
        name="mla_attention",
    )(q, k, v)
    return o.reshape(T, H * V_HEAD)


def _router_kernel(x_ref, g_ref, w_ref, r_ref):
    hn = _rms(x_ref[...], g_ref[...])
    logits = jnp.dot(hn, w_ref[...], precision=lax.Precision.HIGHEST, preferred_element_type=_F32)
    lane = lax.broadcasted_iota(jnp.int32, logits.shape, 1)
    lane_f = lane.astype(_F32)
    neg = -jnp.inf
    lg = jnp.where(lane < N_EXPERTS, logits, neg)
    v1 = jnp.max(lg, axis=1, keepdims=True)
    i1 = jnp.min(jnp.where(lg == v1, lane_f, float(LANES)), axis=1, keepdims=True)
    lg2 = jnp.where(lane_f == i1, neg, lg)
    v2 = jnp.max(lg2, axis=1, keepdims=True)
    i2 = jnp.min(jnp.where(lg2 == v2, lane_f, float(LANES)), axis=1, keepdims=True)
    e2 = jnp.exp(v2 - v1)
    w1 = 1.0 / (1.0 + e2)
    w2 = e2 / (1.0 + e2)
    out = jnp.where(lane < N_EXPERTS, logits, 0.0)
    out = jnp.where(lane == N_EXPERTS, i1, out)
    out = jnp.where(lane == N_EXPERTS + 1, i2, out)
    out = jnp.where(lane == N_EXPERTS + 2, w1, out)
    out = jnp.where(lane == N_EXPERTS + 3, w2, out)
    r_ref[...] = out


def router(x, g, w_router_padded):
    T, d = x.shape
    return pl.pallas_call(
        _router_kernel,
        out_shape=jax.ShapeDtypeStruct((T, LANES), _F32),
        grid=(T // ROW_TILE,),
        in_specs=[pl.BlockSpec((ROW_TILE, d), lambda i: (i, 0)),
                  pl.BlockSpec((1, d), lambda i: (0, 0)),
                  pl.BlockSpec((d, LANES), lambda i: (0, 0))],
        out_specs=pl.BlockSpec((ROW_TILE, LANES), lambda i: (i, 0)),
        compiler_params=_params("parallel"),
        name="moe_router",
    )(x, g.reshape(1, d), w_router_padded)


def _row_copy(src_hbm, dst_vmem, sem, src_row, dst_row):
    return pltpu.make_async_copy(src_hbm.at[pl.ds(src_row, 1)], dst_vmem.at[pl.ds(dst_row, 1)], sem)


def _index_spec(n):
    return pl.BlockSpec((None, 1, n), lambda i: (i, 0, 0), memory_space=pltpu.SMEM)


def _gather_norm_kernel(idx_ref, x_hbm, g_ref, o_ref, buf, sem):
    tr = buf.shape[0]

    def start(r, carry):
        _row_copy(x_hbm, buf, sem, idx_ref[0, r], r).start()
        return carry

    def wait(r, carry):
        _row_copy(x_hbm, buf, sem, 0, r).wait()
        return carry

    lax.fori_loop(0, tr, start, 0)
    lax.fori_loop(0, tr, wait, 0)
    o_ref[...] = _rms(buf[...], g_ref[...]).astype(o_ref.dtype)


def gather_norm(x, g, src_rows, tr):
    T, d = x.shape
    r = src_rows.shape[0]
    return pl.pallas_call(
        _gather_norm_kernel,
        out_shape=jax.ShapeDtypeStruct((r, d), _BF16),
        grid=(r // tr,),
        in_specs=[_index_spec(tr), pl.BlockSpec(memory_space=pl.ANY), pl.BlockSpec((1, d), lambda i: (0, 0))],
        out_specs=pl.BlockSpec((tr, d), lambda i: (i, 0)),
        scratch_shapes=[pltpu.VMEM((tr, d), _F32), pltpu.SemaphoreType.DMA],
        compiler_params=_params("arbitrary"),
        name="moe_gather_norm",
    )(src_rows.reshape(r // tr, 1, tr), x, g.reshape(1, d))


def _combine_kernel(pos_ref, x_ref, y_hbm, o_ref, buf_a, buf_b, sem):
    tt = buf_a.shape[0]

    def start(r, carry):
        _row_copy(y_hbm, buf_a, sem, pos_ref[0, TOP_K * r], r).start()
        _row_copy(y_hbm, buf_b, sem, pos_ref[0, TOP_K * r + 1], r).start()
        return carry

    def wait(r, carry):
        _row_copy(y_hbm, buf_a, sem, 0, r).wait()
        _row_copy(y_hbm, buf_b, sem, 0, r).wait()
        return carry

    lax.fori_loop(0, tt, start, 0)
    lax.fori_loop(0, tt, wait, 0)
    o_ref[...] = x_ref[...] + buf_a[...] + buf_b[...]


def moe_combine(x, y, pos_flat, tt):
    T, d = x.shape
    return pl.pallas_call(
        _combine_kernel,
        out_shape=jax.ShapeDtypeStruct((T, d), _F32),
        grid=(T // tt,),
        in_specs=[_index_spec(tt * TOP_K), pl.BlockSpec((tt, d), lambda i: (i, 0)),
                  pl.BlockSpec(memory_space=pl.ANY)],
        out_specs=pl.BlockSpec((tt, d), lambda i: (i, 0)),
        scratch_shapes=[pltpu.VMEM((tt, d), _F32), pltpu.VMEM((tt, d), _F32), pltpu.SemaphoreType.DMA],
        compiler_params=_params("arbitrary"),
        name="moe_combine",
    )(pos_flat.reshape(T // tt, 1, tt * TOP_K), x, y)


def _moe_plan(route, tm, n_tiles):
    T = route.shape[0]
    e_flat = route[:, N_EXPERTS:N_EXPERTS + TOP_K].astype(jnp.int32).reshape(-1)
    w_flat = route[:, N_EXPERTS + TOP_K:N_EXPERTS + 2 * TOP_K].reshape(-1)
    onehot = (e_flat[:, None] == jnp.arange(N_EXPERTS, dtype=jnp.int32)[None, :]).astype(jnp.int32)
    csum = jnp.cumsum(onehot, axis=0)
    rank = jnp.sum(csum * onehot, axis=1) - 1
    tiles_e = (csum[-1] + tm - 1) // tm
    tile_end = jnp.cumsum(tiles_e)
    tile_start = tile_end - tiles_e
    pos = jnp.sum(onehot * tile_start[None, :], axis=1) * tm + rank
    tile_ids = jnp.arange(n_tiles, dtype=jnp.int32)
    tile_expert = jnp.minimum(jnp.sum((tile_ids[:, None] >= tile_end[None, :]).astype(jnp.int32), axis=1),
                              N_EXPERTS - 1).astype(jnp.int32)
    n_used = tile_end[-1:].astype(jnp.int32)
    rows = n_tiles * tm
    src = jnp.zeros((rows,), jnp.int32).at[pos].set(jnp.arange(T * TOP_K, dtype=jnp.int32) // TOP_K)
    scale = jnp.zeros((rows,), _F32).at[pos].set(w_flat)
    return tile_expert, n_used, src, scale.reshape(rows, 1), pos.astype(jnp.int32)


def moe_layer(x, g, w_router, we_gate, we_up, we_down):
    T, d = x.shape
    tm = MOE_ROW_TILE
    n_tiles = (T * TOP_K) // tm + N_EXPERTS
    w_r = jnp.zeros((d, LANES), _F32).at[:, :N_EXPERTS].set(w_router)
    route = router(x, g, w_r)
    tile_expert, n_used, src, scale, pos = _moe_plan(route, tm, n_tiles)
    xs = gather_norm(x, g, src, tm)
    ff = we_gate.shape[-1]
    h = grouped_swiglu_up(xs, we_gate.astype(_BF16), we_up.astype(_BF16), tile_expert, n_used, tm=tm, tn=ff // 4)
    y = grouped_matmul(h, we_down.astype(_BF16), tile_expert, n_used, tm=tm, tn=d // 2, out_dtype=_F32,
                       row_scale=scale)
    return moe_combine(x, y, pos, tm)


def _rope_tables(positions, half):
    inv = ROPE_BASE ** (-jnp.arange(half, dtype=_F32) / half)
    ang = positions.astype(_F32)[..., None] * inv
    return jnp.cos(ang), jnp.sin(ang)


def _rotate_half_cols(w, width):
    lead = w.shape[:-1]
    w2 = w.reshape(lead + (-1, 2, width // 2))
    return jnp.stack([-w2[..., 1, :], w2[..., 0, :]], axis=-2).reshape(w.shape)


def _swap_halves(g):
    half = g.shape[-1] // 2
    return jnp.concatenate([g[..., half:], g[..., :half]], axis=-1)


def kernel(x, positions, g_mix_norm, g_ffn_norm, w_in, conv_w, b_gates, g_mlstm_out, g_ret_out, w_mix_out,
           ffn_gate, ffn_up, ffn_down, w_dqkv, g_cq, g_ckv, w_uq, w_ukv, g_qn, g_qr, g_kn, g_kr, w_o,
           w_router, we_gate, we_up, we_down):
    B, S, D = x.shape
    T = B * S
    xf = x.reshape(T, D)
    n_gate = 2 * MLSTM_HEADS
    main = 4 * MLSTM_HEADS * HEAD_DIM

    w = w_in[0]
    w_main = jnp.concatenate([w[:, :main], w[:, main + n_gate:]], axis=1).astype(_BF16)
    w_gate = jnp.zeros((D, LANES), _BF16).at[:, :n_gate].set(w[:, main:main + n_gate].astype(_BF16))
    hn = rmsnorm_bf16(xf, g_mix_norm[0])
    proj = dense_matmul(hn, w_main, tn=1024, out_dtype=_F32)
    gates = dense_matmul(hn, w_gate, tn=LANES, out_dtype=_F32)[:, :n_gate] + b_gates[0][None, :]
    cos_r, sin_r = _rope_tables(positions, HEAD_DIM // 2)
    log_gamma = jnp.log1p(-jnp.exp2(-5.0 - jnp.arange(RET_HEADS, dtype=_F32)))
    hm, hr = recurrent_mixer(proj.reshape(B, S, -1), gates.reshape(B, S, n_gate), cos_r, sin_r, conv_w[0],
                             g_mlstm_out[0], g_ret_out[0], log_gamma)
    mix = jnp.concatenate([hm, hr], axis=-1).reshape(T, -1)
    xf = dense_matmul(mix, w_mix_out[0].astype(_BF16), tn=1024, out_dtype=_F32, residual=xf)

    hn = rmsnorm_bf16(xf, g_ffn_norm[0])
    te, nu = _dense_plan(T, ROW_TILE)
    ff = ffn_gate.shape[-1]
    hmid = grouped_swiglu_up(hn, ffn_gate.astype(_BF16), ffn_up.astype(_BF16), te, nu, tm=ROW_TILE, tn=ff // 4)
    xf = dense_matmul(hmid, ffn_down[0].astype(_BF16), tn=512, out_dtype=_F32, residual=xf)

    H = MLA_HEADS
    wd = w_dqkv[0]
    w_kr = wd[:, Q_LORA + KV_LORA:]
    wd_full = jnp.concatenate([wd, _rotate_half_cols(w_kr, QK_ROPE)], axis=1).astype(_BF16)
    hn = rmsnorm_bf16(xf, g_mix_norm[1])
    c = dense_matmul(hn, wd_full, tn=wd_full.shape[1] // 3, out_dtype=_F32)
    wq = w_uq[0].reshape(Q_LORA, H, QK_NOPE + QK_ROPE)
    wq_r = wq[..., QK_NOPE:]
    w_q = jnp.concatenate([wq, _rotate_half_cols(wq_r, QK_ROPE)], axis=-1).transpose(1, 0, 2).astype(_BF16)
    cos_m, sin_m = _rope_tables(positions, QK_ROPE // 2)
    cs = jnp.concatenate([cos_m, cos_m, sin_m, sin_m], axis=-1).reshape(T, LANES)
    gv_q = jnp.concatenate([g_qr[0], _swap_halves(g_qr[0])]).reshape(1, LANES)
    gv_k = jnp.concatenate([g_kr[0], _swap_halves(g_kr[0])]).reshape(1, LANES)
    attn = mla_attention(c, cs, w_q, w_ukv[0].astype(_BF16), g_cq[0], g_ckv[0], g_qn[0], gv_q, g_kn[0], gv_k,
                         B, S)
    xf = dense_matmul(attn, w_o[0].astype(_BF16), tn=1024, out_dtype=_F32, residual=xf)

    xf = moe_layer(xf, g_ffn_norm[1], w_router[0], we_gate[0], we_up[0], we_down[0])
    return xf.reshape(B, S, D)
```

```python
import functools

import jax
import jax.numpy as jnp
from jax import lax
from jax.experimental import pallas as pl
from jax.experimental.pallas import tpu as pltpu

EPS = 1e-6
ROPE_BASE = 10000.0
CONV_WIDTH = 4
MLSTM_HEADS = 4
RET_HEADS = 4
HEAD_DIM = 256
REC_CHUNK = 256
MLA_HEADS = 16
Q_LORA = 512
KV_LORA = 512
QK_NOPE = 128
QK_ROPE = 64
V_HEAD = 128
N_EXPERTS = 8
TOP_K = 2

LANES = 128
SUBLANES = 8
VMEM_LIMIT_BYTES = 48 * 1024 * 1024

ROW_TILE = 512
MOE_ROW_TILE = 256
ATTN_TILE = 512
MLA_HEAD_GROUP = 4
LOG2_E = 1.4426950408889634

_F32 = jnp.float32
_BF16 = jnp.bfloat16
_NT = (((1,), (1,)), ((), ()))
_TN = (((0,), (0,)), ((), ()))


def _params(*semantics):
    return pltpu.CompilerParams(dimension_semantics=semantics, vmem_limit_bytes=VMEM_LIMIT_BYTES)


def _sigmoid(x):
    return 1.0 / (1.0 + jnp.exp(-x))


def _rms(x, g):
    return x * lax.rsqrt(jnp.mean(x * x, axis=-1, keepdims=True) + EPS) * g


def _rmsnorm_kernel(x_ref, g_ref, o_ref):
    o_ref[...] = _rms(x_ref[...], g_ref[...]).astype(o_ref.dtype)


def rmsnorm_bf16(x, g):
    n, d = x.shape
    return pl.pallas_call(
        _rmsnorm_kernel,
        out_shape=jax.ShapeDtypeStruct((n, d), _BF16),
        grid=(n // ROW_TILE,),
        in_specs=[pl.BlockSpec((ROW_TILE, d), lambda i: (i, 0)),
                  pl.BlockSpec((1, d), lambda i: (0, 0))],
        out_specs=pl.BlockSpec((ROW_TILE, d), lambda i: (i, 0)),
        compiler_params=_params("parallel"),
        name="rmsnorm",
    )(x, g.reshape(1, d))


def _gmm_kernel(te_ref, nu_ref, a_ref, w_ref, *rest):
    del te_ref
    o_ref = rest[-1]

    @pl.when(pl.program_id(1) < nu_ref[0])
    def _():
        acc = jnp.dot(a_ref[...], w_ref[...], preferred_element_type=_F32)
        if len(rest) == 2:
            acc = acc + rest[0][...]
        o_ref[...] = acc.astype(o_ref.dtype)

    @pl.when(pl.program_id(1) >= nu_ref[0])
    def _():
        o_ref[...] = jnp.zeros_like(o_ref)


def grouped_matmul(a, w, tile_expert, n_used, *, tm, tn, out_dtype, residual=None):
    m, k = a.shape
    _, _, n = w.shape
    n_tiles = m // tm

    def row(j, i, te, nu):
        return jnp.minimum(i, nu[0] - 1)

    in_specs = [pl.BlockSpec((tm, k), lambda j, i, te, nu: (row(j, i, te, nu), 0)),
                pl.BlockSpec((None, k, tn), lambda j, i, te, nu: (te[row(j, i, te, nu)], 0, j))]
    args = [a, w]
    if residual is not None:
        in_specs.append(pl.BlockSpec((tm, tn), lambda j, i, te, nu: (row(j, i, te, nu), j)))
        args.append(residual)
    return pl.pallas_call(
        _gmm_kernel,
        out_shape=jax.ShapeDtypeStruct((m, n), out_dtype),
        grid_spec=pltpu.PrefetchScalarGridSpec(
            num_scalar_prefetch=2,
            grid=(n // tn, n_tiles),
            in_specs=in_specs,
            out_specs=pl.BlockSpec((tm, tn), lambda j, i, te, nu: (i, j)),
        ),
        compiler_params=_params("arbitrary", "arbitrary"),
        name="grouped_matmul",
    )(tile_expert, n_used, *args)


def _gswiglu_kernel(te_ref, nu_ref, a_ref, wg_ref, wu_ref, o_ref):
    del te_ref

    @pl.when(pl.program_id(1) < nu_ref[0])
    def _():
        a = a_ref[...]
        g = jnp.dot(a, wg_ref[...], preferred_element_type=_F32)
        u = jnp.dot(a, wu_ref[...], preferred_element_type=_F32)
        o_ref[...] = (g * _sigmoid(g) * u).astype(o_ref.dtype)

    @pl.when(pl.program_id(1) >= nu_ref[0])
    def _():
        o_ref[...] = jnp.zeros_like(o_ref)


def grouped_swiglu_up(a, wg, wu, tile_expert, n_used, *, tm, tn):
    m, k = a.shape
    _, _, n = wg.shape

    def row(j, i, te, nu):
        return jnp.minimum(i, nu[0] - 1)

    wspec = pl.BlockSpec((None, k, tn), lambda j, i, te, nu: (te[row(j, i, te, nu)], 0, j))
    return pl.pallas_call(
        _gswiglu_kernel,
        out_shape=jax.ShapeDtypeStruct((m, n), _BF16),
        grid_spec=pltpu.PrefetchScalarGridSpec(
            num_scalar_prefetch=2,
            grid=(n // tn, m // tm),
            in_specs=[pl.BlockSpec((tm, k), lambda j, i, te, nu: (row(j, i, te, nu), 0)), wspec, wspec],
            out_specs=pl.BlockSpec((tm, tn), lambda j, i, te, nu: (i, j)),
        ),
        compiler_params=_params("arbitrary", "arbitrary"),
        name="grouped_swiglu_up",
    )(tile_expert, n_used, a, wg, wu)


def _dense_plan(m, tm):
    n_tiles = m // tm
    return jnp.zeros((n_tiles,), jnp.int32), jnp.full((1,), n_tiles, jnp.int32)


def dense_matmul(a, w, *, tn, out_dtype, residual=None):
    te, nu = _dense_plan(a.shape[0], ROW_TILE)
    return grouped_matmul(a, w[None], te, nu, tm=ROW_TILE, tn=tn, out_dtype=out_dtype, residual=residual)


def _recurrent_kernel(q_ref, k_ref, v_ref, o_ref, rq_ref, rk_ref, rv_ref, rg_ref,
                      gt_ref, gc_ref, cos_ref, sin_ref, cwq_ref, cwk_ref, gm_ref, gr_ref, lg_ref,
                      mix_ref,
                      c_s, n_s, m_s, r_s, qbuf, kbuf):
    L, dh = q_ref.shape
    halo = SUBLANES
    inv_sqrt_d = dh ** -0.5

    @pl.when(pl.program_id(2) == 0)
    def _():
        c_s[...] = jnp.zeros_like(c_s)
        n_s[...] = jnp.zeros_like(n_s)
        m_s[...] = jnp.full_like(m_s, -jnp.inf)
        r_s[...] = jnp.zeros_like(r_s)
        qbuf[0:halo, :] = jnp.zeros((halo, dh), _F32)
        kbuf[0:halo, :] = jnp.zeros((halo, dh), _F32)

    def conv_silu(x_ref, buf, w_ref):
        buf[halo:halo + L, :] = x_ref[...]
        w = w_ref[...]
        y = w[CONV_WIDTH - 1:CONV_WIDTH, :] * buf[halo:halo + L, :]
        for j in range(CONV_WIDTH - 1):
            off = halo - (CONV_WIDTH - 1) + j
            y = y + w[j:j + 1, :] * buf[off:off + L, :]
        buf[0:halo, :] = buf[L:L + halo, :]
        return y * _sigmoid(y)

    row = lax.broadcasted_iota(jnp.int32, (L, L), 0)
    col = lax.broadcasted_iota(jnp.int32, (L, L), 1)
    causal = col <= row

    q = conv_silu(q_ref, qbuf, cwq_ref)
    k = conv_silu(k_ref, kbuf, cwk_ref)
    qb = q.astype(_BF16)
    vb = v_ref[...].astype(_BF16)

    def log_sigmoid(x):
        return jnp.minimum(x, 0.0) - jnp.log1p(jnp.exp(-jnp.abs(x)))

    gt = gt_ref[...]
    gc = gc_ref[...]
    i_row = gt[0:1, :]
    f_row = log_sigmoid(gt[1:2, :])
    i_col = gc[:, 0:1]
    f_col = log_sigmoid(gc[:, 1:2])
    hi = lax.Precision.HIGHEST
    b_col = jnp.dot(causal.astype(_F32), f_col, precision=hi, preferred_element_type=_F32)
    b_row = jnp.dot(f_row, (row <= col).astype(_F32), precision=hi, preferred_element_type=_F32)
    g_tot = b_col[L - 1:L, :]
    m_prev = m_s[...]

    log_d = jnp.where(causal, b_col - b_row + i_row, -jnp.inf)
    m_inter = b_col + m_prev
    m_t = jnp.maximum(jnp.max(log_d, axis=1, keepdims=True), m_inter)
    d_m = jnp.exp(log_d - m_t)
    inter = jnp.exp(m_inter - m_t)
    s = lax.dot_general(qb, k.astype(_BF16), _NT, preferred_element_type=_F32) * inv_sqrt_d
    s_m = s * d_m
    num = (jnp.dot(s_m.astype(_BF16), vb, preferred_element_type=_F32)
           + inter * jnp.dot(qb, c_s[...].astype(_BF16), preferred_element_type=_F32))
    den = jnp.sum(s_m, axis=1, keepdims=True) + inter * jnp.sum(q * n_s[...], axis=1, keepdims=True)
    h = num / jnp.maximum(jnp.abs(den), jnp.exp(-m_t))

    log_w = g_tot - b_col + i_col
    m_new = jnp.maximum(g_tot + m_prev, jnp.max(log_w, axis=0, keepdims=True))
    w_col = jnp.exp(log_w - m_new)
    decay = jnp.exp(g_tot + m_prev - m_new)
    kw = k * (w_col * inv_sqrt_d)
    c_s[...] = decay * c_s[...] + lax.dot_general(kw.astype(_BF16), vb, _TN, preferred_element_type=_F32)
    n_s[...] = decay * n_s[...] + jnp.sum(kw, axis=0, keepdims=True)
    m_s[...] = m_new

    mix_ref[:, :dh] = _rms(_sigmoid(o_ref[...]) * h, gm_ref[...]).astype(mix_ref.dtype)

    half = dh // 2
    cos = cos_ref[...]
    sin = sin_ref[...]

    def rope(x):
        x1 = x[:, :half]
        x2 = x[:, half:]
        return jnp.concatenate([x1 * cos - x2 * sin, x2 * cos + x1 * sin], axis=1)

    rq = rope(rq_ref[...]).astype(_BF16)
    rk = rope(rk_ref[...])
    rvb = rv_ref[...].astype(_BF16)
    lg = lg_ref[...]
    dist = (row - col).astype(_F32)
    d_r = jnp.where(causal, jnp.exp(lg * jnp.maximum(dist, 0.0)), 0.0)
    t_col = lax.broadcasted_iota(jnp.int32, (L, 1), 0).astype(_F32)
    q_decay = jnp.exp(lg * (t_col + 1.0))
    k_decay = jnp.exp(lg * (L - 1.0 - t_col))
    chunk_decay = jnp.exp(lg * L)
    s_r = lax.dot_general(rq, rk.astype(_BF16), _NT, preferred_element_type=_F32) * inv_sqrt_d * d_r
    out_r = (jnp.dot(s_r.astype(_BF16), rvb, preferred_element_type=_F32)
             + jnp.dot(rq, r_s[...].astype(_BF16), preferred_element_type=_F32) * q_decay)
    r_s[...] = chunk_decay * r_s[...] + lax.dot_general(
        (rk * (k_decay * inv_sqrt_d)).astype(_BF16), rvb, _TN, preferred_element_type=_F32)
    rg = rg_ref[...]
    mix_ref[:, dh:] = (rg * _sigmoid(rg) * _rms(out_r, gr_ref[...])).astype(mix_ref.dtype)


def recurrent_mixer(proj, gates, cos, sin, conv_w, g_mlstm_out, g_ret_out, log_gamma):
    B, S, _ = proj.shape
    H, dh, L = MLSTM_HEADS, HEAD_DIM, REC_CHUNK
    g4 = gates.reshape(B, S, 2, H)
    gt = g4.transpose(0, 3, 2, 1)
    gc = g4.transpose(0, 3, 1, 2)

    def pspec(group):
        return pl.BlockSpec((None, L, dh), lambda b, h, c: (b, c, group * H + h))

    in_specs = [pspec(g) for g in range(8)] + [
        pl.BlockSpec((None, None, 2, L), lambda b, h, c: (b, h, 0, c)),
        pl.BlockSpec((None, None, L, 2), lambda b, h, c: (b, h, c, 0)),
        pl.BlockSpec((None, L, dh // 2), lambda b, h, c: (b, c, 0)),
        pl.BlockSpec((None, L, dh // 2), lambda b, h, c: (b, c, 0)),
        pl.BlockSpec((CONV_WIDTH, dh), lambda b, h, c: (0, h)),
        pl.BlockSpec((CONV_WIDTH, dh), lambda b, h, c: (0, H + h)),
        pl.BlockSpec((1, dh), lambda b, h, c: (0, h)),
        pl.BlockSpec((1, dh), lambda b, h, c: (0, h)),
        pl.BlockSpec((None, 1, 1), lambda b, h, c: (h, 0, 0)),
    ]
    return pl.pallas_call(
        _recurrent_kernel,
        out_shape=jax.ShapeDtypeStruct((B, S, 2 * H * dh), _BF16),
        grid=(B, H, S // L),
        in_specs=in_specs,
        out_specs=pl.BlockSpec((None, L, 2 * dh), lambda b, h, c: (b, c, h)),
        scratch_shapes=[pltpu.VMEM((dh, dh), _F32), pltpu.VMEM((1, dh), _F32), pltpu.VMEM((1, 1), _F32),
                        pltpu.VMEM((dh, dh), _F32),
                        pltpu.VMEM((L + SUBLANES, dh), _F32), pltpu.VMEM((L + SUBLANES, dh), _F32)],
        compiler_params=_params("parallel", "parallel", "arbitrary"),
        name="recurrent_mixer",
    )(*([proj] * 8), gt, gc, cos, sin, conv_w, conv_w,
      g_mlstm_out.reshape(1, H * dh), g_ret_out.reshape(1, H * dh), log_gamma.reshape(H, 1, 1))


def _rope_pair(z2, cs, gvec):
    lane = lax.broadcasted_iota(jnp.int32, z2.shape, 1)
    first = lane < QK_ROPE
    ms = jnp.sum(jnp.where(first, z2 * z2, 0.0), axis=1, keepdims=True) * (1.0 / QK_ROPE)
    t = z2 * lax.rsqrt(ms + EPS) * (cs * gvec)
    return jnp.where(first, t + pltpu.roll(t, QK_ROPE, 1), 0.0)


def _mla_q_kernel(c_ref, gcq_ref, w_ref, gqn_ref, gqr_ref, cs_ref, q_ref, cn_s, *, scale):
    @pl.when(pl.program_id(1) == 0)
    def _():
        cn_s[...] = _rms(c_ref[...], gcq_ref[...]).astype(_BF16)

    dq = q_ref.shape[-1]
    z = jnp.dot(cn_s[...], w_ref[...], preferred_element_type=_F32)
    cs = cs_ref[...]
    for h in range(q_ref.shape[0]):
        zh = z[:, h * dq:(h + 1) * dq]
        qn = _rms(zh[:, :QK_NOPE], gqn_ref[...])
        qr = _rope_pair(zh[:, QK_NOPE:], cs, gqr_ref[...])
        q_ref[h] = (jnp.concatenate([qn, qr], axis=1) * scale).astype(q_ref.dtype)


def _mla_kv_kernel(c_ref, kr_ref, gckv_ref, w_ref, gkn_ref, gkr_ref, cs_ref, k_ref, v_ref, cn_s):
    @pl.when(pl.program_id(1) == 0)
    def _():
        cn_s[...] = _rms(c_ref[...], gckv_ref[...]).astype(_BF16)

    dk = k_ref.shape[-1]
    z = jnp.dot(cn_s[...], w_ref[...], preferred_element_type=_F32)
    kr = _rope_pair(kr_ref[...], cs_ref[...], gkr_ref[...])
    ones = jnp.ones((z.shape[0], LANES), _F32)
    for h in range(k_ref.shape[0]):
        zh = z[:, h * dk:(h + 1) * dk]
        kn = _rms(zh[:, :QK_NOPE], gkn_ref[...])
        k_ref[h] = jnp.concatenate([kn, kr], axis=1).astype(k_ref.dtype)
        v_ref[h] = jnp.concatenate([zh[:, QK_NOPE:], ones], axis=1).astype(v_ref.dtype)


def _mla_attn_kernel(q_ref, k_ref, v_ref, o_ref, m_s, acc_s):
    qi = pl.program_id(2)
    hb, tq, _ = q_ref.shape
    dv = v_ref.shape[-1]
    for h in range(hb):
        m_s[h] = jnp.full((tq, LANES), -jnp.inf, _F32)
        acc_s[h] = jnp.zeros((tq, dv), _F32)

    def block(h, start, diagonal):
        kb = k_ref[h, pl.ds(start, tq), :]
        vb = v_ref[h, pl.ds(start, tq), :]
        s = lax.dot_general(q_ref[h], kb, _NT, preferred_element_type=_F32)
        if diagonal:
            row = lax.broadcasted_iota(jnp.int32, s.shape, 0)
            col = lax.broadcasted_iota(jnp.int32, s.shape, 1)
            s = jnp.where(col <= row, s, -jnp.inf)
        m_prev = m_s[h]
        m_new = jnp.maximum(m_prev, jnp.max(s, axis=1, keepdims=True))
        p = jnp.exp2(s - jnp.concatenate([m_new] * (tq // LANES), axis=1))
        alpha = jnp.exp2(m_prev - m_new)
        acc_s[h] = jnp.concatenate([alpha] * (dv // LANES), axis=1) * acc_s[h] + jnp.dot(
            p.astype(_BF16), vb, preferred_element_type=_F32)
        m_s[h] = m_new

    def body(j, carry):
        for h in range(hb):
            block(h, pl.multiple_of(j * tq, tq), False)
        return carry

    lax.fori_loop(0, qi, body, 0)
    for h in range(hb):
        block(h, pl.multiple_of(qi * tq, tq), True)
    for h in range(hb):
        acc = acc_s[h]
        o_ref[:, h * V_HEAD:(h + 1) * V_HEAD] = (acc[:, :V_HEAD] / acc[:, V_HEAD:]).astype(o_ref.dtype)


def mla_attention(c, cs, w_q, w_kv, g_cq, g_ckv, g_qn, gv_q, g_kn, gv_k, B, S):
    T = c.shape[0]
    H, tm, hb = MLA_HEADS, ROW_TILE, MLA_HEAD_GROUP
    dq = QK_NOPE + 2 * QK_ROPE
    dv = 2 * V_HEAD
    scale = (QK_NOPE + QK_ROPE) ** -0.5 * LOG2_E
    nt = S // tm
    vec = lambda n: pl.BlockSpec((1, n), lambda i, h: (0, 0))
    head_out = lambda d: pl.BlockSpec((None, hb, tm, d), lambda i, h: (i // nt, h, i % nt, 0))
    q = pl.pallas_call(
        functools.partial(_mla_q_kernel, scale=scale),
        out_shape=jax.ShapeDtypeStruct((B, H, S, dq), _BF16),
        grid=(T // tm, H // hb),
        in_specs=[pl.BlockSpec((tm, Q_LORA), lambda i, h: (i, 0)), vec(Q_LORA),
                  pl.BlockSpec((Q_LORA, hb * dq), lambda i, h: (0, h)),
                  vec(QK_NOPE), vec(LANES),
                  pl.BlockSpec((tm, LANES), lambda i, h: (i, 0))],
        out_specs=head_out(dq),
        scratch_shapes=[pltpu.VMEM((tm, Q_LORA), _BF16)],
        compiler_params=_params("parallel", "arbitrary"),
        name="mla_q_proj",
    )(c, g_cq.reshape(1, -1), w_q, g_qn.reshape(1, -1), gv_q, cs)
    k, v = pl.pallas_call(
        _mla_kv_kernel,
        out_shape=[jax.ShapeDtypeStruct((B, H, S, dq), _BF16), jax.ShapeDtypeStruct((B, H, S, dv), _BF16)],
        grid=(T // tm, H // hb),
        in_specs=[pl.BlockSpec((tm, KV_LORA), lambda i, h: (i, 1)),
                  pl.BlockSpec((tm, LANES), lambda i, h: (i, (Q_LORA + KV_LORA) // LANES)),
                  vec(KV_LORA),
                  pl.BlockSpec((KV_LORA, hb * (QK_NOPE + V_HEAD)), lambda i, h: (0, h)),
                  vec(QK_NOPE), vec(LANES),
                  pl.BlockSpec((tm, LANES), lambda i, h: (i, 0))],
        out_specs=[head_out(dq), head_out(dv)],
        scratch_shapes=[pltpu.VMEM((tm, KV_LORA), _BF16)],
        compiler_params=_params("parallel", "arbitrary"),
        name="mla_kv_proj",
    )(c, c, g_ckv.reshape(1, -1), w_kv, g_kn.reshape(1, -1), gv_k, cs)
    tq = ATTN_TILE
    o = pl.pallas_call(
        _mla_attn_kernel,
        out_shape=jax.ShapeDtypeStruct((B, S, H * V_HEAD), _BF16),
        grid=(B, H // hb, S // tq),
        in_specs=[pl.BlockSpec((None, hb, tq, dq), lambda b, h, i: (b, h, i, 0)),
                  pl.BlockSpec((None, hb, S, dq), lambda b, h, i: (b, h, 0, 0)),
                  pl.BlockSpec((None, hb, S, dv), lambda b, h, i: (b, h, 0, 0))],
        out_specs=pl.BlockSpec((None, tq, hb * V_HEAD), lambda b, h, i: (b, i, h)),
        scratch_shapes=[pltpu.VMEM((hb, tq, LANES), _F32), pltpu.VMEM((hb, tq, dv), _F32)],
        compiler_params=_params("parallel", "parallel", "arbitrary"),
        name="mla_attention",
    )(q, k, v)
    return o.reshape(T, H * V_HEAD)


def _router_kernel(x_ref, g_ref, w_ref, r_ref):
    hn = _rms(x_ref[...], g_ref[...])
    logits = jnp.dot(hn, w_ref[...], precision=lax.Precision.HIGHEST, preferred_element_type=_F32)
    lane = lax.broadcasted_iota(jnp.int32, logits.shape, 1)
    lane_f = lane.astype(_F32)
    neg = -jnp.inf
    lg = jnp.where(lane < N_EXPERTS, logits, neg)
    v1 = jnp.max(lg, axis=1, keepdims=True)
    i1 = jnp.min(jnp.where(lg == v1, lane_f, float(LANES)), axis=1, keepdims=True)
    lg2 = jnp.where(lane_f == i1, neg, lg)
    v2 = jnp.max(lg2, axis=1, keepdims=True)
    i2 = jnp.min(jnp.where(lg2 == v2, lane_f, float(LANES)), axis=1, keepdims=True)
    e2 = jnp.exp(v2 - v1)
    w1 = 1.0 / (1.0 + e2)
    w2 = e2 / (1.0 + e2)
    out = jnp.where(lane < N_EXPERTS, logits, 0.0)
    out = jnp.where(lane == N_EXPERTS, i1, out)
    out = jnp.where(lane == N_EXPERTS + 1, i2, out)
    out = jnp.where(lane == N_EXPERTS + 2, w1, out)
    out = jnp.where(lane == N_EXPERTS + 3, w2, out)
    r_ref[...] = out


def router(x, g, w_router_padded):
    T, d = x.shape
    return pl.pallas_call(
        _router_kernel,
        out_shape=jax.ShapeDtypeStruct((T, LANES), _F32),
        grid=(T // ROW_TILE,),
        in_specs=[pl.BlockSpec((ROW_TILE, d), lambda i: (i, 0)),
                  pl.BlockSpec((1, d), lambda i: (0, 0)),
                  pl.BlockSpec((d, LANES), lambda i: (0, 0))],
        out_specs=pl.BlockSpec((ROW_TILE, LANES), lambda i: (i, 0)),
        compiler_params=_params("parallel"),
        name="moe_router",
    )(x, g.reshape(1, d), w_router_padded)


def _row_copy(src_hbm, dst_vmem, sem, src_row, dst_row):
    return pltpu.make_async_copy(src_hbm.at[pl.ds(src_row, 1)], dst_vmem.at[pl.ds(dst_row, 1)], sem)


def _index_spec(n):
    return pl.BlockSpec((None, 1, n), lambda i: (i, 0, 0), memory_space=pltpu.SMEM)


def _gather_norm_kernel(idx_ref, x_hbm, g_ref, o_ref, buf, sem):
    tr = buf.shape[0]

    def start(r, carry):
        _row_copy(x_hbm, buf, sem, idx_ref[0, r], r).start()
        return carry

    def wait(r, carry):
        _row_copy(x_hbm, buf, sem, 0, r).wait()
        return carry

    lax.fori_loop(0, tr, start, 0)
    lax.fori_loop(0, tr, wait, 0)
    o_ref[...] = _rms(buf[...], g_ref[...]).astype(o_ref.dtype)


def gather_norm(x, g, src_rows, tr):
    T, d = x.shape
    r = src_rows.shape[0]
    return pl.pallas_call(
        _gather_norm_kernel,
        out_shape=jax.ShapeDtypeStruct((r, d), _BF16),
        grid=(r // tr,),
        in_specs=[_index_spec(tr), pl.BlockSpec(memory_space=pl.ANY), pl.BlockSpec((1, d), lambda i: (0, 0))],
        out_specs=pl.BlockSpec((tr, d), lambda i: (i, 0)),
        scratch_shapes=[pltpu.VMEM((tr, d), _F32), pltpu.SemaphoreType.DMA],
        compiler_params=_params("arbitrary"),
        name="moe_gather_norm",
    )(src_rows.reshape(r // tr, 1, tr), x, g.reshape(1, d))


def _combine_kernel(pos_ref, x_ref, w_ref, y_hbm, o_ref, buf_a, buf_b, sem):
    tt = buf_a.shape[0]

    def start(r, carry):
        _row_copy(y_hbm, buf_a, sem, pos_ref[0, TOP_K * r], r).start()
        _row_copy(y_hbm, buf_b, sem, pos_ref[0, TOP_K * r + 1], r).start()
        return carry

    def wait(r, carry):
        _row_copy(y_hbm, buf_a, sem, 0, r).wait()
        _row_copy(y_hbm, buf_b, sem, 0, r).wait()
        return carry

    lax.fori_loop(0, tt, start, 0)
    lax.fori_loop(0, tt, wait, 0)
    w = w_ref[...]
    o_ref[...] = x_ref[...] + w[:, 0:1] * buf_a[...] + w[:, 1:2] * buf_b[...]


def moe_combine(x, y, pos_flat, weights, tt):
    T, d = x.shape
    return pl.pallas_call(
        _combine_kernel,
        out_shape=jax.ShapeDtypeStruct((T, d), _F32),
        grid=(T // tt,),
        in_specs=[_index_spec(tt * TOP_K), pl.BlockSpec((tt, d), lambda i: (i, 0)),
                  pl.BlockSpec((tt, TOP_K), lambda i: (i, 0)), pl.BlockSpec(memory_space=pl.ANY)],
        out_specs=pl.BlockSpec((tt, d), lambda i: (i, 0)),
        scratch_shapes=[pltpu.VMEM((tt, d), _F32), pltpu.VMEM((tt, d), _F32), pltpu.SemaphoreType.DMA],
        compiler_params=_params("arbitrary"),
        name="moe_combine",
    )(pos_flat.reshape(T // tt, 1, tt * TOP_K), x, weights, y)


def _moe_plan(route, tm, n_tiles):
    T = route.shape[0]
    e_flat = route[:, N_EXPERTS:N_EXPERTS + TOP_K].astype(jnp.int32).reshape(-1)
    onehot = (e_flat[:, None] == jnp.arange(N_EXPERTS, dtype=jnp.int32)[None, :]).astype(jnp.int32)
    csum = jnp.cumsum(onehot, axis=0)
    rank = jnp.sum(csum * onehot, axis=1) - 1
    tiles_e = (csum[-1] + tm - 1) // tm
    tile_end = jnp.cumsum(tiles_e)
    tile_start = tile_end - tiles_e
    pos = jnp.sum(onehot * tile_start[None, :], axis=1) * tm + rank
    tile_ids = jnp.arange(n_tiles, dtype=jnp.int32)
    tile_expert = jnp.minimum(jnp.sum((tile_ids[:, None] >= tile_end[None, :]).astype(jnp.int32), axis=1),
                              N_EXPERTS - 1).astype(jnp.int32)
    n_used = tile_end[-1:].astype(jnp.int32)
    rows = n_tiles * tm
    src = jnp.zeros((rows,), jnp.int32).at[pos].set(jnp.arange(T * TOP_K, dtype=jnp.int32) // TOP_K)
    return tile_expert, n_used, src, pos.astype(jnp.int32)


def moe_layer(x, g, w_router, we_gate, we_up, we_down):
    T, d = x.shape
    tm = MOE_ROW_TILE
    n_tiles = (T * TOP_K) // tm + N_EXPERTS
    w_r = jnp.zeros((d, LANES), _F32).at[:, :N_EXPERTS].set(w_router)
    route = router(x, g, w_r)
    tile_expert, n_used, src, pos = _moe_plan(route, tm, n_tiles)
    xs = gather_norm(x, g, src, tm)
    ff = we_gate.shape[-1]
    h = grouped_swiglu_up(xs, we_gate.astype(_BF16), we_up.astype(_BF16), tile_expert, n_used, tm=tm, tn=ff // 4)
    y = grouped_matmul(h, we_down.astype(_BF16), tile_expert, n_used, tm=tm, tn=d // 2, out_dtype=_F32)
    weights = route[:, N_EXPERTS + TOP_K:N_EXPERTS + 2 * TOP_K]
    return moe_combine(x, y, pos, weights, tm)


def _rope_tables(positions, half):
    inv = ROPE_BASE ** (-jnp.arange(half, dtype=_F32) / half)
    ang = positions.astype(_F32)[..., None] * inv
    return jnp.cos(ang), jnp.sin(ang)


def _rotate_half_cols(w, width):
    lead = w.shape[:-1]
    w2 = w.reshape(lead + (-1, 2, width // 2))
    return jnp.stack([-w2[..., 1, :], w2[..., 0, :]], axis=-2).reshape(w.shape)


def _swap_halves(g):
    half = g.shape[-1] // 2
    return jnp.concatenate([g[..., half:], g[..., :half]], axis=-1)


def kernel(x, positions, g_mix_norm, g_ffn_norm, w_in, conv_w, b_gates, g_mlstm_out, g_ret_out, w_mix_out,
           ffn_gate, ffn_up, ffn_down, w_dqkv, g_cq, g_ckv, w_uq, w_ukv, g_qn, g_qr, g_kn, g_kr, w_o,
           w_router, we_gate, we_up, we_down):
    B, S, D = x.shape
    T = B * S
    xf = x.reshape(T, D)
    n_gate = 2 * MLSTM_HEADS
    main = 4 * MLSTM_HEADS * HEAD_DIM

    w = w_in[0]
    w_main = jnp.concatenate([w[:, :main], w[:, main + n_gate:]], axis=1).astype(_BF16)
    w_gate = jnp.zeros((D, LANES), _BF16).at[:, :n_gate].set(w[:, main:main + n_gate].astype(_BF16))
    hn = rmsnorm_bf16(xf, g_mix_norm[0])
    proj = dense_matmul(hn, w_main, tn=1024, out_dtype=_F32)
    gates = dense_matmul(hn, w_gate, tn=LANES, out_dtype=_F32)[:, :n_gate] + b_gates[0][None, :]
    cos_r, sin_r = _rope_tables(positions, HEAD_DIM // 2)
    log_gamma = jnp.log1p(-jnp.exp2(-5.0 - jnp.arange(RET_HEADS, dtype=_F32)))
    mix = recurrent_mixer(proj.reshape(B, S, -1), gates.reshape(B, S, n_gate), cos_r, sin_r, conv_w[0],
                          g_mlstm_out[0], g_ret_out[0], log_gamma).reshape(T, -1)
    w_mix = w_mix_out[0].reshape(2, MLSTM_HEADS, HEAD_DIM, D).transpose(1, 0, 2, 3).reshape(-1, D).astype(_BF16)
    xf = dense_matmul(mix, w_mix, tn=1024, out_dtype=_F32, residual=xf)

    hn = rmsnorm_bf16(xf, g_ffn_norm[0])
    te, nu = _dense_plan(T, ROW_TILE)
    ff = ffn_gate.shape[-1]
    hmid = grouped_swiglu_up(hn, ffn_gate.astype(_BF16), ffn_up.astype(_BF16), te, nu, tm=ROW_TILE, tn=ff // 4)
    xf = dense_matmul(hmid, ffn_down[0].astype(_BF16), tn=512, out_dtype=_F32, residual=xf)

    H = MLA_HEADS
    wd = w_dqkv[0]
    w_kr = wd[:, Q_LORA + KV_LORA:]
    wd_full = jnp.concatenate([wd, _rotate_half_cols(w_kr, QK_ROPE)], axis=1).astype(_BF16)
    hn = rmsnorm_bf16(xf, g_mix_norm[1])
    c = dense_matmul(hn, wd_full, tn=wd_full.shape[1] // 3, out_dtype=_F32)
    wq = w_uq[0].reshape(Q_LORA, H, QK_NOPE + QK_ROPE)
    wq_r = wq[..., QK_NOPE:]
    w_q = jnp.concatenate([wq, _rotate_half_cols(wq_r, QK_ROPE)], axis=-1).reshape(Q_LORA, -1).astype(_BF16)
    cos_m, sin_m = _rope_tables(positions, QK_ROPE // 2)
    cs = jnp.concatenate([cos_m, cos_m, sin_m, sin_m], axis=-1).reshape(T, LANES)
    gv_q = jnp.concatenate([g_qr[0], _swap_halves(g_qr[0])]).reshape(1, LANES)
    gv_k = jnp.concatenate([g_kr[0], _swap_halves(g_kr[0])]).reshape(1, LANES)
    attn = mla_attention(c, cs, w_q, w_ukv[0].astype(_BF16), g_cq[0], g_ckv[0], g_qn[0], gv_q, g_kn[0], gv_k,
                         B, S)
    xf = dense_matmul(attn, w_o[0].astype(_BF16), tn=1024, out_dtype=_F32, residual=xf)

    xf = moe_layer(xf, g_ffn_norm[1], w_router[0], we_gate[0], we_up[0], we_down[0])
    return xf.reshape(B, S, D)
```

```python
import functools

import jax
import jax.numpy as jnp
from jax import lax
from jax.experimental import pallas as pl
from jax.experimental.pallas import tpu as pltpu

EPS = 1e-6
ROPE_BASE = 10000.0
CONV_WIDTH = 4
MLSTM_HEADS = 4
RET_HEADS = 4
HEAD_DIM = 256
REC_CHUNK = 256
MLA_HEADS = 16
Q_LORA = 512
KV_LORA = 512
QK_NOPE = 128
QK_ROPE = 64
V_HEAD = 128
N_EXPERTS = 8
TOP_K = 2

LANES = 128
SUBLANES = 8
VMEM_LIMIT_BYTES = 48 * 1024 * 1024

ROW_TILE = 512
MOE_ROW_TILE = 256
WEIGHT_COLS = 512
ATTN_TILE = 512
MLA_HEAD_GROUP = 4
DMA_QUEUES = 2
LOG2_E = 1.4426950408889634

_F32 = jnp.float32
_BF16 = jnp.bfloat16
_NT = (((1,), (1,)), ((), ()))
_TN = (((0,), (0,)), ((), ()))


def _params(*semantics):
    return pltpu.CompilerParams(dimension_semantics=semantics, vmem_limit_bytes=VMEM_LIMIT_BYTES)


def _sigmoid(x):
    return 1.0 / (1.0 + jnp.exp(-x))


def _rms(x, g):
    return x * lax.rsqrt(jnp.mean(x * x, axis=-1, keepdims=True) + EPS) * g


def _rmsnorm_kernel(x_ref, g_ref, o_ref):
    o_ref[...] = _rms(x_ref[...], g_ref[...]).astype(o_ref.dtype)


def rmsnorm_bf16(x, g):
    n, d = x.shape
    return pl.pallas_call(
        _rmsnorm_kernel,
        out_shape=jax.ShapeDtypeStruct((n, d), _BF16),
        grid=(n // ROW_TILE,),
        in_specs=[pl.BlockSpec((ROW_TILE, d), lambda i: (i, 0)),
                  pl.BlockSpec((1, d), lambda i: (0, 0))],
        out_specs=pl.BlockSpec((ROW_TILE, d), lambda i: (i, 0)),
        compiler_params=_params("parallel"),
        name="rmsnorm",
    )(x, g.reshape(1, d))


def _gmm_kernel(te_ref, nu_ref, a_ref, w_ref, *rest):
    del te_ref
    o_ref = rest[-1]

    @pl.when(pl.program_id(1) < nu_ref[0])
    def _():
        acc = jnp.dot(a_ref[...], w_ref[...].astype(_BF16), preferred_element_type=_F32)
        if len(rest) == 2:
            acc = acc + rest[0][...]
        o_ref[...] = acc.astype(o_ref.dtype)

    @pl.when(pl.program_id(1) >= nu_ref[0])
    def _():
        o_ref[...] = jnp.zeros_like(o_ref)


def grouped_matmul(a, w, tile_expert, n_used, *, tm, tn, out_dtype, residual=None, n_out=None):
    m, k = a.shape
    n = w.shape[2] if n_out is None else n_out
    n_tiles = m // tm

    def row(j, i, te, nu):
        return jnp.minimum(i, nu[0] - 1)

    in_specs = [pl.BlockSpec((tm, k), lambda j, i, te, nu: (row(j, i, te, nu), 0)),
                pl.BlockSpec((None, k, tn), lambda j, i, te, nu: (te[row(j, i, te, nu)], 0, j))]
    args = [a, w]
    if residual is not None:
        in_specs.append(pl.BlockSpec((tm, tn), lambda j, i, te, nu: (row(j, i, te, nu), j)))
        args.append(residual)
    return pl.pallas_call(
        _gmm_kernel,
        out_shape=jax.ShapeDtypeStruct((m, n), out_dtype),
        grid_spec=pltpu.PrefetchScalarGridSpec(
            num_scalar_prefetch=2,
            grid=(n // tn, n_tiles),
            in_specs=in_specs,
            out_specs=pl.BlockSpec((tm, tn), lambda j, i, te, nu: (i, j)),
        ),
        compiler_params=_params("arbitrary", "arbitrary"),
        name="grouped_matmul",
    )(tile_expert, n_used, *args)


def _gswiglu_kernel(te_ref, nu_ref, a_ref, wg_ref, wu_ref, o_ref):
    del te_ref

    @pl.when(pl.program_id(1) < nu_ref[0])
    def _():
        a = a_ref[...]
        g = jnp.dot(a, wg_ref[...].astype(_BF16), preferred_element_type=_F32)
        u = jnp.dot(a, wu_ref[...].astype(_BF16), preferred_element_type=_F32)
        o_ref[...] = (g * _sigmoid(g) * u).astype(o_ref.dtype)

    @pl.when(pl.program_id(1) >= nu_ref[0])
    def _():
        o_ref[...] = jnp.zeros_like(o_ref)


def grouped_swiglu_up(a, wg, wu, tile_expert, n_used, *, tm, tn):
    m, k = a.shape
    _, _, n = wg.shape

    def row(j, i, te, nu):
        return jnp.minimum(i, nu[0] - 1)

    wspec = pl.BlockSpec((None, k, tn), lambda j, i, te, nu: (te[row(j, i, te, nu)], 0, j))
    return pl.pallas_call(
        _gswiglu_kernel,
        out_shape=jax.ShapeDtypeStruct((m, n), _BF16),
        grid_spec=pltpu.PrefetchScalarGridSpec(
            num_scalar_prefetch=2,
            grid=(n // tn, m // tm),
            in_specs=[pl.BlockSpec((tm, k), lambda j, i, te, nu: (row(j, i, te, nu), 0)), wspec, wspec],
            out_specs=pl.BlockSpec((tm, tn), lambda j, i, te, nu: (i, j)),
        ),
        compiler_params=_params("arbitrary", "arbitrary"),
        name="grouped_swiglu_up",
    )(tile_expert, n_used, a, wg, wu)


def _dense_plan(m, tm):
    n_tiles = m // tm
    return jnp.zeros((n_tiles,), jnp.int32), jnp.full((1,), n_tiles, jnp.int32)


def dense_matmul(a, w, *, tn, out_dtype, residual=None, n_out=None):
    te, nu = _dense_plan(a.shape[0], ROW_TILE)
    return grouped_matmul(a, w[None], te, nu, tm=ROW_TILE, tn=tn, out_dtype=out_dtype, residual=residual,
                          n_out=n_out)


def _recurrent_kernel(q_ref, k_ref, v_ref, o_ref, rq_ref, rk_ref, rv_ref, rg_ref,
                      gt_ref, gc_ref, cos_ref, sin_ref, cwq_ref, cwk_ref, gm_ref, gr_ref, lg_ref,
                      mix_ref,
                      c_s, n_s, m_s, r_s, qbuf, kbuf):
    L, dh = q_ref.shape
    halo = SUBLANES
    inv_sqrt_d = dh ** -0.5

    @pl.when(pl.program_id(2) == 0)
    def _():
        c_s[...] = jnp.zeros_like(c_s)
        n_s[...] = jnp.zeros_like(n_s)
        m_s[...] = jnp.full_like(m_s, -jnp.inf)
        r_s[...] = jnp.zeros_like(r_s)
        qbuf[0:halo, :] = jnp.zeros((halo, dh), _F32)
        kbuf[0:halo, :] = jnp.zeros((halo, dh), _F32)

    def conv_silu(x_ref, buf, w_ref):
        buf[halo:halo + L, :] = x_ref[...]
        w = w_ref[...]
        y = w[CONV_WIDTH - 1:CONV_WIDTH, :] * buf[halo:halo + L, :]
        for j in range(CONV_WIDTH - 1):
            off = halo - (CONV_WIDTH - 1) + j
            y = y + w[j:j + 1, :] * buf[off:off + L, :]
        buf[0:halo, :] = buf[L:L + halo, :]
        return y * _sigmoid(y)

    row = lax.broadcasted_iota(jnp.int32, (L, L), 0)
    col = lax.broadcasted_iota(jnp.int32, (L, L), 1)
    causal = col <= row

    q = conv_silu(q_ref, qbuf, cwq_ref)
    k = conv_silu(k_ref, kbuf, cwk_ref)
    qb = q.astype(_BF16)
    vb = v_ref[...].astype(_BF16)

    def log_sigmoid(x):
        return jnp.minimum(x, 0.0) - jnp.log1p(jnp.exp(-jnp.abs(x)))

    gt = gt_ref[...]
    gc = gc_ref[...]
    i_row = gt[0:1, :]
    f_row = log_sigmoid(gt[1:2, :])
    i_col = gc[:, 0:1]
    f_col = log_sigmoid(gc[:, 1:2])
    hi = lax.Precision.HIGHEST
    b_col = jnp.dot(causal.astype(_F32), f_col, precision=hi, preferred_element_type=_F32)
    b_row = jnp.dot(f_row, (row <= col).astype(_F32), precision=hi, preferred_element_type=_F32)
    g_tot = b_col[L - 1:L, :]
    m_prev = m_s[...]

    log_d = jnp.where(causal, b_col - b_row + i_row, -jnp.inf)
    m_inter = b_col + m_prev
    m_t = jnp.maximum(jnp.max(log_d, axis=1, keepdims=True), m_inter)
    d_m = jnp.exp(log_d - m_t)
    inter = jnp.exp(m_inter - m_t)
    s = lax.dot_general(qb, k.astype(_BF16), _NT, preferred_element_type=_F32) * inv_sqrt_d
    s_m = s * d_m
    num = (jnp.dot(s_m.astype(_BF16), vb, preferred_element_type=_F32)
           + inter * jnp.dot(qb, c_s[...].astype(_BF16), preferred_element_type=_F32))
    den = jnp.sum(s_m, axis=1, keepdims=True) + inter * jnp.sum(q * n_s[...], axis=1, keepdims=True)
    h = num / jnp.maximum(jnp.abs(den), jnp.exp(-m_t))

    log_w = g_tot - b_col + i_col
    m_new = jnp.maximum(g_tot + m_prev, jnp.max(log_w, axis=0, keepdims=True))
    w_col = jnp.exp(log_w - m_new)
    decay = jnp.exp(g_tot + m_prev - m_new)
    kw = k * (w_col * inv_sqrt_d)
    c_s[...] = decay * c_s[...] + lax.dot_general(kw.astype(_BF16), vb, _TN, preferred_element_type=_F32)
    n_s[...] = decay * n_s[...] + jnp.sum(kw, axis=0, keepdims=True)
    m_s[...] = m_new

    mix_ref[:, :dh] = _rms(_sigmoid(o_ref[...]) * h, gm_ref[...]).astype(mix_ref.dtype)

    half = dh // 2
    cos = cos_ref[...]
    sin = sin_ref[...]

    def rope(x):
        x1 = x[:, :half]
        x2 = x[:, half:]
        return jnp.concatenate([x1 * cos - x2 * sin, x2 * cos + x1 * sin], axis=1)

    rq = rope(rq_ref[...]).astype(_BF16)
    rk = rope(rk_ref[...])
    rvb = rv_ref[...].astype(_BF16)
    lg = lg_ref[...]
    dist = (row - col).astype(_F32)
    d_r = jnp.where(causal, jnp.exp(lg * jnp.maximum(dist, 0.0)), 0.0)
    t_col = lax.broadcasted_iota(jnp.int32, (L, 1), 0).astype(_F32)
    q_decay = jnp.exp(lg * (t_col + 1.0))
    k_decay = jnp.exp(lg * (L - 1.0 - t_col))
    chunk_decay = jnp.exp(lg * L)
    s_r = lax.dot_general(rq, rk.astype(_BF16), _NT, preferred_element_type=_F32) * inv_sqrt_d * d_r
    out_r = (jnp.dot(s_r.astype(_BF16), rvb, preferred_element_type=_F32)
             + jnp.dot(rq, r_s[...].astype(_BF16), preferred_element_type=_F32) * q_decay)
    r_s[...] = chunk_decay * r_s[...] + lax.dot_general(
        (rk * (k_decay * inv_sqrt_d)).astype(_BF16), rvb, _TN, preferred_element_type=_F32)
    rg = rg_ref[...]
    mix_ref[:, dh:] = (rg * _sigmoid(rg) * _rms(out_r, gr_ref[...])).astype(mix_ref.dtype)


def recurrent_mixer(proj_m, proj_r, gates, cos, sin, conv_w, g_mlstm_out, g_ret_out, log_gamma):
    B, S, _ = proj_m.shape
    H, dh, L = MLSTM_HEADS, HEAD_DIM, REC_CHUNK
    g4 = gates.reshape(B, S, 2, H)
    gt = g4.transpose(0, 3, 2, 1)
    gc = g4.transpose(0, 3, 1, 2)

    def pspec(group):
        return pl.BlockSpec((None, L, dh), lambda b, h, c: (b, c, group * H + h))

    in_specs = [pspec(g) for g in range(4)] * 2 + [
        pl.BlockSpec((None, None, 2, L), lambda b, h, c: (b, h, 0, c)),
        pl.BlockSpec((None, None, L, 2), lambda b, h, c: (b, h, c, 0)),
        pl.BlockSpec((None, L, dh // 2), lambda b, h, c: (b, c, 0)),
        pl.BlockSpec((None, L, dh // 2), lambda b, h, c: (b, c, 0)),
        pl.BlockSpec((CONV_WIDTH, dh), lambda b, h, c: (0, h)),
        pl.BlockSpec((CONV_WIDTH, dh), lambda b, h, c: (0, H + h)),
        pl.BlockSpec((1, dh), lambda b, h, c: (0, h)),
        pl.BlockSpec((1, dh), lambda b, h, c: (0, h)),
        pl.BlockSpec((None, 1, 1), lambda b, h, c: (h, 0, 0)),
    ]
    return pl.pallas_call(
        _recurrent_kernel,
        out_shape=jax.ShapeDtypeStruct((B, S, 2 * H * dh), _BF16),
        grid=(B, H, S // L),
        in_specs=in_specs,
        out_specs=pl.BlockSpec((None, L, 2 * dh), lambda b, h, c: (b, c, h)),
        scratch_shapes=[pltpu.VMEM((dh, dh), _F32), pltpu.VMEM((1, dh), _F32), pltpu.VMEM((1, 1), _F32),
                        pltpu.VMEM((dh, dh), _F32),
                        pltpu.VMEM((L + SUBLANES, dh), _F32), pltpu.VMEM((L + SUBLANES, dh), _F32)],
        compiler_params=_params("parallel", "parallel", "arbitrary"),
        name="recurrent_mixer",
    )(*([proj_m] * 4), *([proj_r] * 4), gt, gc, cos, sin, conv_w, conv_w,
      g_mlstm_out.reshape(1, H * dh), g_ret_out.reshape(1, H * dh), log_gamma.reshape(H, 1, 1))


def _rope_pair(z2, cs, gvec):
    lane = lax.broadcasted_iota(jnp.int32, z2.shape, 1)
    first = lane < QK_ROPE
    ms = jnp.sum(jnp.where(first, z2 * z2, 0.0), axis=1, keepdims=True) * (1.0 / QK_ROPE)
    t = z2 * lax.rsqrt(ms + EPS) * (cs * gvec)
    return jnp.where(first, t + pltpu.roll(t, QK_ROPE, 1), 0.0)


def _mla_q_kernel(c_ref, gcq_ref, w_ref, gqn_ref, gqr_ref, cs_ref, q_ref, cn_s, *, scale):
    @pl.when(pl.program_id(1) == 0)
    def _():
        cn_s[...] = _rms(c_ref[...], gcq_ref[...]).astype(_BF16)

    dq = q_ref.shape[-1]
    z = jnp.dot(cn_s[...], w_ref[...].astype(_BF16), preferred_element_type=_F32)
    cs = cs_ref[...]
    for h in range(q_ref.shape[0]):
        zh = z[:, h * dq:(h + 1) * dq]
        qn = _rms(zh[:, :QK_NOPE], gqn_ref[...])
        qr = _rope_pair(zh[:, QK_NOPE:], cs, gqr_ref[...])
        q_ref[h] = (jnp.concatenate([qn, qr], axis=1) * scale).astype(q_ref.dtype)


def _mla_kv_kernel(c_ref, kr_ref, gckv_ref, w_ref, gkn_ref, gkr_ref, cs_ref, k_ref, v_ref, cn_s):
    @pl.when(pl.program_id(1) == 0)
    def _():
        cn_s[...] = _rms(c_ref[...], gckv_ref[...]).astype(_BF16)

    dk = k_ref.shape[-1]
    z = jnp.dot(cn_s[...], w_ref[...].astype(_BF16), preferred_element_type=_F32)
    kr = _rope_pair(kr_ref[...], cs_ref[...], gkr_ref[...])
    ones = jnp.ones((z.shape[0], LANES), _F32)
    for h in range(k_ref.shape[0]):
        zh = z[:, h * dk:(h + 1) * dk]
        kn = _rms(zh[:, :QK_NOPE], gkn_ref[...])
        k_ref[h] = jnp.concatenate([kn, kr], axis=1).astype(k_ref.dtype)
        v_ref[h] = jnp.concatenate([zh[:, QK_NOPE:], ones], axis=1).astype(v_ref.dtype)


def _mla_attn_kernel(q_ref, k_ref, v_ref, o_ref, m_s, acc_s):
    qi = pl.program_id(2)
    hb, tq, _ = q_ref.shape
    dv = v_ref.shape[-1]
    for h in range(hb):
        m_s[h] = jnp.full((tq, LANES), -jnp.inf, _F32)
        acc_s[h] = jnp.zeros((tq, dv), _F32)

    def block(h, start, diagonal):
        kb = k_ref[h, pl.ds(start, tq), :]
        vb = v_ref[h, pl.ds(start, tq), :]
        s = lax.dot_general(q_ref[h], kb, _NT, preferred_element_type=_F32)
        if diagonal:
            row = lax.broadcasted_iota(jnp.int32, s.shape, 0)
            col = lax.broadcasted_iota(jnp.int32, s.shape, 1)
            s = jnp.where(col <= row, s, -jnp.inf)
        m_prev = m_s[h]
        m_new = jnp.maximum(m_prev, jnp.max(s, axis=1, keepdims=True))
        p = jnp.exp2(s - jnp.concatenate([m_new] * (tq // LANES), axis=1))
        alpha = jnp.exp2(m_prev - m_new)
        acc_s[h] = jnp.concatenate([alpha] * (dv // LANES), axis=1) * acc_s[h] + jnp.dot(
            p.astype(_BF16), vb, preferred_element_type=_F32)
        m_s[h] = m_new

    def body(j, carry):
        for h in range(hb):
            block(h, pl.multiple_of(j * tq, tq), False)
        return carry

    lax.fori_loop(0, qi, body, 0)
    for h in range(hb):
        block(h, pl.multiple_of(qi * tq, tq), True)
    for h in range(hb):
        acc = acc_s[h]
        o_ref[:, h * V_HEAD:(h + 1) * V_HEAD] = (acc[:, :V_HEAD] / acc[:, V_HEAD:]).astype(o_ref.dtype)


def mla_attention(c, cs, w_q, w_kv, g_cq, g_ckv, g_qn, gv_q, g_kn, gv_k, B, S):
    T = c.shape[0]
    H, tm, hb = MLA_HEADS, ROW_TILE, MLA_HEAD_GROUP
    dq = QK_NOPE + 2 * QK_ROPE
    dv = 2 * V_HEAD
    scale = (QK_NOPE + QK_ROPE) ** -0.5 * LOG2_E
    nt = S // tm
    vec = lambda n: pl.BlockSpec((1, n), lambda i, h: (0, 0))
    head_out = lambda d: pl.BlockSpec((None, hb, tm, d), lambda i, h: (i // nt, h, i % nt, 0))
    q = pl.pallas_call(
        functools.partial(_mla_q_kernel, scale=scale),
        out_shape=jax.ShapeDtypeStruct((B, H, S, dq), _BF16),
        grid=(T // tm, H // hb),
        in_specs=[pl.BlockSpec((tm, Q_LORA), lambda i, h: (i, 0)), vec(Q_LORA),
                  pl.BlockSpec((Q_LORA, hb * dq), lambda i, h: (0, h)),
                  vec(QK_NOPE), vec(LANES),
                  pl.BlockSpec((tm, LANES), lambda i, h: (i, 0))],
        out_specs=head_out(dq),
        scratch_shapes=[pltpu.VMEM((tm, Q_LORA), _BF16)],
        compiler_params=_params("parallel", "arbitrary"),
        name="mla_q_proj",
    )(c, g_cq.reshape(1, -1), w_q, g_qn.reshape(1, -1), gv_q, cs)
    k, v = pl.pallas_call(
        _mla_kv_kernel,
        out_shape=[jax.ShapeDtypeStruct((B, H, S, dq), _BF16), jax.ShapeDtypeStruct((B, H, S, dv), _BF16)],
        grid=(T // tm, H // hb),
        in_specs=[pl.BlockSpec((tm, KV_LORA), lambda i, h: (i, 1)),
                  pl.BlockSpec((tm, LANES), lambda i, h: (i, (Q_LORA + KV_LORA) // LANES)),
                  vec(KV_LORA),
                  pl.BlockSpec((KV_LORA, hb * (QK_NOPE + V_HEAD)), lambda i, h: (0, h)),
                  vec(QK_NOPE), vec(LANES),
                  pl.BlockSpec((tm, LANES), lambda i, h: (i, 0))],
        out_specs=[head_out(dq), head_out(dv)],
        scratch_shapes=[pltpu.VMEM((tm, KV_LORA), _BF16)],
        compiler_params=_params("parallel", "arbitrary"),
        name="mla_kv_proj",
    )(c, c, g_ckv.reshape(1, -1), w_kv, g_kn.reshape(1, -1), gv_k, cs)
    tq = ATTN_TILE
    o = pl.pallas_call(
        _mla_attn_kernel,
        out_shape=jax.ShapeDtypeStruct((B, S, H * V_HEAD), _BF16),
        grid=(B, H // hb, S // tq),
        in_specs=[pl.BlockSpec((None, hb, tq, dq), lambda b, h, i: (b, h, i, 0)),
                  pl.BlockSpec((None, hb, S, dq), lambda b, h, i: (b, h, 0, 0)),
                  pl.BlockSpec((None, hb, S, dv), lambda b, h, i: (b, h, 0, 0))],
        out_specs=pl.BlockSpec((None, tq, hb * V_HEAD), lambda b, h, i: (b, i, h)),
        scratch_shapes=[pltpu.VMEM((hb, tq, LANES), _F32), pltpu.VMEM((hb, tq, dv), _F32)],
        compiler_params=_params("parallel", "parallel", "arbitrary"),
        name="mla_attention",
    )(q, k, v)
    return o.reshape(T, H * V_HEAD)


def _router_kernel(x_ref, g_ref, w_ref, r_ref):
    hn = _rms(x_ref[...], g_ref[...])
    logits = jnp.dot(hn, w_ref[...], precision=lax.Precision.HIGHEST, preferred_element_type=_F32)
    lane = lax.broadcasted_iota(jnp.int32, logits.shape, 1)
    lane_f = lane.astype(_F32)
    neg = -jnp.inf
    lg = jnp.where(lane < N_EXPERTS, logits, neg)
    v1 = jnp.max(lg, axis=1, keepdims=True)
    i1 = jnp.min(jnp.where(lg == v1, lane_f, float(LANES)), axis=1, keepdims=True)
    lg2 = jnp.where(lane_f == i1, neg, lg)
    v2 = jnp.max(lg2, axis=1, keepdims=True)
    i2 = jnp.min(jnp.where(lg2 == v2, lane_f, float(LANES)), axis=1, keepdims=True)
    e2 = jnp.exp(v2 - v1)
    w1 = 1.0 / (1.0 + e2)
    w2 = e2 / (1.0 + e2)
    out = jnp.where(lane < N_EXPERTS, logits, 0.0)
    out = jnp.where(lane == N_EXPERTS, i1, out)
    out = jnp.where(lane == N_EXPERTS + 1, i2, out)
    out = jnp.where(lane == N_EXPERTS + 2, w1, out)
    out = jnp.where(lane == N_EXPERTS + 3, w2, out)
    r_ref[...] = out


def router(x, g, w_router_padded):
    T, d = x.shape
    return pl.pallas_call(
        _router_kernel,
        out_shape=jax.ShapeDtypeStruct((T, LANES), _F32),
        grid=(T // ROW_TILE,),
        in_specs=[pl.BlockSpec((ROW_TILE, d), lambda i: (i, 0)),
                  pl.BlockSpec((1, d), lambda i: (0, 0)),
                  pl.BlockSpec((d, LANES), lambda i: (0, 0))],
        out_specs=pl.BlockSpec((ROW_TILE, LANES), lambda i: (i, 0)),
        compiler_params=_params("parallel"),
        name="moe_router",
    )(x, g.reshape(1, d), w_router_padded)


def _row_copy(src_hbm, dst_vmem, sem, src_row, dst_row):
    return pltpu.make_async_copy(src_hbm.at[pl.ds(src_row, 1)], dst_vmem.at[pl.ds(dst_row, 1)], sem)


def _index_spec(n):
    return pl.BlockSpec((None, 1, n), lambda i: (i, 0, 0), memory_space=pltpu.SMEM)


def _gather_norm_kernel(idx_ref, x_hbm, g_ref, o_ref, buf, sem):
    tr = buf.shape[0]

    def start(r2, carry):
        for p in range(DMA_QUEUES):
            r = DMA_QUEUES * r2 + p
            _row_copy(x_hbm, buf, sem, idx_ref[0, r], r).start(priority=p)
        return carry

    def wait(r, carry):
        _row_copy(x_hbm, buf, sem, 0, r).wait()
        return carry

    lax.fori_loop(0, tr // DMA_QUEUES, start, 0)
    lax.fori_loop(0, tr, wait, 0)
    o_ref[...] = _rms(buf[...], g_ref[...]).astype(o_ref.dtype)


def gather_norm(x, g, src_rows, tr):
    T, d = x.shape
    r = src_rows.shape[0]
    return pl.pallas_call(
        _gather_norm_kernel,
        out_shape=jax.ShapeDtypeStruct((r, d), _BF16),
        grid=(r // tr,),
        in_specs=[_index_spec(tr), pl.BlockSpec(memory_space=pl.ANY), pl.BlockSpec((1, d), lambda i: (0, 0))],
        out_specs=pl.BlockSpec((tr, d), lambda i: (i, 0)),
        scratch_shapes=[pltpu.VMEM((tr, d), _F32), pltpu.SemaphoreType.DMA],
        compiler_params=_params("arbitrary"),
        name="moe_gather_norm",
    )(src_rows.reshape(r // tr, 1, tr), x, g.reshape(1, d))


def _combine_kernel(pos_ref, x_ref, w_ref, y_hbm, o_ref, buf_a, buf_b, sem):
    tt = buf_a.shape[0]

    def start(r, carry):
        _row_copy(y_hbm, buf_a, sem, pos_ref[0, TOP_K * r], r).start(priority=0)
        _row_copy(y_hbm, buf_b, sem, pos_ref[0, TOP_K * r + 1], r).start(priority=1)
        return carry

    def wait(r, carry):
        _row_copy(y_hbm, buf_a, sem, 0, r).wait()
        _row_copy(y_hbm, buf_b, sem, 0, r).wait()
        return carry

    lax.fori_loop(0, tt, start, 0)
    lax.fori_loop(0, tt, wait, 0)
    w = w_ref[...]
    o_ref[...] = x_ref[...] + w[:, 0:1] * buf_a[...] + w[:, 1:2] * buf_b[...]


def moe_combine(x, y, pos_flat, weights, tt):
    T, d = x.shape
    return pl.pallas_call(
        _combine_kernel,
        out_shape=jax.ShapeDtypeStruct((T, d), _F32),
        grid=(T // tt,),
        in_specs=[_index_spec(tt * TOP_K), pl.BlockSpec((tt, d), lambda i: (i, 0)),
                  pl.BlockSpec((tt, TOP_K), lambda i: (i, 0)), pl.BlockSpec(memory_space=pl.ANY)],
        out_specs=pl.BlockSpec((tt, d), lambda i: (i, 0)),
        scratch_shapes=[pltpu.VMEM((tt, d), _F32), pltpu.VMEM((tt, d), _F32), pltpu.SemaphoreType.DMA],
        compiler_params=_params("arbitrary"),
        name="moe_combine",
    )(pos_flat.reshape(T // tt, 1, tt * TOP_K), x, weights, y)


def _moe_plan(route, tm, n_tiles):
    T = route.shape[0]
    e_flat = route[:, N_EXPERTS:N_EXPERTS + TOP_K].astype(jnp.int32).reshape(-1)
    onehot = (e_flat[:, None] == jnp.arange(N_EXPERTS, dtype=jnp.int32)[None, :]).astype(jnp.int32)
    csum = jnp.cumsum(onehot, axis=0)
    rank = jnp.sum(csum * onehot, axis=1) - 1
    tiles_e = (csum[-1] + tm - 1) // tm
    tile_end = jnp.cumsum(tiles_e)
    tile_start = tile_end - tiles_e
    pos = jnp.sum(onehot * tile_start[None, :], axis=1) * tm + rank
    tile_ids = jnp.arange(n_tiles, dtype=jnp.int32)
    tile_expert = jnp.minimum(jnp.sum((tile_ids[:, None] >= tile_end[None, :]).astype(jnp.int32), axis=1),
                              N_EXPERTS - 1).astype(jnp.int32)
    n_used = tile_end[-1:].astype(jnp.int32)
    rows = n_tiles * tm
    src = jnp.zeros((rows,), jnp.int32).at[pos].set(jnp.arange(T * TOP_K, dtype=jnp.int32) // TOP_K)
    return tile_expert, n_used, src, pos.astype(jnp.int32)


def moe_layer(x, g, w_router, we_gate, we_up, we_down):
    T, d = x.shape
    tm = MOE_ROW_TILE
    n_tiles = (T * TOP_K) // tm + N_EXPERTS
    w_r = jnp.zeros((d, LANES), _F32).at[:, :N_EXPERTS].set(w_router)
    route = router(x, g, w_r)
    tile_expert, n_used, src, pos = _moe_plan(route, tm, n_tiles)
    xs = gather_norm(x, g, src, tm)
    h = grouped_swiglu_up(xs, we_gate, we_up, tile_expert, n_used, tm=tm, tn=WEIGHT_COLS)
    y = grouped_matmul(h, we_down, tile_expert, n_used, tm=tm, tn=WEIGHT_COLS, out_dtype=_F32)
    weights = route[:, N_EXPERTS + TOP_K:N_EXPERTS + 2 * TOP_K]
    return moe_combine(x, y, pos, weights, tm)


def _rope_tables(positions, half):
    inv = ROPE_BASE ** (-jnp.arange(half, dtype=_F32) / half)
    ang = positions.astype(_F32)[..., None] * inv
    return jnp.cos(ang), jnp.sin(ang)


def _rotate_half_cols(w, width):
    lead = w.shape[:-1]
    w2 = w.reshape(lead + (-1, 2, width // 2))
    return jnp.stack([-w2[..., 1, :], w2[..., 0, :]], axis=-2).reshape(w.shape)


def _swap_halves(g):
    half = g.shape[-1] // 2
    return jnp.concatenate([g[..., half:], g[..., :half]], axis=-1)


def kernel(x, positions, g_mix_norm, g_ffn_norm, w_in, conv_w, b_gates, g_mlstm_out, g_ret_out, w_mix_out,
           ffn_gate, ffn_up, ffn_down, w_dqkv, g_cq, g_ckv, w_uq, w_ukv, g_qn, g_qr, g_kn, g_kr, w_o,
           w_router, we_gate, we_up, we_down):
    B, S, D = x.shape
    T = B * S
    xf = x.reshape(T, D)
    n_gate = 2 * MLSTM_HEADS
    main = 4 * MLSTM_HEADS * HEAD_DIM

    w = w_in[0]
    w_gate = jnp.zeros((D, LANES), _F32).at[:, :n_gate].set(w[:, main:main + n_gate])
    hn = rmsnorm_bf16(xf, g_mix_norm[0])
    proj_m = dense_matmul(hn, w, tn=1024, out_dtype=_F32, n_out=main)
    proj_r = dense_matmul(hn, w[:, main + n_gate:], tn=1024, out_dtype=_F32)
    gates = dense_matmul(hn, w_gate, tn=LANES, out_dtype=_F32)[:, :n_gate] + b_gates[0][None, :]
    cos_r, sin_r = _rope_tables(positions, HEAD_DIM // 2)
    log_gamma = jnp.log1p(-jnp.exp2(-5.0 - jnp.arange(RET_HEADS, dtype=_F32)))
    mix = recurrent_mixer(proj_m.reshape(B, S, -1), proj_r.reshape(B, S, -1), gates.reshape(B, S, n_gate),
                          cos_r, sin_r, conv_w[0], g_mlstm_out[0], g_ret_out[0], log_gamma).reshape(T, -1)
    w_mix = w_mix_out[0].reshape(2, MLSTM_HEADS, HEAD_DIM, D).transpose(1, 0, 2, 3).reshape(-1, D)
    xf = dense_matmul(mix, w_mix, tn=1024, out_dtype=_F32, residual=xf)

    hn = rmsnorm_bf16(xf, g_ffn_norm[0])
    te, nu = _dense_plan(T, ROW_TILE)
    hmid = grouped_swiglu_up(hn, ffn_gate, ffn_up, te, nu, tm=ROW_TILE, tn=WEIGHT_COLS)
    xf = dense_matmul(hmid, ffn_down[0], tn=WEIGHT_COLS, out_dtype=_F32, residual=xf)

    H = MLA_HEADS
    wd = w_dqkv[0]
    w_kr = wd[:, Q_LORA + KV_LORA:]
    wd_full = jnp.concatenate([wd, _rotate_half_cols(w_kr, QK_ROPE)], axis=1)
    hn = rmsnorm_bf16(xf, g_mix_norm[1])
    c = dense_matmul(hn, wd_full, tn=wd_full.shape[1] // 3, out_dtype=_F32)
    wq = w_uq[0].reshape(Q_LORA, H, QK_NOPE + QK_ROPE)
    wq_r = wq[..., QK_NOPE:]
    w_q = jnp.concatenate([wq, _rotate_half_cols(wq_r, QK_ROPE)], axis=-1).reshape(Q_LORA, -1)
    cos_m, sin_m = _rope_tables(positions, QK_ROPE // 2)
    cs = jnp.concatenate([cos_m, cos_m, sin_m, sin_m], axis=-1).reshape(T, LANES)
    gv_q = jnp.concatenate([g_qr[0], _swap_halves(g_qr[0])]).reshape(1, LANES)
    gv_k = jnp.concatenate([g_kr[0], _swap_halves(g_kr[0])]).reshape(1, LANES)
    attn = mla_attention(c, cs, w_q, w_ukv[0], g_cq[0], g_ckv[0], g_qn[0], gv_q, g_kn[0], gv_k, B, S)
    xf = dense_matmul(attn, w_o[0], tn=1024, out_dtype=_F32, residual=xf)

    xf = moe_layer(xf, g_ffn_norm[1], w_router[0], we_gate[0], we_up[0], we_down[0])
    return xf.reshape(B, S, D)
```

```python
import functools

import jax
import jax.numpy as jnp
from jax import lax
from jax.experimental import pallas as pl
from jax.experimental.pallas import tpu as pltpu

EPS = 1e-6
ROPE_BASE = 10000.0
CONV_WIDTH = 4
MLSTM_HEADS = 4
RET_HEADS = 4
HEAD_DIM = 256
REC_CHUNK = 256
MLA_HEADS = 16
Q_LORA = 512
KV_LORA = 512
QK_NOPE = 128
QK_ROPE = 64
V_HEAD = 128
N_EXPERTS = 8
TOP_K = 2

LANES = 128
SUBLANES = 8
VMEM_LIMIT_BYTES = 48 * 1024 * 1024

ROW_TILE = 512
MOE_ROW_TILE = 256
UP_COLS = 1408
UP_CHUNKS = 8
DOWN_COLS = 1024
DENSE_DOWN_COLS = 512
DOWN_CHUNKS = 11
ATTN_TILE = 512
MLA_HEAD_GROUP = 4
LOG2_E = 1.4426950408889634

_F32 = jnp.float32
_BF16 = jnp.bfloat16
_NT = (((1,), (1,)), ((), ()))
_TN = (((0,), (0,)), ((), ()))


def _params(*semantics):
    return pltpu.CompilerParams(dimension_semantics=semantics, vmem_limit_bytes=VMEM_LIMIT_BYTES)


def _sigmoid(x):
    return 1.0 / (1.0 + jnp.exp(-x))


def _rms(x, g):
    return x * lax.rsqrt(jnp.mean(x * x, axis=-1, keepdims=True) + EPS) * g


def _gmm_kernel(te_ref, nu_ref, a_ref, w_ref, *rest, has_gain, has_res):
    del te_ref
    o_ref = rest[-1]

    @pl.when(pl.program_id(1) < nu_ref[0])
    def _():
        a = a_ref[...]
        if has_gain:
            a = _rms(a, rest[0][...]).astype(_BF16)
        acc = jnp.dot(a, w_ref[...].astype(_BF16), preferred_element_type=_F32)
        if has_res:
            acc = acc + rest[int(has_gain)][...]
        o_ref[...] = acc.astype(o_ref.dtype)

    @pl.when(pl.program_id(1) >= nu_ref[0])
    def _():
        o_ref[...] = jnp.zeros_like(o_ref)


def grouped_matmul(a, w, tile_expert, n_used, *, tm, tn, out_dtype, gain=None, residual=None, n_out=None):
    m, k = a.shape
    n = w.shape[2] if n_out is None else n_out
    n_tiles = m // tm

    def row(j, i, te, nu):
        return jnp.minimum(i, nu[0] - 1)

    in_specs = [pl.BlockSpec((tm, k), lambda j, i, te, nu: (row(j, i, te, nu), 0)),
                pl.BlockSpec((None, k, tn), lambda j, i, te, nu: (te[row(j, i, te, nu)], 0, j))]
    args = [a, w]
    if gain is not None:
        in_specs.append(pl.BlockSpec((1, k), lambda j, i, te, nu: (0, 0)))
        args.append(gain.reshape(1, k))
    if residual is not None:
        in_specs.append(pl.BlockSpec((tm, tn), lambda j, i, te, nu: (row(j, i, te, nu), j)))
        args.append(residual)
    return pl.pallas_call(
        functools.partial(_gmm_kernel, has_gain=gain is not None, has_res=residual is not None),
        out_shape=jax.ShapeDtypeStruct((m, n), out_dtype),
        grid_spec=pltpu.PrefetchScalarGridSpec(
            num_scalar_prefetch=2,
            grid=(n // tn, n_tiles),
            in_specs=in_specs,
            out_specs=pl.BlockSpec((tm, tn), lambda j, i, te, nu: (i, j)),
        ),
        compiler_params=_params("arbitrary", "arbitrary"),
        name="grouped_matmul",
    )(tile_expert, n_used, *args)


_P_EXPERT, _P_FIRST, _P_LO, _P_HI, _P_NEXT, _P_LAST, _P_GROUP, _P_NGROUPS = range(8)


def _streamed_kernel(plan_ref, nu_ref, a_ref, *refs, n_mats, has_gain, has_res, n_col_tiles):
    w_hbm = refs[:n_mats]
    gain_ref = refs[n_mats] if has_gain else None
    res_ref = refs[n_mats + int(has_gain)] if has_res else None
    o_ref, wbf, stage, sem = refs[n_mats + int(has_gain) + int(has_res):]
    j = pl.program_id(0)
    i = pl.program_id(1)
    _, _, k, tn = wbf.shape
    kc = stage.shape[2]
    n_chunks = k // kc

    def chunk_copy(e, jj, c, m):
        src = w_hbm[m].at[e, pl.ds(pl.multiple_of(c * kc, kc), kc), pl.ds(pl.multiple_of(jj * tn, LANES), tn)]
        return pltpu.make_async_copy(src, stage.at[c & 1, m], sem.at[c & 1, m])

    def prime(e, jj):
        for c in range(2):
            for m in range(n_mats):
                chunk_copy(e, jj, c, m).start()

    def fetch(e, jj, slot, lo, hi):
        def body(c, carry):
            for m in range(n_mats):
                chunk_copy(e, jj, c, m).wait()
                wbf[slot, m, pl.ds(pl.multiple_of(c * kc, kc), kc), :] = stage[c & 1, m].astype(_BF16)

                @pl.when(c + 2 < n_chunks)
                def _():
                    chunk_copy(e, jj, c + 2, m).start()
            return carry

        lax.fori_loop(lo, hi, body, 0)

    expert = plan_ref[_P_EXPERT, i]
    used = i < nu_ref[0]
    cur = (j * plan_ref[_P_NGROUPS, 0] + plan_ref[_P_GROUP, i]) & 1
    in_last_group = plan_ref[_P_LAST, i] == 1
    has_next = jnp.logical_and(used, jnp.logical_not(jnp.logical_and(in_last_group, j == n_col_tiles - 1)))
    next_e = plan_ref[_P_NEXT, i]
    next_j = jnp.where(in_last_group, j + 1, j)

    @pl.when(jnp.logical_and(j == 0, i == 0))
    def _():
        prime(expert, 0)
        fetch(expert, 0, 0, 0, n_chunks)

    @pl.when(jnp.logical_and(has_next, plan_ref[_P_FIRST, i] == 1))
    def _():
        prime(next_e, next_j)

    @pl.when(used)
    def _():
        a = a_ref[...]
        if has_gain:
            a = _rms(a, gain_ref[...]).astype(_BF16)
        if n_mats == 2:
            g = jnp.dot(a, wbf[cur, 0], preferred_element_type=_F32)
            u = jnp.dot(a, wbf[cur, 1], preferred_element_type=_F32)
            out = g * _sigmoid(g) * u
        else:
            out = jnp.dot(a, wbf[cur, 0], preferred_element_type=_F32)
            if has_res:
                out = out + res_ref[...]
        o_ref[...] = out.astype(o_ref.dtype)

    @pl.when(has_next)
    def _():
        fetch(next_e, next_j, 1 - cur, plan_ref[_P_LO, i], plan_ref[_P_HI, i])

    @pl.when(jnp.logical_not(used))
    def _():
        o_ref[...] = jnp.zeros_like(o_ref)


def _chunk_shares(index, size, n_chunks):
    return (index * n_chunks) // size, ((index + 1) * n_chunks) // size


def streamed_matmul(a, weights, groups, n_used, *, tm, tn, n_chunks, out_dtype, gain=None, residual=None):
    expert, first, index, size, nxt, last, group, n_groups = groups
    lo, hi = _chunk_shares(index, size, n_chunks)
    plan = jnp.stack([expert, first, lo, hi, nxt, last, group, n_groups]).astype(jnp.int32)
    m, k = a.shape
    n = weights[0].shape[2]
    n_mats = len(weights)

    def row(j, i, plan, nu):
        return jnp.minimum(i, nu[0] - 1)

    in_specs = [pl.BlockSpec((tm, k), lambda j, i, plan, nu: (row(j, i, plan, nu), 0))]
    in_specs += [pl.BlockSpec(memory_space=pl.ANY)] * n_mats
    args = [a, *weights]
    if gain is not None:
        in_specs.append(pl.BlockSpec((1, k), lambda j, i, plan, nu: (0, 0)))
        args.append(gain.reshape(1, k))
    if residual is not None:
        in_specs.append(pl.BlockSpec((tm, tn), lambda j, i, plan, nu: (row(j, i, plan, nu), j)))
        args.append(residual)
    return pl.pallas_call(
        functools.partial(_streamed_kernel, n_mats=n_mats, has_gain=gain is not None, has_res=residual is not None,
                          n_col_tiles=n // tn),
        out_shape=jax.ShapeDtypeStruct((m, n), out_dtype),
        grid_spec=pltpu.PrefetchScalarGridSpec(
            num_scalar_prefetch=2,
            grid=(n // tn, m // tm),
            in_specs=in_specs,
            out_specs=pl.BlockSpec((tm, tn), lambda j, i, plan, nu: (i, j)),
            scratch_shapes=[pltpu.VMEM((2, n_mats, k, tn), _BF16),
                            pltpu.VMEM((2, n_mats, k // n_chunks, tn), _F32),
                            pltpu.SemaphoreType.DMA((2, n_mats))],
        ),
        compiler_params=_params("arbitrary", "arbitrary"),
        name="streamed_swiglu_up" if n_mats == 2 else "streamed_matmul",
    )(plan, n_used, *args)


def _dense_stream_plan(m, tm):
    n_tiles = m // tm
    ids = jnp.arange(n_tiles, dtype=jnp.int32)
    zero = jnp.zeros_like(ids)
    one = jnp.ones_like(ids)
    groups = (zero, (ids == 0).astype(jnp.int32), ids, one * n_tiles, zero, one, zero, one)
    return groups, jnp.full((1,), n_tiles, jnp.int32)


def _dense_plan(m, tm):
    n_tiles = m // tm
    return jnp.zeros((n_tiles,), jnp.int32), jnp.full((1,), n_tiles, jnp.int32)


def dense_matmul(a, w, *, tn, out_dtype, gain=None, residual=None, n_out=None):
    te, nu = _dense_plan(a.shape[0], ROW_TILE)
    w3 = w if w.ndim == 3 else w[None]
    return grouped_matmul(a, w3, te, nu, tm=ROW_TILE, tn=tn, out_dtype=out_dtype, gain=gain, residual=residual,
                          n_out=n_out)


def _recurrent_kernel(q_ref, k_ref, v_ref, o_ref, rq_ref, rk_ref, rv_ref, rg_ref,
                      gt_ref, gc_ref, cos_ref, sin_ref, cwq_ref, cwk_ref, gm_ref, gr_ref, lg_ref,
                      mix_ref,
                      c_s, n_s, m_s, r_s, qbuf, kbuf):
    L, dh = q_ref.shape
    halo = SUBLANES
    inv_sqrt_d = dh ** -0.5

    @pl.when(pl.program_id(2) == 0)
    def _():
        c_s[...] = jnp.zeros_like(c_s)
        n_s[...] = jnp.zeros_like(n_s)
        m_s[...] = jnp.full_like(m_s, -jnp.inf)
        r_s[...] = jnp.zeros_like(r_s)
        qbuf[0:halo, :] = jnp.zeros((halo, dh), _F32)
        kbuf[0:halo, :] = jnp.zeros((halo, dh), _F32)

    def conv_silu(x_ref, buf, w_ref):
        buf[halo:halo + L, :] = x_ref[...]
        w = w_ref[...]
        y = w[CONV_WIDTH - 1:CONV_WIDTH, :] * buf[halo:halo + L, :]
        for j in range(CONV_WIDTH - 1):
            off = halo - (CONV_WIDTH - 1) + j
            y = y + w[j:j + 1, :] * buf[off:off + L, :]
        buf[0:halo, :] = buf[L:L + halo, :]
        return y * _sigmoid(y)

    row = lax.broadcasted_iota(jnp.int32, (L, L), 0)
    col = lax.broadcasted_iota(jnp.int32, (L, L), 1)
    causal = col <= row

    q = conv_silu(q_ref, qbuf, cwq_ref)
    k = conv_silu(k_ref, kbuf, cwk_ref)
    qb = q.astype(_BF16)
    vb = v_ref[...].astype(_BF16)

    def log_sigmoid(x):
        return jnp.minimum(x, 0.0) - jnp.log1p(jnp.exp(-jnp.abs(x)))

    gt = gt_ref[...]
    gc = gc_ref[...]
    i_row = gt[0:1, :]
    f_row = log_sigmoid(gt[1:2, :])
    i_col = gc[:, 0:1]
    f_col = log_sigmoid(gc[:, 1:2])
    hi = lax.Precision.HIGHEST
    b_col = jnp.dot(causal.astype(_F32), f_col, precision=hi, preferred_element_type=_F32)
    b_row = jnp.dot(f_row, (row <= col).astype(_F32), precision=hi, preferred_element_type=_F32)
    g_tot = b_col[L - 1:L, :]
    m_prev = m_s[...]

    log_d = jnp.where(causal, b_col - b_row + i_row, -jnp.inf)
    m_inter = b_col + m_prev
    m_t = jnp.maximum(jnp.max(log_d, axis=1, keepdims=True), m_inter)
    d_m = jnp.exp(log_d - m_t)
    inter = jnp.exp(m_inter - m_t)
    s = lax.dot_general(qb, k.astype(_BF16), _NT, preferred_element_type=_F32) * inv_sqrt_d
    s_m = s * d_m
    num = (jnp.dot(s_m.astype(_BF16), vb, preferred_element_type=_F32)
           + inter * jnp.dot(qb, c_s[...].astype(_BF16), preferred_element_type=_F32))
    den = jnp.sum(s_m, axis=1, keepdims=True) + inter * jnp.sum(q * n_s[...], axis=1, keepdims=True)
    h = num / jnp.maximum(jnp.abs(den), jnp.exp(-m_t))

    log_w = g_tot - b_col + i_col
    m_new = jnp.maximum(g_tot + m_prev, jnp.max(log_w, axis=0, keepdims=True))
    w_col = jnp.exp(log_w - m_new)
    decay = jnp.exp(g_tot + m_prev - m_new)
    kw = k * (w_col * inv_sqrt_d)
    c_s[...] = decay * c_s[...] + lax.dot_general(kw.astype(_BF16), vb, _TN, preferred_element_type=_F32)
    n_s[...] = decay * n_s[...] + jnp.sum(kw, axis=0, keepdims=True)
    m_s[...] = m_new

    mix_ref[:, :dh] = _rms(_sigmoid(o_ref[...]) * h, gm_ref[...]).astype(mix_ref.dtype)

    half = dh // 2
    cos = cos_ref[...]
    sin = sin_ref[...]

    def rope(x):
        x1 = x[:, :half]
        x2 = x[:, half:]
        return jnp.concatenate([x1 * cos - x2 * sin, x2 * cos + x1 * sin], axis=1)

    rq = rope(rq_ref[...]).astype(_BF16)
    rk = rope(rk_ref[...])
    rvb = rv_ref[...].astype(_BF16)
    lg = lg_ref[...]
    dist = (row - col).astype(_F32)
    d_r = jnp.where(causal, jnp.exp(lg * jnp.maximum(dist, 0.0)), 0.0)
    t_col = lax.broadcasted_iota(jnp.int32, (L, 1), 0).astype(_F32)
    q_decay = jnp.exp(lg * (t_col + 1.0))
    k_decay = jnp.exp(lg * (L - 1.0 - t_col))
    chunk_decay = jnp.exp(lg * L)
    s_r = lax.dot_general(rq, rk.astype(_BF16), _NT, preferred_element_type=_F32) * inv_sqrt_d * d_r
    out_r = (jnp.dot(s_r.astype(_BF16), rvb, preferred_element_type=_F32)
             + jnp.dot(rq, r_s[...].astype(_BF16), preferred_element_type=_F32) * q_decay)
    r_s[...] = chunk_decay * r_s[...] + lax.dot_general(
        (rk * (k_decay * inv_sqrt_d)).astype(_BF16), rvb, _TN, preferred_element_type=_F32)
    rg = rg_ref[...]
    mix_ref[:, dh:] = (rg * _sigmoid(rg) * _rms(out_r, gr_ref[...])).astype(mix_ref.dtype)


def recurrent_mixer(proj_m, proj_r, gates, cos, sin, conv_w, g_mlstm_out, g_ret_out, log_gamma):
    B, S, _ = proj_m.shape
    H, dh, L = MLSTM_HEADS, HEAD_DIM, REC_CHUNK
    g4 = gates.reshape(B, S, 2, H)
    gt = g4.transpose(0, 3, 2, 1)
    gc = g4.transpose(0, 3, 1, 2)

    def pspec(group):
        return pl.BlockSpec((None, L, dh), lambda b, h, c: (b, c, group * H + h))

    in_specs = [pspec(g) for g in range(4)] * 2 + [
        pl.BlockSpec((None, None, 2, L), lambda b, h, c: (b, h, 0, c)),
        pl.BlockSpec((None, None, L, 2), lambda b, h, c: (b, h, c, 0)),
        pl.BlockSpec((None, L, dh // 2), lambda b, h, c: (b, c, 0)),
        pl.BlockSpec((None, L, dh // 2), lambda b, h, c: (b, c, 0)),
        pl.BlockSpec((CONV_WIDTH, dh), lambda b, h, c: (0, h)),
        pl.BlockSpec((CONV_WIDTH, dh), lambda b, h, c: (0, H + h)),
        pl.BlockSpec((1, dh), lambda b, h, c: (0, h)),
        pl.BlockSpec((1, dh), lambda b, h, c: (0, h)),
        pl.BlockSpec((None, 1, 1), lambda b, h, c: (h, 0, 0)),
    ]
    return pl.pallas_call(
        _recurrent_kernel,
        out_shape=jax.ShapeDtypeStruct((B, S, 2 * H * dh), _BF16),
        grid=(B, H, S // L),
        in_specs=in_specs,
        out_specs=pl.BlockSpec((None, L, 2 * dh), lambda b, h, c: (b, c, h)),
        scratch_shapes=[pltpu.VMEM((dh, dh), _F32), pltpu.VMEM((1, dh), _F32), pltpu.VMEM((1, 1), _F32),
                        pltpu.VMEM((dh, dh), _F32),
                        pltpu.VMEM((L + SUBLANES, dh), _F32), pltpu.VMEM((L + SUBLANES, dh), _F32)],
        compiler_params=_params("parallel", "parallel", "arbitrary"),
        name="recurrent_mixer",
    )(*([proj_m] * 4), *([proj_r] * 4), gt, gc, cos, sin, conv_w, conv_w,
      g_mlstm_out.reshape(1, H * dh), g_ret_out.reshape(1, H * dh), log_gamma.reshape(H, 1, 1))


def _rope_pair(z2, cs, gvec):
    lane = lax.broadcasted_iota(jnp.int32, z2.shape, 1)
    first = lane < QK_ROPE
    ms = jnp.sum(jnp.where(first, z2 * z2, 0.0), axis=1, keepdims=True) * (1.0 / QK_ROPE)
    t = z2 * lax.rsqrt(ms + EPS) * (cs * gvec)
    return jnp.where(first, t + pltpu.roll(t, QK_ROPE, 1), 0.0)


def _mla_q_kernel(c_ref, gcq_ref, w_ref, gqn_ref, gqr_ref, cs_ref, q_ref, cn_s, *, scale):
    @pl.when(pl.program_id(1) == 0)
    def _():
        cn_s[...] = _rms(c_ref[...], gcq_ref[...]).astype(_BF16)

    dq = q_ref.shape[-1]
    z = jnp.dot(cn_s[...], w_ref[...].astype(_BF16), preferred_element_type=_F32)
    cs = cs_ref[...]
    for h in range(q_ref.shape[0]):
        zh = z[:, h * dq:(h + 1) * dq]
        qn = _rms(zh[:, :QK_NOPE], gqn_ref[...])
        qr = _rope_pair(zh[:, QK_NOPE:], cs, gqr_ref[...])
        q_ref[h] = (jnp.concatenate([qn, qr], axis=1) * scale).astype(q_ref.dtype)


def _mla_kv_kernel(c_ref, kr_ref, gckv_ref, w_ref, gkn_ref, gkr_ref, cs_ref, k_ref, v_ref, cn_s):
    @pl.when(pl.program_id(1) == 0)
    def _():
        cn_s[...] = _rms(c_ref[...], gckv_ref[...]).astype(_BF16)

    dk = k_ref.shape[-1]
    z = jnp.dot(cn_s[...], w_ref[...].astype(_BF16), preferred_element_type=_F32)
    kr = _rope_pair(kr_ref[...], cs_ref[...], gkr_ref[...])
    ones = jnp.ones((z.shape[0], LANES), _F32)
    for h in range(k_ref.shape[0]):
        zh = z[:, h * dk:(h + 1) * dk]
        kn = _rms(zh[:, :QK_NOPE], gkn_ref[...])
        k_ref[h] = jnp.concatenate([kn, kr], axis=1).astype(k_ref.dtype)
        v_ref[h] = jnp.concatenate([zh[:, QK_NOPE:], ones], axis=1).astype(v_ref.dtype)


def _mla_attn_kernel(q_ref, k_ref, v_ref, o_ref, m_s, acc_s):
    qi = pl.program_id(2)
    hb, tq, _ = q_ref.shape
    dv = v_ref.shape[-1]
    for h in range(hb):
        m_s[h] = jnp.full((tq, LANES), -jnp.inf, _F32)
        acc_s[h] = jnp.zeros((tq, dv), _F32)

    def block(h, start, diagonal):
        kb = k_ref[h, pl.ds(start, tq), :]
        vb = v_ref[h, pl.ds(start, tq), :]
        s = lax.dot_general(q_ref[h], kb, _NT, preferred_element_type=_F32)
        if diagonal:
            row = lax.broadcasted_iota(jnp.int32, s.shape, 0)
            col = lax.broadcasted_iota(jnp.int32, s.shape, 1)
            s = jnp.where(col <= row, s, -jnp.inf)
        m_prev = m_s[h]
        m_new = jnp.maximum(m_prev, jnp.max(s, axis=1, keepdims=True))
        p = jnp.exp2(s - jnp.concatenate([m_new] * (tq // LANES), axis=1))
        alpha = jnp.exp2(m_prev - m_new)
        acc_s[h] = jnp.concatenate([alpha] * (dv // LANES), axis=1) * acc_s[h] + jnp.dot(
            p.astype(_BF16), vb, preferred_element_type=_F32)
        m_s[h] = m_new

    def body(j, carry):
        for h in range(hb):
            block(h, pl.multiple_of(j * tq, tq), False)
        return carry

    lax.fori_loop(0, qi, body, 0)
    for h in range(hb):
        block(h, pl.multiple_of(qi * tq, tq), True)
    for h in range(hb):
        acc = acc_s[h]
        o_ref[:, h * V_HEAD:(h + 1) * V_HEAD] = (acc[:, :V_HEAD] / acc[:, V_HEAD:]).astype(o_ref.dtype)


def mla_attention(c, cs, w_q, w_kv, g_cq, g_ckv, g_qn, gv_q, g_kn, gv_k, B, S):
    T = c.shape[0]
    H, tm, hb = MLA_HEADS, ROW_TILE, MLA_HEAD_GROUP
    dq = QK_NOPE + 2 * QK_ROPE
    dv = 2 * V_HEAD
    scale = (QK_NOPE + QK_ROPE) ** -0.5 * LOG2_E
    nt = S // tm
    vec = lambda n: pl.BlockSpec((1, n), lambda i, h: (0, 0))
    head_out = lambda d: pl.BlockSpec((None, hb, tm, d), lambda i, h: (i // nt, h, i % nt, 0))
    q = pl.pallas_call(
        functools.partial(_mla_q_kernel, scale=scale),
        out_shape=jax.ShapeDtypeStruct((B, H, S, dq), _BF16),
        grid=(T // tm, H // hb),
        in_specs=[pl.BlockSpec((tm, Q_LORA), lambda i, h: (i, 0)), vec(Q_LORA),
                  pl.BlockSpec((Q_LORA, hb * dq), lambda i, h: (0, h)),
                  vec(QK_NOPE), vec(LANES),
                  pl.BlockSpec((tm, LANES), lambda i, h: (i, 0))],
        out_specs=head_out(dq),
        scratch_shapes=[pltpu.VMEM((tm, Q_LORA), _BF16)],
        compiler_params=_params("parallel", "arbitrary"),
        name="mla_q_proj",
    )(c, g_cq.reshape(1, -1), w_q, g_qn.reshape(1, -1), gv_q, cs)
    k, v = pl.pallas_call(
        _mla_kv_kernel,
        out_shape=[jax.ShapeDtypeStruct((B, H, S, dq), _BF16), jax.ShapeDtypeStruct((B, H, S, dv), _BF16)],
        grid=(T // tm, H // hb),
        in_specs=[pl.BlockSpec((tm, KV_LORA), lambda i, h: (i, 1)),
                  pl.BlockSpec((tm, LANES), lambda i, h: (i, (Q_LORA + KV_LORA) // LANES)),
                  vec(KV_LORA),
                  pl.BlockSpec((KV_LORA, hb * (QK_NOPE + V_HEAD)), lambda i, h: (0, h)),
                  vec(QK_NOPE), vec(LANES),
                  pl.BlockSpec((tm, LANES), lambda i, h: (i, 0))],
        out_specs=[head_out(dq), head_out(dv)],
        scratch_shapes=[pltpu.VMEM((tm, KV_LORA), _BF16)],
        compiler_params=_params("parallel", "arbitrary"),
        name="mla_kv_proj",
    )(c, c, g_ckv.reshape(1, -1), w_kv, g_kn.reshape(1, -1), gv_k, cs)
    tq = ATTN_TILE
    o = pl.pallas_call(
        _mla_attn_kernel,
        out_shape=jax.ShapeDtypeStruct((B, S, H * V_HEAD), _BF16),
        grid=(B, H // hb, S // tq),
        in_specs=[pl.BlockSpec((None, hb, tq, dq), lambda b, h, i: (b, h, i, 0)),
                  pl.BlockSpec((None, hb, S, dq), lambda b, h, i: (b, h, 0, 0)),
                  pl.BlockSpec((None, hb, S, dv), lambda b, h, i: (b, h, 0, 0))],
        out_specs=pl.BlockSpec((None, tq, hb * V_HEAD), lambda b, h, i: (b, i, h)),
        scratch_shapes=[pltpu.VMEM((hb, tq, LANES), _F32), pltpu.VMEM((hb, tq, dv), _F32)],
        compiler_params=_params("parallel", "parallel", "arbitrary"),
        name="mla_attention",
    )(q, k, v)
    return o.reshape(T, H * V_HEAD)


def _router_kernel(x_ref, g_ref, w_ref, r_ref):
    hn = _rms(x_ref[...], g_ref[...])
    logits = jnp.dot(hn, w_ref[...], precision=lax.Precision.HIGHEST, preferred_element_type=_F32)
    lane = lax.broadcasted_iota(jnp.int32, logits.shape, 1)
    lane_f = lane.astype(_F32)
    neg = -jnp.inf
    lg = jnp.where(lane < N_EXPERTS, logits, neg)
    v1 = jnp.max(lg, axis=1, keepdims=True)
    i1 = jnp.min(jnp.where(lg == v1, lane_f, float(LANES)), axis=1, keepdims=True)
    lg2 = jnp.where(lane_f == i1, neg, lg)
    v2 = jnp.max(lg2, axis=1, keepdims=True)
    i2 = jnp.min(jnp.where(lg2 == v2, lane_f, float(LANES)), axis=1, keepdims=True)
    e2 = jnp.exp(v2 - v1)
    w1 = 1.0 / (1.0 + e2)
    w2 = e2 / (1.0 + e2)
    out = jnp.where(lane < N_EXPERTS, logits, 0.0)
    out = jnp.where(lane == N_EXPERTS, i1, out)
    out = jnp.where(lane == N_EXPERTS + 1, i2, out)
    out = jnp.where(lane == N_EXPERTS + 2, w1, out)
    out = jnp.where(lane == N_EXPERTS + 3, w2, out)
    r_ref[...] = out


def router(x, g, w_router_padded):
    T, d = x.shape
    return pl.pallas_call(
        _router_kernel,
        out_shape=jax.ShapeDtypeStruct((T, LANES), _F32),
        grid=(T // ROW_TILE,),
        in_specs=[pl.BlockSpec((ROW_TILE, d), lambda i: (i, 0)),
                  pl.BlockSpec((1, d), lambda i: (0, 0)),
                  pl.BlockSpec((d, LANES), lambda i: (0, 0))],
        out_specs=pl.BlockSpec((ROW_TILE, LANES), lambda i: (i, 0)),
        compiler_params=_params("parallel"),
        name="moe_router",
    )(x, g.reshape(1, d), w_router_padded)


def _row_copy(src_hbm, dst_vmem, sem, src_row, dst_row):
    return pltpu.make_async_copy(src_hbm.at[pl.ds(src_row, 1)], dst_vmem.at[pl.ds(dst_row, 1)], sem)


def _index_spec(n):
    return pl.BlockSpec((None, 1, n), lambda i: (i, 0, 0), memory_space=pltpu.SMEM)


def _gather_norm_kernel(idx_ref, x_hbm, g_ref, o_ref, buf, sem):
    tr = buf.shape[0]

    def start(r, carry):
        _row_copy(x_hbm, buf, sem, idx_ref[0, r], r).start()
        return carry

    def wait(r, carry):
        _row_copy(x_hbm, buf, sem, 0, r).wait()
        return carry

    lax.fori_loop(0, tr, start, 0)
    lax.fori_loop(0, tr, wait, 0)
    o_ref[...] = _rms(buf[...], g_ref[...]).astype(o_ref.dtype)


def gather_norm(x, g, src_rows, tr):
    T, d = x.shape
    r = src_rows.shape[0]
    return pl.pallas_call(
        _gather_norm_kernel,
        out_shape=jax.ShapeDtypeStruct((r, d), _BF16),
        grid=(r // tr,),
        in_specs=[_index_spec(tr), pl.BlockSpec(memory_space=pl.ANY), pl.BlockSpec((1, d), lambda i: (0, 0))],
        out_specs=pl.BlockSpec((tr, d), lambda i: (i, 0)),
        scratch_shapes=[pltpu.VMEM((tr, d), _F32), pltpu.SemaphoreType.DMA],
        compiler_params=_params("arbitrary"),
        name="moe_gather_norm",
    )(src_rows.reshape(r // tr, 1, tr), x, g.reshape(1, d))


def _combine_kernel(pos_ref, x_ref, w_ref, y_hbm, o_ref, buf_a, buf_b, sem):
    tt = buf_a.shape[0]

    def start(r, carry):
        _row_copy(y_hbm, buf_a, sem, pos_ref[0, TOP_K * r], r).start()
        _row_copy(y_hbm, buf_b, sem, pos_ref[0, TOP_K * r + 1], r).start()
        return carry

    def wait(r, carry):
        _row_copy(y_hbm, buf_a, sem, 0, r).wait()
        _row_copy(y_hbm, buf_b, sem, 0, r).wait()
        return carry

    lax.fori_loop(0, tt, start, 0)
    lax.fori_loop(0, tt, wait, 0)
    w = w_ref[...]
    o_ref[...] = x_ref[...] + w[:, 0:1] * buf_a[...] + w[:, 1:2] * buf_b[...]


def moe_combine(x, y, pos_flat, weights, tt):
    T, d = x.shape
    return pl.pallas_call(
        _combine_kernel,
        out_shape=jax.ShapeDtypeStruct((T, d), _F32),
        grid=(T // tt,),
        in_specs=[_index_spec(tt * TOP_K), pl.BlockSpec((tt, d), lambda i: (i, 0)),
                  pl.BlockSpec((tt, TOP_K), lambda i: (i, 0)), pl.BlockSpec(memory_space=pl.ANY)],
        out_specs=pl.BlockSpec((tt, d), lambda i: (i, 0)),
        scratch_shapes=[pltpu.VMEM((tt, d), _F32), pltpu.VMEM((tt, d), _F32), pltpu.SemaphoreType.DMA],
        compiler_params=_params("arbitrary"),
        name="moe_combine",
    )(pos_flat.reshape(T // tt, 1, tt * TOP_K), x, weights, y)


def _moe_plan(route, tm, n_tiles):
    T = route.shape[0]
    e_flat = route[:, N_EXPERTS:N_EXPERTS + TOP_K].astype(jnp.int32).reshape(-1)
    onehot = (e_flat[:, None] == jnp.arange(N_EXPERTS, dtype=jnp.int32)[None, :]).astype(jnp.int32)
    csum = jnp.cumsum(onehot, axis=0)
    rank = jnp.sum(csum * onehot, axis=1) - 1
    tiles_e = (csum[-1] + tm - 1) // tm
    tile_end = jnp.cumsum(tiles_e)
    tile_start = tile_end - tiles_e
    pos = jnp.sum(onehot * tile_start[None, :], axis=1) * tm + rank
    tile_ids = jnp.arange(n_tiles, dtype=jnp.int32)
    tile_expert = jnp.minimum(jnp.sum((tile_ids[:, None] >= tile_end[None, :]).astype(jnp.int32), axis=1),
                              N_EXPERTS - 1).astype(jnp.int32)
    n_used = tile_end[-1:].astype(jnp.int32)
    rows = n_tiles * tm
    src = jnp.zeros((rows,), jnp.int32).at[pos].set(jnp.arange(T * TOP_K, dtype=jnp.int32) // TOP_K)
    experts = jnp.arange(N_EXPERTS, dtype=jnp.int32)
    has = tiles_e > 0
    later = jnp.where(jnp.logical_and(has[None, :], experts[None, :] > experts[:, None]), experts[None, :], N_EXPERTS)
    next_e = jnp.min(later, axis=1)
    last_e = next_e == N_EXPERTS
    next_e = jnp.where(last_e, jnp.min(jnp.where(has, experts, N_EXPERTS)), next_e)
    group_e = jnp.cumsum(has.astype(jnp.int32)) - 1
    tile_onehot = (tile_expert[:, None] == experts[None, :]).astype(jnp.int32)

    def of_tile(per_expert):
        return jnp.sum(tile_onehot * per_expert.astype(jnp.int32)[None, :], axis=1)

    index = tile_ids - of_tile(tile_start)
    groups = (tile_expert, (index == 0).astype(jnp.int32), index, jnp.maximum(of_tile(tiles_e), 1),
              of_tile(next_e), of_tile(last_e), of_tile(group_e),
              jnp.full((n_tiles,), jnp.sum(has.astype(jnp.int32)), jnp.int32))
    return groups, n_used, src, pos.astype(jnp.int32)


def moe_layer(x, g, w_router, we_gate, we_up, we_down):
    T, d = x.shape
    tm = MOE_ROW_TILE
    n_tiles = (T * TOP_K) // tm + N_EXPERTS
    w_r = jnp.zeros((d, LANES), _F32).at[:, :N_EXPERTS].set(w_router)
    route = router(x, g, w_r)
    plan, n_used, src, pos = _moe_plan(route, tm, n_tiles)
    xs = gather_norm(x, g, src, tm)
    h = streamed_matmul(xs, [we_gate, we_up], plan, n_used, tm=tm, tn=UP_COLS, n_chunks=UP_CHUNKS, out_dtype=_BF16)
    y = streamed_matmul(h, [we_down], plan, n_used, tm=tm, tn=DOWN_COLS, n_chunks=DOWN_CHUNKS, out_dtype=_F32)
    weights = route[:, N_EXPERTS + TOP_K:N_EXPERTS + 2 * TOP_K]
    return moe_combine(x, y, pos, weights, tm)


def _rope_tables(positions, half):
    inv = ROPE_BASE ** (-jnp.arange(half, dtype=_F32) / half)
    ang = positions.astype(_F32)[..., None] * inv
    return jnp.cos(ang), jnp.sin(ang)


def _rotate_half_cols(w, width):
    lead = w.shape[:-1]
    w2 = w.reshape(lead + (-1, 2, width // 2))
    return jnp.stack([-w2[..., 1, :], w2[..., 0, :]], axis=-2).reshape(w.shape)


def _swap_halves(g):
    half = g.shape[-1] // 2
    return jnp.concatenate([g[..., half:], g[..., :half]], axis=-1)


def kernel(x, positions, g_mix_norm, g_ffn_norm, w_in, conv_w, b_gates, g_mlstm_out, g_ret_out, w_mix_out,
           ffn_gate, ffn_up, ffn_down, w_dqkv, g_cq, g_ckv, w_uq, w_ukv, g_qn, g_qr, g_kn, g_kr, w_o,
           w_router, we_gate, we_up, we_down):
    B, S, D = x.shape
    T = B * S
    xf = x.reshape(T, D)
    n_gate = 2 * MLSTM_HEADS
    main = 4 * MLSTM_HEADS * HEAD_DIM

    w = w_in[0]
    w_gate = jnp.zeros((D, LANES), _F32).at[:, :n_gate].set(w[:, main:main + n_gate])
    g0 = g_mix_norm[0]
    proj_m = dense_matmul(xf, w_in, tn=1024, out_dtype=_F32, gain=g0, n_out=main)
    proj_r = dense_matmul(xf, w[:, main + n_gate:], tn=1024, out_dtype=_F32, gain=g0)
    gates = dense_matmul(xf, w_gate, tn=LANES, out_dtype=_F32, gain=g0)[:, :n_gate] + b_gates[0][None, :]
    cos_r, sin_r = _rope_tables(positions, HEAD_DIM // 2)
    log_gamma = jnp.log1p(-jnp.exp2(-5.0 - jnp.arange(RET_HEADS, dtype=_F32)))
    mix = recurrent_mixer(proj_m.reshape(B, S, -1), proj_r.reshape(B, S, -1), gates.reshape(B, S, n_gate),
                          cos_r, sin_r, conv_w[0], g_mlstm_out[0], g_ret_out[0], log_gamma).reshape(T, -1)
    w_mix = w_mix_out[0].reshape(2, MLSTM_HEADS, HEAD_DIM, D).transpose(1, 0, 2, 3).reshape(-1, D)
    xf = dense_matmul(mix, w_mix, tn=1024, out_dtype=_F32, residual=xf)

    plan, nu = _dense_stream_plan(T, ROW_TILE)
    hmid = streamed_matmul(xf, [ffn_gate, ffn_up], plan, nu, tm=ROW_TILE, tn=UP_COLS, n_chunks=UP_CHUNKS,
                           out_dtype=_BF16, gain=g_ffn_norm[0])
    xf = streamed_matmul(hmid, [ffn_down], plan, nu, tm=ROW_TILE, tn=DENSE_DOWN_COLS, n_chunks=DOWN_CHUNKS,
                         out_dtype=_F32, residual=xf)

    H = MLA_HEADS
    wd = w_dqkv[0]
    w_kr = wd[:, Q_LORA + KV_LORA:]
    wd_full = jnp.concatenate([wd, _rotate_half_cols(w_kr, QK_ROPE)], axis=1)
    c = dense_matmul(xf, wd_full, tn=wd_full.shape[1] // 3, out_dtype=_F32, gain=g_mix_norm[1])
    wq = w_uq[0].reshape(Q_LORA, H, QK_NOPE + QK_ROPE)
    wq_r = wq[..., QK_NOPE:]
    w_q = jnp.concatenate([wq, _rotate_half_cols(wq_r, QK_ROPE)], axis=-1).reshape(Q_LORA, -1)
    cos_m, sin_m = _rope_tables(positions, QK_ROPE // 2)
    cs = jnp.concatenate([cos_m, cos_m, sin_m, sin_m], axis=-1).reshape(T, LANES)
    gv_q = jnp.concatenate([g_qr[0], _swap_halves(g_qr[0])]).reshape(1, LANES)
    gv_k = jnp.concatenate([g_kr[0], _swap_halves(g_kr[0])]).reshape(1, LANES)
    attn = mla_attention(c, cs, w_q, w_ukv[0], g_cq[0], g_ckv[0], g_qn[0], gv_q, g_kn[0], gv_k, B, S)
    xf = dense_matmul(attn, w_o[0], tn=1024, out_dtype=_F32, residual=xf)

    xf = moe_layer(xf, g_ffn_norm[1], w_router[0], we_gate[0], we_up[0], we_down[0])
    return xf.reshape(B, S, D)
```

```python
import functools

import jax
import jax.numpy as jnp
from jax import lax
from jax.experimental import pallas as pl
from jax.experimental.pallas import tpu as pltpu

EPS = 1e-6
ROPE_BASE = 10000.0
CONV_WIDTH = 4
MLSTM_HEADS = 4
RET_HEADS = 4
HEAD_DIM = 256
REC_CHUNK = 256
MLA_HEADS = 16
Q_LORA = 512
KV_LORA = 512
QK_NOPE = 128
QK_ROPE = 64
V_HEAD = 128
N_EXPERTS = 8
TOP_K = 2

LANES = 128
SUBLANES = 8
VMEM_LIMIT_BYTES = 48 * 1024 * 1024

ROW_TILE = 512
MOE_ROW_TILE = 256
UP_COLS = 1408
UP_CHUNKS = 8
DOWN_COLS = 1024
DENSE_DOWN_COLS = 512
DOWN_CHUNKS = 11
ATTN_TILE = 512
MLA_HEAD_GROUP = 4
GATHER_CHUNK = 512
LOG2_E = 1.4426950408889634

_F32 = jnp.float32
_BF16 = jnp.bfloat16
_NT = (((1,), (1,)), ((), ()))
_TN = (((0,), (0,)), ((), ()))


def _params(*semantics):
    return pltpu.CompilerParams(dimension_semantics=semantics, vmem_limit_bytes=VMEM_LIMIT_BYTES)


def _sigmoid(x):
    return 1.0 / (1.0 + jnp.exp(-x))


def _rms(x, g):
    return x * lax.rsqrt(jnp.mean(x * x, axis=-1, keepdims=True) + EPS) * g


def _gmm_kernel(te_ref, nu_ref, a_ref, w_ref, *rest, has_gain, has_res):
    del te_ref
    o_ref = rest[-1]

    @pl.when(pl.program_id(1) < nu_ref[0])
    def _():
        a = a_ref[...]
        if has_gain:
            a = _rms(a, rest[0][...]).astype(_BF16)
        acc = jnp.dot(a, w_ref[...].astype(_BF16), preferred_element_type=_F32)
        if has_res:
            acc = acc + rest[int(has_gain)][...]
        o_ref[...] = acc.astype(o_ref.dtype)

    @pl.when(pl.program_id(1) >= nu_ref[0])
    def _():
        o_ref[...] = jnp.zeros_like(o_ref)


def grouped_matmul(a, w, tile_expert, n_used, *, tm, tn, out_dtype, gain=None, residual=None, n_out=None):
    m, k = a.shape
    n = w.shape[2] if n_out is None else n_out
    n_tiles = m // tm

    def row(j, i, te, nu):
        return jnp.minimum(i, nu[0] - 1)

    in_specs = [pl.BlockSpec((tm, k), lambda j, i, te, nu: (row(j, i, te, nu), 0)),
                pl.BlockSpec((None, k, tn), lambda j, i, te, nu: (te[row(j, i, te, nu)], 0, j))]
    args = [a, w]
    if gain is not None:
        in_specs.append(pl.BlockSpec((1, k), lambda j, i, te, nu: (0, 0)))
        args.append(gain.reshape(1, k))
    if residual is not None:
        in_specs.append(pl.BlockSpec((tm, tn), lambda j, i, te, nu: (row(j, i, te, nu), j)))
        args.append(residual)
    return pl.pallas_call(
        functools.partial(_gmm_kernel, has_gain=gain is not None, has_res=residual is not None),
        out_shape=jax.ShapeDtypeStruct((m, n), out_dtype),
        grid_spec=pltpu.PrefetchScalarGridSpec(
            num_scalar_prefetch=2,
            grid=(n // tn, n_tiles),
            in_specs=in_specs,
            out_specs=pl.BlockSpec((tm, tn), lambda j, i, te, nu: (i, j)),
        ),
        compiler_params=_params("arbitrary", "arbitrary"),
        name="grouped_matmul",
    )(tile_expert, n_used, *args)


_P_EXPERT, _P_FIRST, _P_LO, _P_HI, _P_NEXT, _P_LAST, _P_GROUP, _P_NGROUPS = range(8)


def _streamed_kernel(plan_ref, nu_ref, a_ref, *refs, n_mats, has_gain, has_res, n_col_tiles):
    w_hbm = refs[:n_mats]
    gain_ref = refs[n_mats] if has_gain else None
    res_ref = refs[n_mats + int(has_gain)] if has_res else None
    o_ref, wbf, stage, sem = refs[n_mats + int(has_gain) + int(has_res):]
    j = pl.program_id(0)
    i = pl.program_id(1)
    _, _, k, tn = wbf.shape
    kc = stage.shape[2]
    n_chunks = k // kc

    def chunk_copy(e, jj, c, m):
        src = w_hbm[m].at[e, pl.ds(pl.multiple_of(c * kc, kc), kc), pl.ds(pl.multiple_of(jj * tn, LANES), tn)]
        return pltpu.make_async_copy(src, stage.at[c & 1, m], sem.at[c & 1, m])

    def prime(e, jj):
        for c in range(2):
            for m in range(n_mats):
                chunk_copy(e, jj, c, m).start()

    def fetch(e, jj, slot, lo, hi):
        def body(c, carry):
            for m in range(n_mats):
                chunk_copy(e, jj, c, m).wait()
                wbf[slot, m, pl.ds(pl.multiple_of(c * kc, kc), kc), :] = stage[c & 1, m].astype(_BF16)

                @pl.when(c + 2 < n_chunks)
                def _():
                    chunk_copy(e, jj, c + 2, m).start()
            return carry

        lax.fori_loop(lo, hi, body, 0)

    expert = plan_ref[_P_EXPERT, i]
    used = i < nu_ref[0]
    cur = (j * plan_ref[_P_NGROUPS, 0] + plan_ref[_P_GROUP, i]) & 1
    in_last_group = plan_ref[_P_LAST, i] == 1
    has_next = jnp.logical_and(used, jnp.logical_not(jnp.logical_and(in_last_group, j == n_col_tiles - 1)))
    next_e = plan_ref[_P_NEXT, i]
    next_j = jnp.where(in_last_group, j + 1, j)

    @pl.when(jnp.logical_and(j == 0, i == 0))
    def _():
        prime(expert, 0)
        fetch(expert, 0, 0, 0, n_chunks)

    @pl.when(jnp.logical_and(has_next, plan_ref[_P_FIRST, i] == 1))
    def _():
        prime(next_e, next_j)

    @pl.when(used)
    def _():
        a = a_ref[...]
        if has_gain:
            a = _rms(a, gain_ref[...]).astype(_BF16)
        if n_mats == 2:
            g = jnp.dot(a, wbf[cur, 0], preferred_element_type=_F32)
            u = jnp.dot(a, wbf[cur, 1], preferred_element_type=_F32)
            out = g * _sigmoid(g) * u
        else:
            out = jnp.dot(a, wbf[cur, 0], preferred_element_type=_F32)
            if has_res:
                out = out + res_ref[...]
        o_ref[...] = out.astype(o_ref.dtype)

    @pl.when(has_next)
    def _():
        fetch(next_e, next_j, 1 - cur, plan_ref[_P_LO, i], plan_ref[_P_HI, i])

    @pl.when(jnp.logical_not(used))
    def _():
        o_ref[...] = jnp.zeros_like(o_ref)


def _chunk_shares(index, size, n_chunks):
    return (index * n_chunks) // size, ((index + 1) * n_chunks) // size


def streamed_matmul(a, weights, groups, n_used, *, tm, tn, n_chunks, out_dtype, gain=None, residual=None):
    expert, first, index, size, nxt, last, group, n_groups = groups
    lo, hi = _chunk_shares(index, size, n_chunks)
    plan = jnp.stack([expert, first, lo, hi, nxt, last, group, n_groups]).astype(jnp.int32)
    m, k = a.shape
    n = weights[0].shape[2]
    n_mats = len(weights)

    def row(j, i, plan, nu):
        return jnp.minimum(i, nu[0] - 1)

    in_specs = [pl.BlockSpec((tm, k), lambda j, i, plan, nu: (row(j, i, plan, nu), 0))]
    in_specs += [pl.BlockSpec(memory_space=pl.ANY)] * n_mats
    args = [a, *weights]
    if gain is not None:
        in_specs.append(pl.BlockSpec((1, k), lambda j, i, plan, nu: (0, 0)))
        args.append(gain.reshape(1, k))
    if residual is not None:
        in_specs.append(pl.BlockSpec((tm, tn), lambda j, i, plan, nu: (row(j, i, plan, nu), j)))
        args.append(residual)
    return pl.pallas_call(
        functools.partial(_streamed_kernel, n_mats=n_mats, has_gain=gain is not None, has_res=residual is not None,
                          n_col_tiles=n // tn),
        out_shape=jax.ShapeDtypeStruct((m, n), out_dtype),
        grid_spec=pltpu.PrefetchScalarGridSpec(
            num_scalar_prefetch=2,
            grid=(n // tn, m // tm),
            in_specs=in_specs,
            out_specs=pl.BlockSpec((tm, tn), lambda j, i, plan, nu: (i, j)),
            scratch_shapes=[pltpu.VMEM((2, n_mats, k, tn), _BF16),
                            pltpu.VMEM((2, n_mats, k // n_chunks, tn), _F32),
                            pltpu.SemaphoreType.DMA((2, n_mats))],
        ),
        compiler_params=_params("arbitrary", "arbitrary"),
        name="streamed_swiglu_up" if n_mats == 2 else "streamed_matmul",
    )(plan, n_used, *args)


def _dense_stream_plan(m, tm):
    n_tiles = m // tm
    ids = jnp.arange(n_tiles, dtype=jnp.int32)
    zero = jnp.zeros_like(ids)
    one = jnp.ones_like(ids)
    groups = (zero, (ids == 0).astype(jnp.int32), ids, one * n_tiles, zero, one, zero, one)
    return groups, jnp.full((1,), n_tiles, jnp.int32)


def _dense_plan(m, tm):
    n_tiles = m // tm
    return jnp.zeros((n_tiles,), jnp.int32), jnp.full((1,), n_tiles, jnp.int32)


def dense_matmul(a, w, *, tn, out_dtype, gain=None, residual=None, n_out=None):
    te, nu = _dense_plan(a.shape[0], ROW_TILE)
    w3 = w if w.ndim == 3 else w[None]
    return grouped_matmul(a, w3, te, nu, tm=ROW_TILE, tn=tn, out_dtype=out_dtype, gain=gain, residual=residual,
                          n_out=n_out)


def _recurrent_kernel(q_ref, k_ref, v_ref, o_ref, rq_ref, rk_ref, rv_ref, rg_ref,
                      gt_ref, gc_ref, cos_ref, sin_ref, cwq_ref, cwk_ref, gm_ref, gr_ref, lg_ref,
                      mix_ref,
                      c_s, n_s, m_s, r_s, qbuf, kbuf):
    L, dh = q_ref.shape
    halo = SUBLANES
    inv_sqrt_d = dh ** -0.5

    @pl.when(pl.program_id(2) == 0)
    def _():
        c_s[...] = jnp.zeros_like(c_s)
        n_s[...] = jnp.zeros_like(n_s)
        m_s[...] = jnp.full_like(m_s, -jnp.inf)
        r_s[...] = jnp.zeros_like(r_s)
        qbuf[0:halo, :] = jnp.zeros((halo, dh), _F32)
        kbuf[0:halo, :] = jnp.zeros((halo, dh), _F32)

    def conv_silu(x_ref, buf, w_ref):
        buf[halo:halo + L, :] = x_ref[...]
        w = w_ref[...]
        y = w[CONV_WIDTH - 1:CONV_WIDTH, :] * buf[halo:halo + L, :]
        for j in range(CONV_WIDTH - 1):
            off = halo - (CONV_WIDTH - 1) + j
            y = y + w[j:j + 1, :] * buf[off:off + L, :]
        buf[0:halo, :] = buf[L:L + halo, :]
        return y * _sigmoid(y)

    row = lax.broadcasted_iota(jnp.int32, (L, L), 0)
    col = lax.broadcasted_iota(jnp.int32, (L, L), 1)
    causal = col <= row

    q = conv_silu(q_ref, qbuf, cwq_ref)
    k = conv_silu(k_ref, kbuf, cwk_ref)
    qb = q.astype(_BF16)
    vb = v_ref[...].astype(_BF16)

    def log_sigmoid(x):
        return jnp.minimum(x, 0.0) - jnp.log1p(jnp.exp(-jnp.abs(x)))

    gt = gt_ref[...]
    gc = gc_ref[...]
    i_row = gt[0:1, :]
    f_row = log_sigmoid(gt[1:2, :])
    i_col = gc[:, 0:1]
    f_col = log_sigmoid(gc[:, 1:2])
    hi = lax.Precision.HIGHEST
    b_col = jnp.dot(causal.astype(_F32), f_col, precision=hi, preferred_element_type=_F32)
    b_row = jnp.dot(f_row, (row <= col).astype(_F32), precision=hi, preferred_element_type=_F32)
    g_tot = b_col[L - 1:L, :]
    m_prev = m_s[...]

    log_d = jnp.where(causal, b_col - b_row + i_row, -jnp.inf)
    m_inter = b_col + m_prev
    m_t = jnp.maximum(jnp.max(log_d, axis=1, keepdims=True), m_inter)
    d_m = jnp.exp(log_d - m_t)
    inter = jnp.exp(m_inter - m_t)
    s = lax.dot_general(qb, k.astype(_BF16), _NT, preferred_element_type=_F32) * inv_sqrt_d
    s_m = s * d_m
    num = (jnp.dot(s_m.astype(_BF16), vb, preferred_element_type=_F32)
           + inter * jnp.dot(qb, c_s[...].astype(_BF16), preferred_element_type=_F32))
    den = jnp.sum(s_m, axis=1, keepdims=True) + inter * jnp.sum(q * n_s[...], axis=1, keepdims=True)
    h = num / jnp.maximum(jnp.abs(den), jnp.exp(-m_t))

    log_w = g_tot - b_col + i_col
    m_new = jnp.maximum(g_tot + m_prev, jnp.max(log_w, axis=0, keepdims=True))
    w_col = jnp.exp(log_w - m_new)
    decay = jnp.exp(g_tot + m_prev - m_new)
    kw = k * (w_col * inv_sqrt_d)
    c_s[...] = decay * c_s[...] + lax.dot_general(kw.astype(_BF16), vb, _TN, preferred_element_type=_F32)
    n_s[...] = decay * n_s[...] + jnp.sum(kw, axis=0, keepdims=True)
    m_s[...] = m_new

    mix_ref[:, :dh] = _rms(_sigmoid(o_ref[...]) * h, gm_ref[...]).astype(mix_ref.dtype)

    half = dh // 2
    cos = cos_ref[...]
    sin = sin_ref[...]

    def rope(x):
        x1 = x[:, :half]
        x2 = x[:, half:]
        return jnp.concatenate([x1 * cos - x2 * sin, x2 * cos + x1 * sin], axis=1)

    rq = rope(rq_ref[...]).astype(_BF16)
    rk = rope(rk_ref[...])
    rvb = rv_ref[...].astype(_BF16)
    lg = lg_ref[...]
    dist = (row - col).astype(_F32)
    d_r = jnp.where(causal, jnp.exp(lg * jnp.maximum(dist, 0.0)), 0.0)
    t_col = lax.broadcasted_iota(jnp.int32, (L, 1), 0).astype(_F32)
    q_decay = jnp.exp(lg * (t_col + 1.0))
    k_decay = jnp.exp(lg * (L - 1.0 - t_col))
    chunk_decay = jnp.exp(lg * L)
    s_r = lax.dot_general(rq, rk.astype(_BF16), _NT, preferred_element_type=_F32) * inv_sqrt_d * d_r
    out_r = (jnp.dot(s_r.astype(_BF16), rvb, preferred_element_type=_F32)
             + jnp.dot(rq, r_s[...].astype(_BF16), preferred_element_type=_F32) * q_decay)
    r_s[...] = chunk_decay * r_s[...] + lax.dot_general(
        (rk * (k_decay * inv_sqrt_d)).astype(_BF16), rvb, _TN, preferred_element_type=_F32)
    rg = rg_ref[...]
    mix_ref[:, dh:] = (rg * _sigmoid(rg) * _rms(out_r, gr_ref[...])).astype(mix_ref.dtype)


def recurrent_mixer(proj_m, proj_r, gates, cos, sin, conv_w, g_mlstm_out, g_ret_out, log_gamma):
    B, S, _ = proj_m.shape
    H, dh, L = MLSTM_HEADS, HEAD_DIM, REC_CHUNK
    g4 = gates.reshape(B, S, 2, H)
    gt = g4.transpose(0, 3, 2, 1)
    gc = g4.transpose(0, 3, 1, 2)

    def pspec(group):
        return pl.BlockSpec((None, L, dh), lambda b, h, c: (b, c, group * H + h))

    in_specs = [pspec(g) for g in range(4)] * 2 + [
        pl.BlockSpec((None, None, 2, L), lambda b, h, c: (b, h, 0, c)),
        pl.BlockSpec((None, None, L, 2), lambda b, h, c: (b, h, c, 0)),
        pl.BlockSpec((None, L, dh // 2), lambda b, h, c: (b, c, 0)),
        pl.BlockSpec((None, L, dh // 2), lambda b, h, c: (b, c, 0)),
        pl.BlockSpec((CONV_WIDTH, dh), lambda b, h, c: (0, h)),
        pl.BlockSpec((CONV_WIDTH, dh), lambda b, h, c: (0, H + h)),
        pl.BlockSpec((1, dh), lambda b, h, c: (0, h)),
        pl.BlockSpec((1, dh), lambda b, h, c: (0, h)),
        pl.BlockSpec((None, 1, 1), lambda b, h, c: (h, 0, 0)),
    ]
    return pl.pallas_call(
        _recurrent_kernel,
        out_shape=jax.ShapeDtypeStruct((B, S, 2 * H * dh), _BF16),
        grid=(B, H, S // L),
        in_specs=in_specs,
        out_specs=pl.BlockSpec((None, L, 2 * dh), lambda b, h, c: (b, c, h)),
        scratch_shapes=[pltpu.VMEM((dh, dh), _F32), pltpu.VMEM((1, dh), _F32), pltpu.VMEM((1, 1), _F32),
                        pltpu.VMEM((dh, dh), _F32),
                        pltpu.VMEM((L + SUBLANES, dh), _F32), pltpu.VMEM((L + SUBLANES, dh), _F32)],
        compiler_params=_params("parallel", "parallel", "arbitrary"),
        name="recurrent_mixer",
    )(*([proj_m] * 4), *([proj_r] * 4), gt, gc, cos, sin, conv_w, conv_w,
      g_mlstm_out.reshape(1, H * dh), g_ret_out.reshape(1, H * dh), log_gamma.reshape(H, 1, 1))


def _rope_pair(z2, cs, gvec):
    lane = lax.broadcasted_iota(jnp.int32, z2.shape, 1)
    first = lane < QK_ROPE
    ms = jnp.sum(jnp.where(first, z2 * z2, 0.0), axis=1, keepdims=True) * (1.0 / QK_ROPE)
    t = z2 * lax.rsqrt(ms + EPS) * (cs * gvec)
    return jnp.where(first, t + pltpu.roll(t, QK_ROPE, 1), 0.0)


def _mla_q_kernel(c_ref, gcq_ref, w_ref, gqn_ref, gqr_ref, cs_ref, q_ref, cn_s, *, scale):
    @pl.when(pl.program_id(1) == 0)
    def _():
        cn_s[...] = _rms(c_ref[...], gcq_ref[...]).astype(_BF16)

    dq = q_ref.shape[-1]
    z = jnp.dot(cn_s[...], w_ref[...].astype(_BF16), preferred_element_type=_F32)
    cs = cs_ref[...]
    for h in range(q_ref.shape[0]):
        zh = z[:, h * dq:(h + 1) * dq]
        qn = _rms(zh[:, :QK_NOPE], gqn_ref[...])
        qr = _rope_pair(zh[:, QK_NOPE:], cs, gqr_ref[...])
        q_ref[h] = (jnp.concatenate([qn, qr], axis=1) * scale).astype(q_ref.dtype)


def _mla_kv_kernel(c_ref, kr_ref, gckv_ref, w_ref, gkn_ref, gkr_ref, cs_ref, k_ref, v_ref, cn_s):
    @pl.when(pl.program_id(1) == 0)
    def _():
        cn_s[...] = _rms(c_ref[...], gckv_ref[...]).astype(_BF16)

    dk = k_ref.shape[-1]
    z = jnp.dot(cn_s[...], w_ref[...].astype(_BF16), preferred_element_type=_F32)
    kr = _rope_pair(kr_ref[...], cs_ref[...], gkr_ref[...])
    ones = jnp.ones((z.shape[0], LANES), _F32)
    for h in range(k_ref.shape[0]):
        zh = z[:, h * dk:(h + 1) * dk]
        kn = _rms(zh[:, :QK_NOPE], gkn_ref[...])
        k_ref[h] = jnp.concatenate([kn, kr], axis=1).astype(k_ref.dtype)
        v_ref[h] = jnp.concatenate([zh[:, QK_NOPE:], ones], axis=1).astype(v_ref.dtype)


def _mla_attn_kernel(q_ref, k_ref, v_ref, o_ref, m_s, acc_s):
    qi = pl.program_id(2)
    hb, tq, _ = q_ref.shape
    dv = v_ref.shape[-1]
    for h in range(hb):
        m_s[h] = jnp.full((tq, LANES), -jnp.inf, _F32)
        acc_s[h] = jnp.zeros((tq, dv), _F32)

    def block(h, start, diagonal):
        kb = k_ref[h, pl.ds(start, tq), :]
        vb = v_ref[h, pl.ds(start, tq), :]
        s = lax.dot_general(q_ref[h], kb, _NT, preferred_element_type=_F32)
        if diagonal:
            row = lax.broadcasted_iota(jnp.int32, s.shape, 0)
            col = lax.broadcasted_iota(jnp.int32, s.shape, 1)
            s = jnp.where(col <= row, s, -jnp.inf)
        m_prev = m_s[h]
        m_new = jnp.maximum(m_prev, jnp.max(s, axis=1, keepdims=True))
        p = jnp.exp2(s - jnp.concatenate([m_new] * (tq // LANES), axis=1))
        alpha = jnp.exp2(m_prev - m_new)
        acc_s[h] = jnp.concatenate([alpha] * (dv // LANES), axis=1) * acc_s[h] + jnp.dot(
            p.astype(_BF16), vb, preferred_element_type=_F32)
        m_s[h] = m_new

    def body(j, carry):
        for h in range(hb):
            block(h, pl.multiple_of(j * tq, tq), False)
        return carry

    lax.fori_loop(0, qi, body, 0)
    for h in range(hb):
        block(h, pl.multiple_of(qi * tq, tq), True)
    for h in range(hb):
        acc = acc_s[h]
        o_ref[:, h * V_HEAD:(h + 1) * V_HEAD] = (acc[:, :V_HEAD] / acc[:, V_HEAD:]).astype(o_ref.dtype)


def mla_attention(c, cs, w_q, w_kv, g_cq, g_ckv, g_qn, gv_q, g_kn, gv_k, B, S):
    T = c.shape[0]
    H, tm, hb = MLA_HEADS, ROW_TILE, MLA_HEAD_GROUP
    dq = QK_NOPE + 2 * QK_ROPE
    dv = 2 * V_HEAD
    scale = (QK_NOPE + QK_ROPE) ** -0.5 * LOG2_E
    nt = S // tm
    vec = lambda n: pl.BlockSpec((1, n), lambda i, h: (0, 0))
    head_out = lambda d: pl.BlockSpec((None, hb, tm, d), lambda i, h: (i // nt, h, i % nt, 0))
    q = pl.pallas_call(
        functools.partial(_mla_q_kernel, scale=scale),
        out_shape=jax.ShapeDtypeStruct((B, H, S, dq), _BF16),
        grid=(T // tm, H // hb),
        in_specs=[pl.BlockSpec((tm, Q_LORA), lambda i, h: (i, 0)), vec(Q_LORA),
                  pl.BlockSpec((Q_LORA, hb * dq), lambda i, h: (0, h)),
                  vec(QK_NOPE), vec(LANES),
                  pl.BlockSpec((tm, LANES), lambda i, h: (i, 0))],
        out_specs=head_out(dq),
        scratch_shapes=[pltpu.VMEM((tm, Q_LORA), _BF16)],
        compiler_params=_params("parallel", "arbitrary"),
        name="mla_q_proj",
    )(c, g_cq.reshape(1, -1), w_q, g_qn.reshape(1, -1), gv_q, cs)
    k, v = pl.pallas_call(
        _mla_kv_kernel,
        out_shape=[jax.ShapeDtypeStruct((B, H, S, dq), _BF16), jax.ShapeDtypeStruct((B, H, S, dv), _BF16)],
        grid=(T // tm, H // hb),
        in_specs=[pl.BlockSpec((tm, KV_LORA), lambda i, h: (i, 1)),
                  pl.BlockSpec((tm, LANES), lambda i, h: (i, (Q_LORA + KV_LORA) // LANES)),
                  vec(KV_LORA),
                  pl.BlockSpec((KV_LORA, hb * (QK_NOPE + V_HEAD)), lambda i, h: (0, h)),
                  vec(QK_NOPE), vec(LANES),
                  pl.BlockSpec((tm, LANES), lambda i, h: (i, 0))],
        out_specs=[head_out(dq), head_out(dv)],
        scratch_shapes=[pltpu.VMEM((tm, KV_LORA), _BF16)],
        compiler_params=_params("parallel", "arbitrary"),
        name="mla_kv_proj",
    )(c, c, g_ckv.reshape(1, -1), w_kv, g_kn.reshape(1, -1), gv_k, cs)
    tq = ATTN_TILE
    o = pl.pallas_call(
        _mla_attn_kernel,
        out_shape=jax.ShapeDtypeStruct((B, S, H * V_HEAD), _BF16),
        grid=(B, H // hb, S // tq),
        in_specs=[pl.BlockSpec((None, hb, tq, dq), lambda b, h, i: (b, h, i, 0)),
                  pl.BlockSpec((None, hb, S, dq), lambda b, h, i: (b, h, 0, 0)),
                  pl.BlockSpec((None, hb, S, dv), lambda b, h, i: (b, h, 0, 0))],
        out_specs=pl.BlockSpec((None, tq, hb * V_HEAD), lambda b, h, i: (b, i, h)),
        scratch_shapes=[pltpu.VMEM((hb, tq, LANES), _F32), pltpu.VMEM((hb, tq, dv), _F32)],
        compiler_params=_params("parallel", "parallel", "arbitrary"),
        name="mla_attention",
    )(q, k, v)
    return o.reshape(T, H * V_HEAD)


def _router_kernel(x_ref, g_ref, w_ref, r_ref, hn_ref):
    hn = _rms(x_ref[...], g_ref[...])
    hn_ref[...] = hn.astype(hn_ref.dtype)
    logits = jnp.dot(hn, w_ref[...], precision=lax.Precision.HIGHEST, preferred_element_type=_F32)
    lane = lax.broadcasted_iota(jnp.int32, logits.shape, 1)
    lane_f = lane.astype(_F32)
    neg = -jnp.inf
    lg = jnp.where(lane < N_EXPERTS, logits, neg)
    v1 = jnp.max(lg, axis=1, keepdims=True)
    i1 = jnp.min(jnp.where(lg == v1, lane_f, float(LANES)), axis=1, keepdims=True)
    lg2 = jnp.where(lane_f == i1, neg, lg)
    v2 = jnp.max(lg2, axis=1, keepdims=True)
    i2 = jnp.min(jnp.where(lg2 == v2, lane_f, float(LANES)), axis=1, keepdims=True)
    e2 = jnp.exp(v2 - v1)
    w1 = 1.0 / (1.0 + e2)
    w2 = e2 / (1.0 + e2)
    out = jnp.where(lane < N_EXPERTS, logits, 0.0)
    out = jnp.where(lane == N_EXPERTS, i1, out)
    out = jnp.where(lane == N_EXPERTS + 1, i2, out)
    out = jnp.where(lane == N_EXPERTS + 2, w1, out)
    out = jnp.where(lane == N_EXPERTS + 3, w2, out)
    r_ref[...] = out


def router(x, g, w_router_padded):
    T, d = x.shape
    return pl.pallas_call(
        _router_kernel,
        out_shape=[jax.ShapeDtypeStruct((T, LANES), _F32), jax.ShapeDtypeStruct((T, d), _BF16)],
        grid=(T // ROW_TILE,),
        in_specs=[pl.BlockSpec((ROW_TILE, d), lambda i: (i, 0)),
                  pl.BlockSpec((1, d), lambda i: (0, 0)),
                  pl.BlockSpec((d, LANES), lambda i: (0, 0))],
        out_specs=[pl.BlockSpec((ROW_TILE, LANES), lambda i: (i, 0)), pl.BlockSpec((ROW_TILE, d), lambda i: (i, 0))],
        compiler_params=_params("parallel"),
        name="moe_router",
    )(x, g.reshape(1, d), w_router_padded)


def _row_copy(src_hbm, dst_vmem, sem, src_row, dst_row):
    return pltpu.make_async_copy(src_hbm.at[pl.ds(src_row, 1)], dst_vmem.at[pl.ds(dst_row, 1)], sem)


def _index_spec(n):
    return pl.BlockSpec((None, 1, n), lambda i: (i, 0, 0), memory_space=pltpu.SMEM)


def _gather_kernel(tile_ref, chunk_ref, flag_ref, src_ref, x_ref, o_ref, acc):
    del tile_ref
    s = pl.program_id(0)
    flags = flag_ref[s]
    tr, ch = acc.shape[0], x_ref.shape[0]

    @pl.when((flags & _G_VALID) != 0)
    def _():
        local = src_ref[...] - chunk_ref[s] * ch
        onehot = jnp.where(local == lax.broadcasted_iota(jnp.int32, (tr, ch), 1), 1.0, 0.0).astype(_BF16)
        part = jnp.dot(onehot, x_ref[...], preferred_element_type=_F32)

        @pl.when((flags & _G_FIRST) != 0)
        def _():
            acc[...] = part

        @pl.when((flags & _G_FIRST) == 0)
        def _():
            acc[...] = acc[...] + part

        @pl.when((flags & _G_LAST) != 0)
        def _():
            o_ref[...] = acc[...].astype(o_ref.dtype)


_G_VALID, _G_FIRST, _G_LAST = 1, 2, 4


def gather_rows(x, src_rows, tr):
    T, d = x.shape
    r = src_rows.shape[0]
    n_tiles = r // tr
    ch = GATHER_CHUNK
    n_steps = N_EXPERTS * (T // ch) + n_tiles
    src2 = src_rows.reshape(n_tiles, tr)
    lo = src2[:, 0] // ch
    hi = jnp.max(src2, axis=1) // ch
    count = hi - lo + 1
    end = jnp.cumsum(count)
    steps = jnp.arange(n_steps, dtype=jnp.int32)
    valid = steps < end[-1]
    tile = jnp.minimum(jnp.sum((end[None, :] <= steps[:, None]).astype(jnp.int32), axis=1), n_tiles - 1)
    tile_oh = (tile[:, None] == jnp.arange(n_tiles, dtype=jnp.int32)[None, :]).astype(jnp.int32)
    pick = lambda v: jnp.sum(tile_oh * v[None, :], axis=1)
    start = pick(end - count)
    chunk = jnp.where(valid, pick(lo) + steps - start, hi[-1])
    flags = (valid * _G_VALID + jnp.logical_and(valid, steps == start) * _G_FIRST
             + jnp.logical_and(valid, steps == pick(end) - 1) * _G_LAST)
    return pl.pallas_call(
        _gather_kernel,
        out_shape=jax.ShapeDtypeStruct((r, d), x.dtype),
        grid_spec=pltpu.PrefetchScalarGridSpec(
            num_scalar_prefetch=3,
            grid=(n_steps,),
            in_specs=[pl.BlockSpec((tr, 1), lambda s, tile, chunk, flag: (tile[s], 0)),
                      pl.BlockSpec((ch, d), lambda s, tile, chunk, flag: (chunk[s], 0))],
            out_specs=pl.BlockSpec((tr, d), lambda s, tile, chunk, flag: (tile[s], 0)),
            scratch_shapes=[pltpu.VMEM((tr, d), _F32)],
        ),
        compiler_params=_params("arbitrary"),
        name="moe_gather",
    )(tile.astype(jnp.int32), chunk.astype(jnp.int32), flags.astype(jnp.int32), src_rows.reshape(r, 1), x)


def _combine_kernel(pos_ref, x_ref, w_ref, y_hbm, o_ref, buf_a, buf_b, sem):
    tt = buf_a.shape[0]

    def start(r, carry):
        _row_copy(y_hbm, buf_a, sem, pos_ref[0, TOP_K * r], r).start()
        _row_copy(y_hbm, buf_b, sem, pos_ref[0, TOP_K * r + 1], r).start()
        return carry

    def wait(r, carry):
        _row_copy(y_hbm, buf_a, sem, 0, r).wait()
        _row_copy(y_hbm, buf_b, sem, 0, r).wait()
        return carry

    lax.fori_loop(0, tt, start, 0)
    lax.fori_loop(0, tt, wait, 0)
    w = w_ref[...]
    o_ref[...] = x_ref[...] + w[:, 0:1] * buf_a[...] + w[:, 1:2] * buf_b[...]


def moe_combine(x, y, pos_flat, weights, tt):
    T, d = x.shape
    return pl.pallas_call(
        _combine_kernel,
        out_shape=jax.ShapeDtypeStruct((T, d), _F32),
        grid=(T // tt,),
        in_specs=[_index_spec(tt * TOP_K), pl.BlockSpec((tt, d), lambda i: (i, 0)),
                  pl.BlockSpec((tt, TOP_K), lambda i: (i, 0)), pl.BlockSpec(memory_space=pl.ANY)],
        out_specs=pl.BlockSpec((tt, d), lambda i: (i, 0)),
        scratch_shapes=[pltpu.VMEM((tt, d), _F32), pltpu.VMEM((tt, d), _F32), pltpu.SemaphoreType.DMA],
        compiler_params=_params("arbitrary"),
        name="moe_combine",
    )(pos_flat.reshape(T // tt, 1, tt * TOP_K), x, weights, y)


def _moe_plan(route, tm, n_tiles):
    T = route.shape[0]
    e_flat = route[:, N_EXPERTS:N_EXPERTS + TOP_K].astype(jnp.int32).reshape(-1)
    onehot = (e_flat[:, None] == jnp.arange(N_EXPERTS, dtype=jnp.int32)[None, :]).astype(jnp.int32)
    csum = jnp.cumsum(onehot, axis=0)
    rank = jnp.sum(csum * onehot, axis=1) - 1
    tiles_e = (csum[-1] + tm - 1) // tm
    tile_end = jnp.cumsum(tiles_e)
    tile_start = tile_end - tiles_e
    pos = jnp.sum(onehot * tile_start[None, :], axis=1) * tm + rank
    tile_ids = jnp.arange(n_tiles, dtype=jnp.int32)
    tile_expert = jnp.minimum(jnp.sum((tile_ids[:, None] >= tile_end[None, :]).astype(jnp.int32), axis=1),
                              N_EXPERTS - 1).astype(jnp.int32)
    n_used = tile_end[-1:].astype(jnp.int32)
    rows = n_tiles * tm
    src = jnp.zeros((rows,), jnp.int32).at[pos].set(jnp.arange(T * TOP_K, dtype=jnp.int32) // TOP_K)
    experts = jnp.arange(N_EXPERTS, dtype=jnp.int32)
    has = tiles_e > 0
    later = jnp.where(jnp.logical_and(has[None, :], experts[None, :] > experts[:, None]), experts[None, :], N_EXPERTS)
    next_e = jnp.min(later, axis=1)
    last_e = next_e == N_EXPERTS
    next_e = jnp.where(last_e, jnp.min(jnp.where(has, experts, N_EXPERTS)), next_e)
    group_e = jnp.cumsum(has.astype(jnp.int32)) - 1
    tile_onehot = (tile_expert[:, None] == experts[None, :]).astype(jnp.int32)

    def of_tile(per_expert):
        return jnp.sum(tile_onehot * per_expert.astype(jnp.int32)[None, :], axis=1)

    index = tile_ids - of_tile(tile_start)
    groups = (tile_expert, (index == 0).astype(jnp.int32), index, jnp.maximum(of_tile(tiles_e), 1),
              of_tile(next_e), of_tile(last_e), of_tile(group_e),
              jnp.full((n_tiles,), jnp.sum(has.astype(jnp.int32)), jnp.int32))
    return groups, n_used, src, pos.astype(jnp.int32)


def moe_layer(x, g, w_router, we_gate, we_up, we_down):
    T, d = x.shape
    tm = MOE_ROW_TILE
    n_tiles = (T * TOP_K) // tm + N_EXPERTS
    w_r = jnp.zeros((d, LANES), _F32).at[:, :N_EXPERTS].set(w_router)
    route, hn = router(x, g, w_r)
    plan, n_used, src, pos = _moe_plan(route, tm, n_tiles)
    xs = gather_rows(hn, src, tm)
    h = streamed_matmul(xs, [we_gate, we_up], plan, n_used, tm=tm, tn=UP_COLS, n_chunks=UP_CHUNKS, out_dtype=_BF16)
    y = streamed_matmul(h, [we_down], plan, n_used, tm=tm, tn=DOWN_COLS, n_chunks=DOWN_CHUNKS, out_dtype=_F32)
    weights = route[:, N_EXPERTS + TOP_K:N_EXPERTS + 2 * TOP_K]
    return moe_combine(x, y, pos, weights, tm)


def _rope_tables(positions, half):
    inv = ROPE_BASE ** (-jnp.arange(half, dtype=_F32) / half)
    ang = positions.astype(_F32)[..., None] * inv
    return jnp.cos(ang), jnp.sin(ang)


def _rotate_half_cols(w, width):
    lead = w.shape[:-1]
    w2 = w.reshape(lead + (-1, 2, width // 2))
    return jnp.stack([-w2[..., 1, :], w2[..., 0, :]], axis=-2).reshape(w.shape)


def _swap_halves(g):
    half = g.shape[-1] // 2
    return jnp.concatenate([g[..., half:], g[..., :half]], axis=-1)


def kernel(x, positions, g_mix_norm, g_ffn_norm, w_in, conv_w, b_gates, g_mlstm_out, g_ret_out, w_mix_out,
           ffn_gate, ffn_up, ffn_down, w_dqkv, g_cq, g_ckv, w_uq, w_ukv, g_qn, g_qr, g_kn, g_kr, w_o,
           w_router, we_gate, we_up, we_down):
    B, S, D = x.shape
    T = B * S
    xf = x.reshape(T, D)
    n_gate = 2 * MLSTM_HEADS
    main = 4 * MLSTM_HEADS * HEAD_DIM

    w = w_in[0]
    w_gate = jnp.zeros((D, LANES), _F32).at[:, :n_gate].set(w[:, main:main + n_gate])
    g0 = g_mix_norm[0]
    proj_m = dense_matmul(xf, w_in, tn=1024, out_dtype=_F32, gain=g0, n_out=main)
    proj_r = dense_matmul(xf, w[:, main + n_gate:], tn=1024, out_dtype=_F32, gain=g0)
    gates = dense_matmul(xf, w_gate, tn=LANES, out_dtype=_F32, gain=g0)[:, :n_gate] + b_gates[0][None, :]
    cos_r, sin_r = _rope_tables(positions, HEAD_DIM // 2)
    log_gamma = jnp.log1p(-jnp.exp2(-5.0 - jnp.arange(RET_HEADS, dtype=_F32)))
    mix = recurrent_mixer(proj_m.reshape(B, S, -1), proj_r.reshape(B, S, -1), gates.reshape(B, S, n_gate),
                          cos_r, sin_r, conv_w[0], g_mlstm_out[0], g_ret_out[0], log_gamma).reshape(T, -1)
    w_mix = w_mix_out[0].reshape(2, MLSTM_HEADS, HEAD_DIM, D).transpose(1, 0, 2, 3).reshape(-1, D)
    xf = dense_matmul(mix, w_mix, tn=1024, out_dtype=_F32, residual=xf)

    plan, nu = _dense_stream_plan(T, ROW_TILE)
    hmid = streamed_matmul(xf, [ffn_gate, ffn_up], plan, nu, tm=ROW_TILE, tn=UP_COLS, n_chunks=UP_CHUNKS,
                           out_dtype=_BF16, gain=g_ffn_norm[0])
    xf = streamed_matmul(hmid, [ffn_down], plan, nu, tm=ROW_TILE, tn=DENSE_DOWN_COLS, n_chunks=DOWN_CHUNKS,
                         out_dtype=_F32, residual=xf)

    H = MLA_HEADS
    wd = w_dqkv[0]
    w_kr = wd[:, Q_LORA + KV_LORA:]
    wd_full = jnp.concatenate([wd, _rotate_half_cols(w_kr, QK_ROPE)], axis=1)
    c = dense_matmul(xf, wd_full, tn=wd_full.shape[1] // 3, out_dtype=_F32, gain=g_mix_norm[1])
    wq = w_uq[0].reshape(Q_LORA, H, QK_NOPE + QK_ROPE)
    wq_r = wq[..., QK_NOPE:]
    w_q = jnp.concatenate([wq, _rotate_half_cols(wq_r, QK_ROPE)], axis=-1).reshape(Q_LORA, -1)
    cos_m, sin_m = _rope_tables(positions, QK_ROPE // 2)
    cs = jnp.concatenate([cos_m, cos_m, sin_m, sin_m], axis=-1).reshape(T, LANES)
    gv_q = jnp.concatenate([g_qr[0], _swap_halves(g_qr[0])]).reshape(1, LANES)
    gv_k = jnp.concatenate([g_kr[0], _swap_halves(g_kr[0])]).reshape(1, LANES)
    attn = mla_attention(c, cs, w_q, w_ukv[0], g_cq[0], g_ckv[0], g_qn[0], gv_q, g_kn[0], gv_k, B, S)
    xf = dense_matmul(attn, w_o[0], tn=1024, out_dtype=_F32, residual=xf)

    xf = moe_layer(xf, g_ffn_norm[1], w_router[0], we_gate[0], we_up[0], we_down[0])
    return xf.reshape(B, S, D)
```

```python
import functools

import jax
import jax.numpy as jnp
from jax import lax
from jax.experimental import pallas as pl
from jax.experimental.pallas import tpu as pltpu

EPS = 1e-6
ROPE_BASE = 10000.0
CONV_WIDTH = 4
MLSTM_HEADS = 4
RET_HEADS = 4
HEAD_DIM = 256
REC_CHUNK = 256
MLA_HEADS = 16
Q_LORA = 512
KV_LORA = 512
QK_NOPE = 128
QK_ROPE = 64
V_HEAD = 128
N_EXPERTS = 8
TOP_K = 2

LANES = 128
SUBLANES = 8
VMEM_LIMIT_BYTES = 48 * 1024 * 1024

ROW_TILE = 512
NORM_ROW_TILE = 1024
NORM_VMEM_LIMIT_BYTES = 56 * 1024 * 1024
MOE_ROW_TILE = 256
UP_COLS = 1408
UP_CHUNKS = 8
DOWN_COLS = 1024
DENSE_DOWN_COLS = 512
DOWN_CHUNKS = 11
ATTN_TILE = 512
MLA_HEAD_GROUP = 4
GATHER_CHUNK = 512
LOG2_E = 1.4426950408889634

_F32 = jnp.float32
_BF16 = jnp.bfloat16
_NT = (((1,), (1,)), ((), ()))
_TN = (((0,), (0,)), ((), ()))


def _params(*semantics):
    return pltpu.CompilerParams(dimension_semantics=semantics, vmem_limit_bytes=VMEM_LIMIT_BYTES)


def _sigmoid(x):
    return 1.0 / (1.0 + jnp.exp(-x))


def _rms(x, g):
    return x * lax.rsqrt(jnp.mean(x * x, axis=-1, keepdims=True) + EPS) * g


def _gmm_kernel(te_ref, nu_ref, a_ref, w_ref, *rest):
    del te_ref
    o_ref = rest[-1]

    @pl.when(pl.program_id(1) < nu_ref[0])
    def _():
        acc = jnp.dot(a_ref[...], w_ref[...].astype(_BF16), preferred_element_type=_F32)
        if len(rest) == 2:
            acc = acc + rest[0][...]
        o_ref[...] = acc.astype(o_ref.dtype)

    @pl.when(pl.program_id(1) >= nu_ref[0])
    def _():
        o_ref[...] = jnp.zeros_like(o_ref)


def grouped_matmul(a, w, tile_expert, n_used, *, tm, tn, out_dtype, residual=None):
    m, k = a.shape
    n = w.shape[2]
    n_tiles = m // tm

    def row(j, i, te, nu):
        return jnp.minimum(i, nu[0] - 1)

    in_specs = [pl.BlockSpec((tm, k), lambda j, i, te, nu: (row(j, i, te, nu), 0)),
                pl.BlockSpec((None, k, tn), lambda j, i, te, nu: (te[row(j, i, te, nu)], 0, j))]
    args = [a, w]
    if residual is not None:
        in_specs.append(pl.BlockSpec((tm, tn), lambda j, i, te, nu: (row(j, i, te, nu), j)))
        args.append(residual)
    return pl.pallas_call(
        _gmm_kernel,
        out_shape=jax.ShapeDtypeStruct((m, n), out_dtype),
        grid_spec=pltpu.PrefetchScalarGridSpec(
            num_scalar_prefetch=2,
            grid=(n // tn, n_tiles),
            in_specs=in_specs,
            out_specs=pl.BlockSpec((tm, tn), lambda j, i, te, nu: (i, j)),
        ),
        compiler_params=_params("arbitrary", "arbitrary"),
        name="grouped_matmul",
    )(tile_expert, n_used, *args)


_P_EXPERT, _P_FIRST, _P_LO, _P_HI, _P_NEXT, _P_LAST, _P_GROUP, _P_NGROUPS = range(8)


def _streamed_kernel(plan_ref, nu_ref, a_ref, *refs, n_mats, has_gain, has_res, n_col_tiles):
    w_hbm = refs[:n_mats]
    gain_ref = refs[n_mats] if has_gain else None
    res_ref = refs[n_mats + int(has_gain)] if has_res else None
    o_ref, wbf, stage, sem = refs[n_mats + int(has_gain) + int(has_res):]
    j = pl.program_id(0)
    i = pl.program_id(1)
    _, _, k, tn = wbf.shape
    kc = stage.shape[2]
    n_chunks = k // kc

    def chunk_copy(e, jj, c, m):
        src = w_hbm[m].at[e, pl.ds(pl.multiple_of(c * kc, kc), kc), pl.ds(pl.multiple_of(jj * tn, LANES), tn)]
        return pltpu.make_async_copy(src, stage.at[c & 1, m], sem.at[c & 1, m])

    def prime(e, jj):
        for c in range(2):
            for m in range(n_mats):
                chunk_copy(e, jj, c, m).start()

    def fetch(e, jj, slot, lo, hi):
        def body(c, carry):
            for m in range(n_mats):
                chunk_copy(e, jj, c, m).wait()
                wbf[slot, m, pl.ds(pl.multiple_of(c * kc, kc), kc), :] = stage[c & 1, m].astype(_BF16)

                @pl.when(c + 2 < n_chunks)
                def _():
                    chunk_copy(e, jj, c + 2, m).start()
            return carry

        lax.fori_loop(lo, hi, body, 0)

    expert = plan_ref[_P_EXPERT, i]
    used = i < nu_ref[0]
    cur = (j * plan_ref[_P_NGROUPS, 0] + plan_ref[_P_GROUP, i]) & 1
    in_last_group = plan_ref[_P_LAST, i] == 1
    has_next = jnp.logical_and(used, jnp.logical_not(jnp.logical_and(in_last_group, j == n_col_tiles - 1)))
    next_e = plan_ref[_P_NEXT, i]
    next_j = jnp.where(in_last_group, j + 1, j)

    @pl.when(jnp.logical_and(j == 0, i == 0))
    def _():
        prime(expert, 0)
        fetch(expert, 0, 0, 0, n_chunks)

    @pl.when(jnp.logical_and(has_next, plan_ref[_P_FIRST, i] == 1))
    def _():
        prime(next_e, next_j)

    @pl.when(used)
    def _():
        a = a_ref[...]
        if has_gain:
            a = _rms(a, gain_ref[...]).astype(_BF16)
        if n_mats == 2:
            g = jnp.dot(a, wbf[cur, 0], preferred_element_type=_F32)
            u = jnp.dot(a, wbf[cur, 1], preferred_element_type=_F32)
            out = g * _sigmoid(g) * u
        else:
            out = jnp.dot(a, wbf[cur, 0], preferred_element_type=_F32)
            if has_res:
                out = out + res_ref[...]
        o_ref[...] = out.astype(o_ref.dtype)

    @pl.when(has_next)
    def _():
        fetch(next_e, next_j, 1 - cur, plan_ref[_P_LO, i], plan_ref[_P_HI, i])

    @pl.when(jnp.logical_not(used))
    def _():
        o_ref[...] = jnp.zeros_like(o_ref)


def _chunk_shares(index, size, n_chunks):
    return (index * n_chunks) // size, ((index + 1) * n_chunks) // size


def streamed_matmul(a, weights, groups, n_used, *, tm, tn, n_chunks, out_dtype, gain=None, residual=None):
    expert, first, index, size, nxt, last, group, n_groups = groups
    lo, hi = _chunk_shares(index, size, n_chunks)
    plan = jnp.stack([expert, first, lo, hi, nxt, last, group, n_groups]).astype(jnp.int32)
    m, k = a.shape
    n = weights[0].shape[2]
    n_mats = len(weights)

    def row(j, i, plan, nu):
        return jnp.minimum(i, nu[0] - 1)

    in_specs = [pl.BlockSpec((tm, k), lambda j, i, plan, nu: (row(j, i, plan, nu), 0))]
    in_specs += [pl.BlockSpec(memory_space=pl.ANY)] * n_mats
    args = [a, *weights]
    if gain is not None:
        in_specs.append(pl.BlockSpec((1, k), lambda j, i, plan, nu: (0, 0)))
        args.append(gain.reshape(1, k))
    if residual is not None:
        in_specs.append(pl.BlockSpec((tm, tn), lambda j, i, plan, nu: (row(j, i, plan, nu), j)))
        args.append(residual)
    return pl.pallas_call(
        functools.partial(_streamed_kernel, n_mats=n_mats, has_gain=gain is not None, has_res=residual is not None,
                          n_col_tiles=n // tn),
        out_shape=jax.ShapeDtypeStruct((m, n), out_dtype),
        grid_spec=pltpu.PrefetchScalarGridSpec(
            num_scalar_prefetch=2,
            grid=(n // tn, m // tm),
            in_specs=in_specs,
            out_specs=pl.BlockSpec((tm, tn), lambda j, i, plan, nu: (i, j)),
            scratch_shapes=[pltpu.VMEM((2, n_mats, k, tn), _BF16),
                            pltpu.VMEM((2, n_mats, k // n_chunks, tn), _F32),
                            pltpu.SemaphoreType.DMA((2, n_mats))],
        ),
        compiler_params=_params("arbitrary", "arbitrary"),
        name="streamed_swiglu_up" if n_mats == 2 else "streamed_matmul",
    )(plan, n_used, *args)


def _dense_stream_plan(m, tm):
    n_tiles = m // tm
    ids = jnp.arange(n_tiles, dtype=jnp.int32)
    zero = jnp.zeros_like(ids)
    one = jnp.ones_like(ids)
    groups = (zero, (ids == 0).astype(jnp.int32), ids, one * n_tiles, zero, one, zero, one)
    return groups, jnp.full((1,), n_tiles, jnp.int32)


def _dense_plan(m, tm):
    n_tiles = m // tm
    return jnp.zeros((n_tiles,), jnp.int32), jnp.full((1,), n_tiles, jnp.int32)


def dense_matmul(a, w, *, tn, out_dtype, residual=None):
    te, nu = _dense_plan(a.shape[0], ROW_TILE)
    return grouped_matmul(a, w[None], te, nu, tm=ROW_TILE, tn=tn, out_dtype=out_dtype, residual=residual)


def _norm_matmul_kernel(x_ref, g_ref, w_ref, o_ref, xn_s):
    @pl.when(pl.program_id(1) == 0)
    def _():
        xn_s[...] = _rms(x_ref[...], g_ref[...]).astype(_BF16)

    o_ref[...] = jnp.dot(xn_s[...], w_ref[...].astype(_BF16), preferred_element_type=_F32)


def norm_matmul(x, gain, w, *, tn, n_out=None):
    m, k = x.shape
    n = w.shape[2] if n_out is None else n_out
    tm = NORM_ROW_TILE
    return pl.pallas_call(
        _norm_matmul_kernel,
        out_shape=jax.ShapeDtypeStruct((m, n), _F32),
        grid=(m // tm, n // tn),
        in_specs=[pl.BlockSpec((tm, k), lambda i, j: (i, 0)),
                  pl.BlockSpec((1, k), lambda i, j: (0, 0)),
                  pl.BlockSpec((None, k, tn), lambda i, j: (0, 0, j))],
        out_specs=pl.BlockSpec((tm, tn), lambda i, j: (i, j)),
        scratch_shapes=[pltpu.VMEM((tm, k), _BF16)],
        compiler_params=pltpu.CompilerParams(dimension_semantics=("parallel", "arbitrary"),
                                             vmem_limit_bytes=NORM_VMEM_LIMIT_BYTES),
        name="norm_matmul",
    )(x, gain.reshape(1, k), w)


def _recurrent_kernel(q_ref, k_ref, v_ref, o_ref, rq_ref, rk_ref, rv_ref, rg_ref,
                      gt_ref, gc_ref, cos_ref, sin_ref, cwq_ref, cwk_ref, gm_ref, gr_ref, lg_ref,
                      mix_ref,
                      c_s, n_s, m_s, r_s, qbuf, kbuf):
    L, dh = q_ref.shape
    halo = SUBLANES
    inv_sqrt_d = dh ** -0.5

    @pl.when(pl.program_id(2) == 0)
    def _():
        c_s[...] = jnp.zeros_like(c_s)
        n_s[...] = jnp.zeros_like(n_s)
        m_s[...] = jnp.full_like(m_s, -jnp.inf)
        r_s[...] = jnp.zeros_like(r_s)
        qbuf[0:halo, :] = jnp.zeros((halo, dh), _F32)
        kbuf[0:halo, :] = jnp.zeros((halo, dh), _F32)

    def conv_silu(x_ref, buf, w_ref):
        buf[halo:halo + L, :] = x_ref[...]
        w = w_ref[...]
        y = w[CONV_WIDTH - 1:CONV_WIDTH, :] * buf[halo:halo + L, :]
        for j in range(CONV_WIDTH - 1):
            off = halo - (CONV_WIDTH - 1) + j
            y = y + w[j:j + 1, :] * buf[off:off + L, :]
        buf[0:halo, :] = buf[L:L + halo, :]
        return y * _sigmoid(y)

    row = lax.broadcasted_iota(jnp.int32, (L, L), 0)
    col = lax.broadcasted_iota(jnp.int32, (L, L), 1)
    causal = col <= row

    q = conv_silu(q_ref, qbuf, cwq_ref)
    k = conv_silu(k_ref, kbuf, cwk_ref)
    qb = q.astype(_BF16)
    vb = v_ref[...].astype(_BF16)

    def log_sigmoid(x):
        return jnp.minimum(x, 0.0) - jnp.log1p(jnp.exp(-jnp.abs(x)))

    gt = gt_ref[...]
    gc = gc_ref[...]
    i_row = gt[0:1, :]
    f_row = log_sigmoid(gt[1:2, :])
    i_col = gc[:, 0:1]
    f_col = log_sigmoid(gc[:, 1:2])
    hi = lax.Precision.HIGHEST
    b_col = jnp.dot(causal.astype(_F32), f_col, precision=hi, preferred_element_type=_F32)
    b_row = jnp.dot(f_row, (row <= col).astype(_F32), precision=hi, preferred_element_type=_F32)
    g_tot = b_col[L - 1:L, :]
    m_prev = m_s[...]

    log_d = jnp.where(causal, b_col - b_row + i_row, -jnp.inf)
    m_inter = b_col + m_prev
    m_t = jnp.maximum(jnp.max(log_d, axis=1, keepdims=True), m_inter)
    d_m = jnp.exp(log_d - m_t)
    inter = jnp.exp(m_inter - m_t)
    s = lax.dot_general(qb, k.astype(_BF16), _NT, preferred_element_type=_F32) * inv_sqrt_d
    s_m = s * d_m
    num = (jnp.dot(s_m.astype(_BF16), vb, preferred_element_type=_F32)
           + inter * jnp.dot(qb, c_s[...].astype(_BF16), preferred_element_type=_F32))
    den = jnp.sum(s_m, axis=1, keepdims=True) + inter * jnp.sum(q * n_s[...], axis=1, keepdims=True)
    h = num / jnp.maximum(jnp.abs(den), jnp.exp(-m_t))

    log_w = g_tot - b_col + i_col
    m_new = jnp.maximum(g_tot + m_prev, jnp.max(log_w, axis=0, keepdims=True))
    w_col = jnp.exp(log_w - m_new)
    decay = jnp.exp(g_tot + m_prev - m_new)
    kw = k * (w_col * inv_sqrt_d)
    c_s[...] = decay * c_s[...] + lax.dot_general(kw.astype(_BF16), vb, _TN, preferred_element_type=_F32)
    n_s[...] = decay * n_s[...] + jnp.sum(kw, axis=0, keepdims=True)
    m_s[...] = m_new

    mix_ref[:, :dh] = _rms(_sigmoid(o_ref[...]) * h, gm_ref[...]).astype(mix_ref.dtype)

    half = dh // 2
    cos = cos_ref[...]
    sin = sin_ref[...]

    def rope(x):
        x1 = x[:, :half]
        x2 = x[:, half:]
        return jnp.concatenate([x1 * cos - x2 * sin, x2 * cos + x1 * sin], axis=1)

    rq = rope(rq_ref[...]).astype(_BF16)
    rk = rope(rk_ref[...])
    rvb = rv_ref[...].astype(_BF16)
    lg = lg_ref[...]
    dist = (row - col).astype(_F32)
    d_r = jnp.where(causal, jnp.exp(lg * jnp.maximum(dist, 0.0)), 0.0)
    t_col = lax.broadcasted_iota(jnp.int32, (L, 1), 0).astype(_F32)
    q_decay = jnp.exp(lg * (t_col + 1.0))
    k_decay = jnp.exp(lg * (L - 1.0 - t_col))
    chunk_decay = jnp.exp(lg * L)
    s_r = lax.dot_general(rq, rk.astype(_BF16), _NT, preferred_element_type=_F32) * inv_sqrt_d * d_r
    out_r = (jnp.dot(s_r.astype(_BF16), rvb, preferred_element_type=_F32)
             + jnp.dot(rq, r_s[...].astype(_BF16), preferred_element_type=_F32) * q_decay)
    r_s[...] = chunk_decay * r_s[...] + lax.dot_general(
        (rk * (k_decay * inv_sqrt_d)).astype(_BF16), rvb, _TN, preferred_element_type=_F32)
    rg = rg_ref[...]
    mix_ref[:, dh:] = (rg * _sigmoid(rg) * _rms(out_r, gr_ref[...])).astype(mix_ref.dtype)


def recurrent_mixer(proj_m, proj_r, gates, cos, sin, conv_w, g_mlstm_out, g_ret_out, log_gamma):
    B, S, _ = proj_m.shape
    H, dh, L = MLSTM_HEADS, HEAD_DIM, REC_CHUNK
    g4 = gates.reshape(B, S, 2, H)
    gt = g4.transpose(0, 3, 2, 1)
    gc = g4.transpose(0, 3, 1, 2)

    def pspec(group):
        return pl.BlockSpec((None, L, dh), lambda b, h, c: (b, c, group * H + h))

    in_specs = [pspec(g) for g in range(4)] * 2 + [
        pl.BlockSpec((None, None, 2, L), lambda b, h, c: (b, h, 0, c)),
        pl.BlockSpec((None, None, L, 2), lambda b, h, c: (b, h, c, 0)),
        pl.BlockSpec((None, L, dh // 2), lambda b, h, c: (b, c, 0)),
        pl.BlockSpec((None, L, dh // 2), lambda b, h, c: (b, c, 0)),
        pl.BlockSpec((CONV_WIDTH, dh), lambda b, h, c: (0, h)),
        pl.BlockSpec((CONV_WIDTH, dh), lambda b, h, c: (0, H + h)),
        pl.BlockSpec((1, dh), lambda b, h, c: (0, h)),
        pl.BlockSpec((1, dh), lambda b, h, c: (0, h)),
        pl.BlockSpec((None, 1, 1), lambda b, h, c: (h, 0, 0)),
    ]
    return pl.pallas_call(
        _recurrent_kernel,
        out_shape=jax.ShapeDtypeStruct((B, S, 2 * H * dh), _BF16),
        grid=(B, H, S // L),
        in_specs=in_specs,
        out_specs=pl.BlockSpec((None, L, 2 * dh), lambda b, h, c: (b, c, h)),
        scratch_shapes=[pltpu.VMEM((dh, dh), _F32), pltpu.VMEM((1, dh), _F32), pltpu.VMEM((1, 1), _F32),
                        pltpu.VMEM((dh, dh), _F32),
                        pltpu.VMEM((L + SUBLANES, dh), _F32), pltpu.VMEM((L + SUBLANES, dh), _F32)],
        compiler_params=_params("parallel", "parallel", "arbitrary"),
        name="recurrent_mixer",
    )(*([proj_m] * 4), *([proj_r] * 4), gt, gc, cos, sin, conv_w, conv_w,
      g_mlstm_out.reshape(1, H * dh), g_ret_out.reshape(1, H * dh), log_gamma.reshape(H, 1, 1))


def _rope_pair(z2, cs, gvec):
    lane = lax.broadcasted_iota(jnp.int32, z2.shape, 1)
    first = lane < QK_ROPE
    ms = jnp.sum(jnp.where(first, z2 * z2, 0.0), axis=1, keepdims=True) * (1.0 / QK_ROPE)
    t = z2 * lax.rsqrt(ms + EPS) * (cs * gvec)
    return jnp.where(first, t + pltpu.roll(t, QK_ROPE, 1), 0.0)


def _mla_q_kernel(c_ref, gcq_ref, w_ref, gqn_ref, gqr_ref, cs_ref, q_ref, cn_s, *, scale):
    @pl.when(pl.program_id(1) == 0)
    def _():
        cn_s[...] = _rms(c_ref[...], gcq_ref[...]).astype(_BF16)

    dq = q_ref.shape[-1]
    z = jnp.dot(cn_s[...], w_ref[...].astype(_BF16), preferred_element_type=_F32)
    cs = cs_ref[...]
    for h in range(q_ref.shape[0]):
        zh = z[:, h * dq:(h + 1) * dq]
        qn = _rms(zh[:, :QK_NOPE], gqn_ref[...])
        qr = _rope_pair(zh[:, QK_NOPE:], cs, gqr_ref[...])
        q_ref[h] = (jnp.concatenate([qn, qr], axis=1) * scale).astype(q_ref.dtype)


def _mla_kv_kernel(c_ref, kr_ref, gckv_ref, w_ref, gkn_ref, gkr_ref, cs_ref, k_ref, v_ref, cn_s):
    @pl.when(pl.program_id(1) == 0)
    def _():
        cn_s[...] = _rms(c_ref[...], gckv_ref[...]).astype(_BF16)

    dk = k_ref.shape[-1]
    z = jnp.dot(cn_s[...], w_ref[...].astype(_BF16), preferred_element_type=_F32)
    kr = _rope_pair(kr_ref[...], cs_ref[...], gkr_ref[...])
    ones = jnp.ones((z.shape[0], LANES), _F32)
    for h in range(k_ref.shape[0]):
        zh = z[:, h * dk:(h + 1) * dk]
        kn = _rms(zh[:, :QK_NOPE], gkn_ref[...])
        k_ref[h] = jnp.concatenate([kn, kr], axis=1).astype(k_ref.dtype)
        v_ref[h] = jnp.concatenate([zh[:, QK_NOPE:], ones], axis=1).astype(v_ref.dtype)


def _mla_attn_kernel(q_ref, k_ref, v_ref, o_ref, m_s, acc_s):
    qi = pl.program_id(2)
    hb, tq, _ = q_ref.shape
    dv = v_ref.shape[-1]
    for h in range(hb):
        m_s[h] = jnp.full((tq, LANES), -jnp.inf, _F32)
        acc_s[h] = jnp.zeros((tq, dv), _F32)

    def block(h, start, diagonal):
        kb = k_ref[h, pl.ds(start, tq), :]
        vb = v_ref[h, pl.ds(start, tq), :]
        s = lax.dot_general(q_ref[h], kb, _NT, preferred_element_type=_F32)
        if diagonal:
            row = lax.broadcasted_iota(jnp.int32, s.shape, 0)
            col = lax.broadcasted_iota(jnp.int32, s.shape, 1)
            s = jnp.where(col <= row, s, -jnp.inf)
        m_prev = m_s[h]
        m_new = jnp.maximum(m_prev, jnp.max(s, axis=1, keepdims=True))
        p = jnp.exp2(s - jnp.concatenate([m_new] * (tq // LANES), axis=1))
        alpha = jnp.exp2(m_prev - m_new)
        acc_s[h] = jnp.concatenate([alpha] * (dv // LANES), axis=1) * acc_s[h] + jnp.dot(
            p.astype(_BF16), vb, preferred_element_type=_F32)
        m_s[h] = m_new

    def body(j, carry):
        for h in range(hb):
            block(h, pl.multiple_of(j * tq, tq), False)
        return carry

    lax.fori_loop(0, qi, body, 0)
    for h in range(hb):
        block(h, pl.multiple_of(qi * tq, tq), True)
    for h in range(hb):
        acc = acc_s[h]
        o_ref[:, h * V_HEAD:(h + 1) * V_HEAD] = (acc[:, :V_HEAD] / acc[:, V_HEAD:]).astype(o_ref.dtype)


def mla_attention(c, cs, w_q, w_kv, g_cq, g_ckv, g_qn, gv_q, g_kn, gv_k, B, S):
    T = c.shape[0]
    H, tm, hb = MLA_HEADS, ROW_TILE, MLA_HEAD_GROUP
    dq = QK_NOPE + 2 * QK_ROPE
    dv = 2 * V_HEAD
    scale = (QK_NOPE + QK_ROPE) ** -0.5 * LOG2_E
    nt = S // tm
    vec = lambda n: pl.BlockSpec((1, n), lambda i, h: (0, 0))
    head_out = lambda d: pl.BlockSpec((None, hb, tm, d), lambda i, h: (i // nt, h, i % nt, 0))
    q = pl.pallas_call(
        functools.partial(_mla_q_kernel, scale=scale),
        out_shape=jax.ShapeDtypeStruct((B, H, S, dq), _BF16),
        grid=(T // tm, H // hb),
        in_specs=[pl.BlockSpec((tm, Q_LORA), lambda i, h: (i, 0)), vec(Q_LORA),
                  pl.BlockSpec((Q_LORA, hb * dq), lambda i, h: (0, h)),
                  vec(QK_NOPE), vec(LANES),
                  pl.BlockSpec((tm, LANES), lambda i, h: (i, 0))],
        out_specs=head_out(dq),
        scratch_shapes=[pltpu.VMEM((tm, Q_LORA), _BF16)],
        compiler_params=_params("parallel", "arbitrary"),
        name="mla_q_proj",
    )(c, g_cq.reshape(1, -1), w_q, g_qn.reshape(1, -1), gv_q, cs)
    k, v = pl.pallas_call(
        _mla_kv_kernel,
        out_shape=[jax.ShapeDtypeStruct((B, H, S, dq), _BF16), jax.ShapeDtypeStruct((B, H, S, dv), _BF16)],
        grid=(T // tm, H // hb),
        in_specs=[pl.BlockSpec((tm, KV_LORA), lambda i, h: (i, 1)),
                  pl.BlockSpec((tm, LANES), lambda i, h: (i, (Q_LORA + KV_LORA) // LANES)),
                  vec(KV_LORA),
                  pl.BlockSpec((KV_LORA, hb * (QK_NOPE + V_HEAD)), lambda i, h: (0, h)),
                  vec(QK_NOPE), vec(LANES),
                  pl.BlockSpec((tm, LANES), lambda i, h: (i, 0))],
        out_specs=[head_out(dq), head_out(dv)],
        scratch_shapes=[pltpu.VMEM((tm, KV_LORA), _BF16)],
        compiler_params=_params("parallel", "arbitrary"),
        name="mla_kv_proj",
    )(c, c, g_ckv.reshape(1, -1), w_kv, g_kn.reshape(1, -1), gv_k, cs)
    tq = ATTN_TILE
    o = pl.pallas_call(
        _mla_attn_kernel,
        out_shape=jax.ShapeDtypeStruct((B, S, H * V_HEAD), _BF16),
        grid=(B, H // hb, S // tq),
        in_specs=[pl.BlockSpec((None, hb, tq, dq), lambda b, h, i: (b, h, i, 0)),
                  pl.BlockSpec((None, hb, S, dq), lambda b, h, i: (b, h, 0, 0)),
                  pl.BlockSpec((None, hb, S, dv), lambda b, h, i: (b, h, 0, 0))],
        out_specs=pl.BlockSpec((None, tq, hb * V_HEAD), lambda b, h, i: (b, i, h)),
        scratch_shapes=[pltpu.VMEM((hb, tq, LANES), _F32), pltpu.VMEM((hb, tq, dv), _F32)],
        compiler_params=_params("parallel", "parallel", "arbitrary"),
        name="mla_attention",
    )(q, k, v)
    return o.reshape(T, H * V_HEAD)


def _router_kernel(x_ref, g_ref, w_ref, r_ref, hn_ref):
    hn = _rms(x_ref[...], g_ref[...])
    hn_ref[...] = hn.astype(hn_ref.dtype)
    logits = jnp.dot(hn, w_ref[...], precision=lax.Precision.HIGHEST, preferred_element_type=_F32)
    lane = lax.broadcasted_iota(jnp.int32, logits.shape, 1)
    lane_f = lane.astype(_F32)
    neg = -jnp.inf
    lg = jnp.where(lane < N_EXPERTS, logits, neg)
    v1 = jnp.max(lg, axis=1, keepdims=True)
    i1 = jnp.min(jnp.where(lg == v1, lane_f, float(LANES)), axis=1, keepdims=True)
    lg2 = jnp.where(lane_f == i1, neg, lg)
    v2 = jnp.max(lg2, axis=1, keepdims=True)
    i2 = jnp.min(jnp.where(lg2 == v2, lane_f, float(LANES)), axis=1, keepdims=True)
    e2 = jnp.exp(v2 - v1)
    w1 = 1.0 / (1.0 + e2)
    w2 = e2 / (1.0 + e2)
    out = jnp.where(lane < N_EXPERTS, logits, 0.0)
    out = jnp.where(lane == N_EXPERTS, i1, out)
    out = jnp.where(lane == N_EXPERTS + 1, i2, out)
    out = jnp.where(lane == N_EXPERTS + 2, w1, out)
    out = jnp.where(lane == N_EXPERTS + 3, w2, out)
    r_ref[...] = out


def router(x, g, w_router_padded):
    T, d = x.shape
    return pl.pallas_call(
        _router_kernel,
        out_shape=[jax.ShapeDtypeStruct((T, LANES), _F32), jax.ShapeDtypeStruct((T, d), _BF16)],
        grid=(T // ROW_TILE,),
        in_specs=[pl.BlockSpec((ROW_TILE, d), lambda i: (i, 0)),
                  pl.BlockSpec((1, d), lambda i: (0, 0)),
                  pl.BlockSpec((d, LANES), lambda i: (0, 0))],
        out_specs=[pl.BlockSpec((ROW_TILE, LANES), lambda i: (i, 0)), pl.BlockSpec((ROW_TILE, d), lambda i: (i, 0))],
        compiler_params=_params("parallel"),
        name="moe_router",
    )(x, g.reshape(1, d), w_router_padded)


def _row_copy(src_hbm, dst_vmem, sem, src_row, dst_row):
    return pltpu.make_async_copy(src_hbm.at[pl.ds(src_row, 1)], dst_vmem.at[pl.ds(dst_row, 1)], sem)


def _index_spec(n):
    return pl.BlockSpec((None, 1, n), lambda i: (i, 0, 0), memory_space=pltpu.SMEM)


def _gather_kernel(tile_ref, chunk_ref, flag_ref, pos_ref, x_ref, o_ref, acc):
    del chunk_ref
    s = pl.program_id(0)
    flags = flag_ref[s]
    tr, ch = acc.shape[0], x_ref.shape[0]

    @pl.when((flags & _G_VALID) != 0)
    def _():
        rows = tile_ref[s] * tr + lax.broadcasted_iota(jnp.int32, (tr, ch), 0)
        pos = pos_ref[...]
        hit = jnp.logical_or(rows == pos[0:1, :], rows == pos[1:2, :])
        onehot = jnp.where(hit, 1.0, 0.0).astype(_BF16)
        part = jnp.dot(onehot, x_ref[...], preferred_element_type=_F32)

        @pl.when((flags & _G_FIRST) != 0)
        def _():
            acc[...] = part

        @pl.when((flags & _G_FIRST) == 0)
        def _():
            acc[...] = acc[...] + part

        @pl.when((flags & _G_LAST) != 0)
        def _():
            o_ref[...] = acc[...].astype(o_ref.dtype)


_G_VALID, _G_FIRST, _G_LAST = 1, 2, 4


def gather_rows(x, pos, n_tiles, tr):
    T, d = x.shape
    r = n_tiles * tr
    ch = GATHER_CHUNK
    n_steps = N_EXPERTS * (T // ch) + n_tiles
    tile_ids = jnp.arange(n_tiles, dtype=jnp.int32)
    member = (pos.reshape(-1) // tr)[:, None] == tile_ids[None, :]
    token = (jnp.arange(T * TOP_K, dtype=jnp.int32) // TOP_K)[:, None]
    first_tok = jnp.min(jnp.where(member, token, T), axis=0)
    last_tok = jnp.max(jnp.where(member, token, -1), axis=0)
    lo = jnp.where(last_tok >= 0, first_tok, 0) // ch
    hi = jnp.maximum(last_tok, 0) // ch
    count = hi - lo + 1
    end = jnp.cumsum(count)
    steps = jnp.arange(n_steps, dtype=jnp.int32)
    valid = steps < end[-1]
    tile = jnp.minimum(jnp.sum((end[None, :] <= steps[:, None]).astype(jnp.int32), axis=1), n_tiles - 1)
    tile_oh = (tile[:, None] == jnp.arange(n_tiles, dtype=jnp.int32)[None, :]).astype(jnp.int32)
    pick = lambda v: jnp.sum(tile_oh * v[None, :], axis=1)
    start = pick(end - count)
    chunk = jnp.where(valid, pick(lo) + steps - start, hi[-1])
    flags = (valid * _G_VALID + jnp.logical_and(valid, steps == start) * _G_FIRST
             + jnp.logical_and(valid, steps == pick(end) - 1) * _G_LAST)
    return pl.pallas_call(
        _gather_kernel,
        out_shape=jax.ShapeDtypeStruct((r, d), x.dtype),
        grid_spec=pltpu.PrefetchScalarGridSpec(
            num_scalar_prefetch=3,
            grid=(n_steps,),
            in_specs=[pl.BlockSpec((TOP_K, ch), lambda s, tile, chunk, flag: (0, chunk[s])),
                      pl.BlockSpec((ch, d), lambda s, tile, chunk, flag: (chunk[s], 0))],
            out_specs=pl.BlockSpec((tr, d), lambda s, tile, chunk, flag: (tile[s], 0)),
            scratch_shapes=[pltpu.VMEM((tr, d), _F32)],
        ),
        compiler_params=_params("arbitrary"),
        name="moe_gather",
    )(tile.astype(jnp.int32), chunk.astype(jnp.int32), flags.astype(jnp.int32), pos.T, x)


def _combine_kernel(pos_ref, x_ref, w_ref, y_hbm, o_ref, buf_a, buf_b, sem):
    tt = buf_a.shape[0]

    def start(r, carry):
        _row_copy(y_hbm, buf_a, sem, pos_ref[0, TOP_K * r], r).start()
        _row_copy(y_hbm, buf_b, sem, pos_ref[0, TOP_K * r + 1], r).start()
        return carry

    def wait(r, carry):
        _row_copy(y_hbm, buf_a, sem, 0, r).wait()
        _row_copy(y_hbm, buf_b, sem, 0, r).wait()
        return carry

    lax.fori_loop(0, tt, start, 0)
    lax.fori_loop(0, tt, wait, 0)
    w = w_ref[...]
    o_ref[...] = x_ref[...] + w[:, 0:1] * buf_a[...] + w[:, 1:2] * buf_b[...]


def moe_combine(x, y, pos_flat, weights, tt):
    T, d = x.shape
    return pl.pallas_call(
        _combine_kernel,
        out_shape=jax.ShapeDtypeStruct((T, d), _F32),
        grid=(T // tt,),
        in_specs=[_index_spec(tt * TOP_K), pl.BlockSpec((tt, d), lambda i: (i, 0)),
                  pl.BlockSpec((tt, TOP_K), lambda i: (i, 0)), pl.BlockSpec(memory_space=pl.ANY)],
        out_specs=pl.BlockSpec((tt, d), lambda i: (i, 0)),
        scratch_shapes=[pltpu.VMEM((tt, d), _F32), pltpu.VMEM((tt, d), _F32), pltpu.SemaphoreType.DMA],
        compiler_params=_params("arbitrary"),
        name="moe_combine",
    )(pos_flat.reshape(T // tt, 1, tt * TOP_K), x, weights, y)


def _moe_plan(route, tm, n_tiles):
    T = route.shape[0]
    e_flat = route[:, N_EXPERTS:N_EXPERTS + TOP_K].astype(jnp.int32).reshape(-1)
    onehot = (e_flat[:, None] == jnp.arange(N_EXPERTS, dtype=jnp.int32)[None, :]).astype(jnp.int32)
    csum = jnp.cumsum(onehot, axis=0)
    rank = jnp.sum(csum * onehot, axis=1) - 1
    tiles_e = (csum[-1] + tm - 1) // tm
    tile_end = jnp.cumsum(tiles_e)
    tile_start = tile_end - tiles_e
    pos = jnp.sum(onehot * tile_start[None, :], axis=1) * tm + rank
    tile_ids = jnp.arange(n_tiles, dtype=jnp.int32)
    tile_expert = jnp.minimum(jnp.sum((tile_ids[:, None] >= tile_end[None, :]).astype(jnp.int32), axis=1),
                              N_EXPERTS - 1).astype(jnp.int32)
    n_used = tile_end[-1:].astype(jnp.int32)
    experts = jnp.arange(N_EXPERTS, dtype=jnp.int32)
    has = tiles_e > 0
    later = jnp.where(jnp.logical_and(has[None, :], experts[None, :] > experts[:, None]), experts[None, :], N_EXPERTS)
    next_e = jnp.min(later, axis=1)
    last_e = next_e == N_EXPERTS
    next_e = jnp.where(last_e, jnp.min(jnp.where(has, experts, N_EXPERTS)), next_e)
    group_e = jnp.cumsum(has.astype(jnp.int32)) - 1
    tile_onehot = (tile_expert[:, None] == experts[None, :]).astype(jnp.int32)

    def of_tile(per_expert):
        return jnp.sum(tile_onehot * per_expert.astype(jnp.int32)[None, :], axis=1)

    index = tile_ids - of_tile(tile_start)
    groups = (tile_expert, (index == 0).astype(jnp.int32), index, jnp.maximum(of_tile(tiles_e), 1),
              of_tile(next_e), of_tile(last_e), of_tile(group_e),
              jnp.full((n_tiles,), jnp.sum(has.astype(jnp.int32)), jnp.int32))
    return groups, n_used, pos.astype(jnp.int32).reshape(T, TOP_K)


def moe_layer(x, g, w_router, we_gate, we_up, we_down):
    T, d = x.shape
    tm = MOE_ROW_TILE
    n_tiles = (T * TOP_K) // tm + N_EXPERTS
    w_r = jnp.zeros((d, LANES), _F32).at[:, :N_EXPERTS].set(w_router)
    route, hn = router(x, g, w_r)
    plan, n_used, pos = _moe_plan(route, tm, n_tiles)
    xs = gather_rows(hn, pos, n_tiles, tm)
    h = streamed_matmul(xs, [we_gate, we_up], plan, n_used, tm=tm, tn=UP_COLS, n_chunks=UP_CHUNKS, out_dtype=_BF16)
    y = streamed_matmul(h, [we_down], plan, n_used, tm=tm, tn=DOWN_COLS, n_chunks=DOWN_CHUNKS, out_dtype=_F32)
    weights = route[:, N_EXPERTS + TOP_K:N_EXPERTS + 2 * TOP_K]
    return moe_combine(x, y, pos.reshape(-1), weights, tm)


def _rope_tables(positions, half):
    inv = ROPE_BASE ** (-jnp.arange(half, dtype=_F32) / half)
    ang = positions.astype(_F32)[..., None] * inv
    return jnp.cos(ang), jnp.sin(ang)


def _rotate_half_cols(w, width):
    lead = w.shape[:-1]
    w2 = w.reshape(lead + (-1, 2, width // 2))
    return jnp.stack([-w2[..., 1, :], w2[..., 0, :]], axis=-2).reshape(w.shape)


def _swap_halves(g):
    half = g.shape[-1] // 2
    return jnp.concatenate([g[..., half:], g[..., :half]], axis=-1)


def kernel(x, positions, g_mix_norm, g_ffn_norm, w_in, conv_w, b_gates, g_mlstm_out, g_ret_out, w_mix_out,
           ffn_gate, ffn_up, ffn_down, w_dqkv, g_cq, g_ckv, w_uq, w_ukv, g_qn, g_qr, g_kn, g_kr, w_o,
           w_router, we_gate, we_up, we_down):
    B, S, D = x.shape
    T = B * S
    xf = x.reshape(T, D)
    n_gate = 2 * MLSTM_HEADS
    main = 4 * MLSTM_HEADS * HEAD_DIM

    w = w_in[0]
    w_gate = jnp.zeros((D, LANES), _F32).at[:, :n_gate].set(w[:, main:main + n_gate])
    g0 = g_mix_norm[0]
    proj_m = norm_matmul(xf, g0, w_in, tn=1024, n_out=main)
    proj_r = norm_matmul(xf, g0, w[None, :, main + n_gate:], tn=1024)
    gates = norm_matmul(xf, g0, w_gate[None], tn=LANES)[:, :n_gate] + b_gates[0][None, :]
    cos_r, sin_r = _rope_tables(positions, HEAD_DIM // 2)
    log_gamma = jnp.log1p(-jnp.exp2(-5.0 - jnp.arange(RET_HEADS, dtype=_F32)))
    mix = recurrent_mixer(proj_m.reshape(B, S, -1), proj_r.reshape(B, S, -1), gates.reshape(B, S, n_gate),
                          cos_r, sin_r, conv_w[0], g_mlstm_out[0], g_ret_out[0], log_gamma).reshape(T, -1)
    w_mix = w_mix_out[0].reshape(2, MLSTM_HEADS, HEAD_DIM, D).transpose(1, 0, 2, 3).reshape(-1, D)
    xf = dense_matmul(mix, w_mix, tn=1024, out_dtype=_F32, residual=xf)

    plan, nu = _dense_stream_plan(T, ROW_TILE)
    hmid = streamed_matmul(xf, [ffn_gate, ffn_up], plan, nu, tm=ROW_TILE, tn=UP_COLS, n_chunks=UP_CHUNKS,
                           out_dtype=_BF16, gain=g_ffn_norm[0])
    xf = streamed_matmul(hmid, [ffn_down], plan, nu, tm=ROW_TILE, tn=DENSE_DOWN_COLS, n_chunks=DOWN_CHUNKS,
                         out_dtype=_F32, residual=xf)

    H = MLA_HEADS
    wd = w_dqkv[0]
    w_kr = wd[:, Q_LORA + KV_LORA:]
    wd_full = jnp.concatenate([wd, _rotate_half_cols(w_kr, QK_ROPE)], axis=1)
    c = norm_matmul(xf, g_mix_norm[1], wd_full[None], tn=wd_full.shape[1] // 3)
    wq = w_uq[0].reshape(Q_LORA, H, QK_NOPE + QK_ROPE)
    wq_r = wq[..., QK_NOPE:]
    w_q = jnp.concatenate([wq, _rotate_half_cols(wq_r, QK_ROPE)], axis=-1).reshape(Q_LORA, -1)
    cos_m, sin_m = _rope_tables(positions, QK_ROPE // 2)
    cs = jnp.concatenate([cos_m, cos_m, sin_m, sin_m], axis=-1).reshape(T, LANES)
    gv_q = jnp.concatenate([g_qr[0], _swap_halves(g_qr[0])]).reshape(1, LANES)
    gv_k = jnp.concatenate([g_kr[0], _swap_halves(g_kr[0])]).reshape(1, LANES)
    attn = mla_attention(c, cs, w_q, w_ukv[0], g_cq[0], g_ckv[0], g_qn[0], gv_q, g_kn[0], gv_k, B, S)
    xf = dense_matmul(attn, w_o[0], tn=1024, out_dtype=_F32, residual=xf)

    xf = moe_layer(xf, g_ffn_norm[1], w_router[0], we_gate[0], we_up[0], we_down[0])
    return xf.reshape(B, S, D)
```

```python
import functools

import jax
import jax.numpy as jnp
from jax import lax
from jax.experimental import pallas as pl
from jax.experimental.pallas import tpu as pltpu

EPS = 1e-6
ROPE_BASE = 10000.0
CONV_WIDTH = 4
MLSTM_HEADS = 4
RET_HEADS = 4
HEAD_DIM = 256
REC_CHUNK = 256
MLA_HEADS = 16
Q_LORA = 512
KV_LORA = 512
QK_NOPE = 128
QK_ROPE = 64
V_HEAD = 128
N_EXPERTS = 8
TOP_K = 2

LANES = 128
SUBLANES = 8
VMEM_LIMIT_BYTES = 48 * 1024 * 1024

ROW_TILE = 512
NORM_ROW_TILE = 1024
NORM_VMEM_LIMIT_BYTES = 56 * 1024 * 1024
MOE_ROW_TILE = 256
UP_COLS = 1408
UP_CHUNKS = 8
PROJ_COLS = 2048
DOWN_COLS = 1024
DENSE_DOWN_COLS = 512
DOWN_CHUNKS = 11
ATTN_TILE = 512
MLA_HEAD_GROUP = 4
GATHER_CHUNK = 512
LOG2_E = 1.4426950408889634

_F32 = jnp.float32
_BF16 = jnp.bfloat16
_NT = (((1,), (1,)), ((), ()))
_TN = (((0,), (0,)), ((), ()))


def _params(*semantics):
    return pltpu.CompilerParams(dimension_semantics=semantics, vmem_limit_bytes=VMEM_LIMIT_BYTES)


def _sigmoid(x):
    return 1.0 / (1.0 + jnp.exp(-x))


def _rms(x, g):
    return x * lax.rsqrt(jnp.mean(x * x, axis=-1, keepdims=True) + EPS) * g


def _gmm_kernel(te_ref, nu_ref, a_ref, w_ref, *rest):
    del te_ref
    o_ref = rest[-1]

    @pl.when(pl.program_id(1) < nu_ref[0])
    def _():
        acc = jnp.dot(a_ref[...], w_ref[...].astype(_BF16), preferred_element_type=_F32)
        if len(rest) == 2:
            acc = acc + rest[0][...]
        o_ref[...] = acc.astype(o_ref.dtype)

    @pl.when(pl.program_id(1) >= nu_ref[0])
    def _():
        o_ref[...] = jnp.zeros_like(o_ref)


def grouped_matmul(a, w, tile_expert, n_used, *, tm, tn, out_dtype, residual=None):
    m, k = a.shape
    n = w.shape[2]
    n_tiles = m // tm

    def row(j, i, te, nu):
        return jnp.minimum(i, nu[0] - 1)

    in_specs = [pl.BlockSpec((tm, k), lambda j, i, te, nu: (row(j, i, te, nu), 0)),
                pl.BlockSpec((None, k, tn), lambda j, i, te, nu: (te[row(j, i, te, nu)], 0, j))]
    args = [a, w]
    if residual is not None:
        in_specs.append(pl.BlockSpec((tm, tn), lambda j, i, te, nu: (row(j, i, te, nu), j)))
        args.append(residual)
    return pl.pallas_call(
        _gmm_kernel,
        out_shape=jax.ShapeDtypeStruct((m, n), out_dtype),
        grid_spec=pltpu.PrefetchScalarGridSpec(
            num_scalar_prefetch=2,
            grid=(n // tn, n_tiles),
            in_specs=in_specs,
            out_specs=pl.BlockSpec((tm, tn), lambda j, i, te, nu: (i, j)),
        ),
        compiler_params=_params("arbitrary", "arbitrary"),
        name="grouped_matmul",
    )(tile_expert, n_used, *args)


_P_EXPERT, _P_FIRST, _P_LO, _P_HI, _P_NEXT, _P_LAST, _P_GROUP, _P_NGROUPS = range(8)


def _streamed_kernel(plan_ref, nu_ref, a_ref, *refs, n_mats, has_gain, has_res, n_col_tiles):
    w_hbm = refs[:n_mats]
    gain_ref = refs[n_mats] if has_gain else None
    res_ref = refs[n_mats + int(has_gain)] if has_res else None
    o_ref, wbf, stage, sem = refs[n_mats + int(has_gain) + int(has_res):]
    j = pl.program_id(0)
    i = pl.program_id(1)
    _, _, k, tn = wbf.shape
    kc = stage.shape[2]
    n_chunks = k // kc

    def chunk_copy(e, jj, c, m):
        src = w_hbm[m].at[e, pl.ds(pl.multiple_of(c * kc, kc), kc), pl.ds(pl.multiple_of(jj * tn, LANES), tn)]
        return pltpu.make_async_copy(src, stage.at[c & 1, m], sem.at[c & 1, m])

    def prime(e, jj):
        for c in range(2):
            for m in range(n_mats):
                chunk_copy(e, jj, c, m).start()

    def fetch(e, jj, slot, lo, hi):
        def body(c, carry):
            for m in range(n_mats):
                chunk_copy(e, jj, c, m).wait()
                wbf[slot, m, pl.ds(pl.multiple_of(c * kc, kc), kc), :] = stage[c & 1, m].astype(_BF16)

                @pl.when(c + 2 < n_chunks)
                def _():
                    chunk_copy(e, jj, c + 2, m).start()
            return carry

        lax.fori_loop(lo, hi, body, 0)

    expert = plan_ref[_P_EXPERT, i]
    used = i < nu_ref[0]
    cur = (j * plan_ref[_P_NGROUPS, 0] + plan_ref[_P_GROUP, i]) & 1
    in_last_group = plan_ref[_P_LAST, i] == 1
    has_next = jnp.logical_and(used, jnp.logical_not(jnp.logical_and(in_last_group, j == n_col_tiles - 1)))
    next_e = plan_ref[_P_NEXT, i]
    next_j = jnp.where(in_last_group, j + 1, j)

    @pl.when(jnp.logical_and(j == 0, i == 0))
    def _():
        prime(expert, 0)
        fetch(expert, 0, 0, 0, n_chunks)

    @pl.when(jnp.logical_and(has_next, plan_ref[_P_FIRST, i] == 1))
    def _():
        prime(next_e, next_j)

    @pl.when(used)
    def _():
        a = a_ref[...]
        if has_gain:
            a = _rms(a, gain_ref[...]).astype(_BF16)
        if n_mats == 2:
            g = jnp.dot(a, wbf[cur, 0], preferred_element_type=_F32)
            u = jnp.dot(a, wbf[cur, 1], preferred_element_type=_F32)
            out = g * _sigmoid(g) * u
        else:
            out = jnp.dot(a, wbf[cur, 0], preferred_element_type=_F32)
            if has_res:
                out = out + res_ref[...]
        o_ref[...] = out.astype(o_ref.dtype)

    @pl.when(has_next)
    def _():
        fetch(next_e, next_j, 1 - cur, plan_ref[_P_LO, i], plan_ref[_P_HI, i])

    @pl.when(jnp.logical_not(used))
    def _():
        o_ref[...] = jnp.zeros_like(o_ref)


def _chunk_shares(index, size, n_chunks):
    return (index * n_chunks) // size, ((index + 1) * n_chunks) // size


def streamed_matmul(a, weights, groups, n_used, *, tm, tn, n_chunks, out_dtype, gain=None, residual=None,
                    n_out=None):
    expert, first, index, size, nxt, last, group, n_groups = groups
    lo, hi = _chunk_shares(index, size, n_chunks)
    plan = jnp.stack([expert, first, lo, hi, nxt, last, group, n_groups]).astype(jnp.int32)
    m, k = a.shape
    n = weights[0].shape[2] if n_out is None else n_out
    n_mats = len(weights)

    def row(j, i, plan, nu):
        return jnp.minimum(i, nu[0] - 1)

    in_specs = [pl.BlockSpec((tm, k), lambda j, i, plan, nu: (row(j, i, plan, nu), 0))]
    in_specs += [pl.BlockSpec(memory_space=pl.ANY)] * n_mats
    args = [a, *weights]
    if gain is not None:
        in_specs.append(pl.BlockSpec((1, k), lambda j, i, plan, nu: (0, 0)))
        args.append(gain.reshape(1, k))
    if residual is not None:
        in_specs.append(pl.BlockSpec((tm, tn), lambda j, i, plan, nu: (row(j, i, plan, nu), j)))
        args.append(residual)
    return pl.pallas_call(
        functools.partial(_streamed_kernel, n_mats=n_mats, has_gain=gain is not None, has_res=residual is not None,
                          n_col_tiles=n // tn),
        out_shape=jax.ShapeDtypeStruct((m, n), out_dtype),
        grid_spec=pltpu.PrefetchScalarGridSpec(
            num_scalar_prefetch=2,
            grid=(n // tn, m // tm),
            in_specs=in_specs,
            out_specs=pl.BlockSpec((tm, tn), lambda j, i, plan, nu: (i, j)),
            scratch_shapes=[pltpu.VMEM((2, n_mats, k, tn), _BF16),
                            pltpu.VMEM((2, n_mats, k // n_chunks, tn), _F32),
                            pltpu.SemaphoreType.DMA((2, n_mats))],
        ),
        compiler_params=_params("arbitrary", "arbitrary"),
        name="streamed_swiglu_up" if n_mats == 2 else "streamed_matmul",
    )(plan, n_used, *args)


def _dense_stream_plan(m, tm):
    n_tiles = m // tm
    ids = jnp.arange(n_tiles, dtype=jnp.int32)
    zero = jnp.zeros_like(ids)
    one = jnp.ones_like(ids)
    groups = (zero, (ids == 0).astype(jnp.int32), ids, one * n_tiles, zero, one, zero, one)
    return groups, jnp.full((1,), n_tiles, jnp.int32)


def _dense_plan(m, tm):
    n_tiles = m // tm
    return jnp.zeros((n_tiles,), jnp.int32), jnp.full((1,), n_tiles, jnp.int32)


def dense_matmul(a, w, *, tn, out_dtype, residual=None):
    te, nu = _dense_plan(a.shape[0], ROW_TILE)
    return grouped_matmul(a, w[None], te, nu, tm=ROW_TILE, tn=tn, out_dtype=out_dtype, residual=residual)


def _norm_matmul_kernel(x_ref, g_ref, w_ref, o_ref, xn_s):
    @pl.when(pl.program_id(1) == 0)
    def _():
        xn_s[...] = _rms(x_ref[...], g_ref[...]).astype(_BF16)

    o_ref[...] = jnp.dot(xn_s[...], w_ref[...].astype(_BF16), preferred_element_type=_F32)


def norm_matmul(x, gain, w, *, tn):
    m, k = x.shape
    n = w.shape[2]
    tm = NORM_ROW_TILE
    return pl.pallas_call(
        _norm_matmul_kernel,
        out_shape=jax.ShapeDtypeStruct((m, n), _F32),
        grid=(m // tm, n // tn),
        in_specs=[pl.BlockSpec((tm, k), lambda i, j: (i, 0)),
                  pl.BlockSpec((1, k), lambda i, j: (0, 0)),
                  pl.BlockSpec((None, k, tn), lambda i, j: (0, 0, j))],
        out_specs=pl.BlockSpec((tm, tn), lambda i, j: (i, j)),
        scratch_shapes=[pltpu.VMEM((tm, k), _BF16)],
        compiler_params=pltpu.CompilerParams(dimension_semantics=("parallel", "arbitrary"),
                                             vmem_limit_bytes=NORM_VMEM_LIMIT_BYTES),
        name="norm_matmul",
    )(x, gain.reshape(1, k), w)


def _recurrent_kernel(q_ref, k_ref, v_ref, o_ref, rq_ref, rk_ref, rv_ref, rg_ref,
                      gt_ref, gc_ref, cos_ref, sin_ref, cwq_ref, cwk_ref, gm_ref, gr_ref, lg_ref,
                      mix_ref,
                      c_s, n_s, m_s, r_s, qbuf, kbuf):
    L, dh = q_ref.shape
    halo = SUBLANES
    inv_sqrt_d = dh ** -0.5

    @pl.when(pl.program_id(2) == 0)
    def _():
        c_s[...] = jnp.zeros_like(c_s)
        n_s[...] = jnp.zeros_like(n_s)
        m_s[...] = jnp.full_like(m_s, -jnp.inf)
        r_s[...] = jnp.zeros_like(r_s)
        qbuf[0:halo, :] = jnp.zeros((halo, dh), _F32)
        kbuf[0:halo, :] = jnp.zeros((halo, dh), _F32)

    def conv_silu(x_ref, buf, w_ref):
        buf[halo:halo + L, :] = x_ref[...]
        w = w_ref[...]
        y = w[CONV_WIDTH - 1:CONV_WIDTH, :] * buf[halo:halo + L, :]
        for j in range(CONV_WIDTH - 1):
            off = halo - (CONV_WIDTH - 1) + j
            y = y + w[j:j + 1, :] * buf[off:off + L, :]
        buf[0:halo, :] = buf[L:L + halo, :]
        return y * _sigmoid(y)

    row = lax.broadcasted_iota(jnp.int32, (L, L), 0)
    col = lax.broadcasted_iota(jnp.int32, (L, L), 1)
    causal = col <= row

    q = conv_silu(q_ref, qbuf, cwq_ref)
    k = conv_silu(k_ref, kbuf, cwk_ref)
    qb = q.astype(_BF16)
    vb = v_ref[...].astype(_BF16)

    def log_sigmoid(x):
        return jnp.minimum(x, 0.0) - jnp.log1p(jnp.exp(-jnp.abs(x)))

    gt = gt_ref[...]
    gc = gc_ref[...]
    i_row = gt[0:1, :]
    f_row = log_sigmoid(gt[1:2, :])
    i_col = gc[:, 0:1]
    f_col = log_sigmoid(gc[:, 1:2])
    hi = lax.Precision.HIGHEST
    b_col = jnp.dot(causal.astype(_F32), f_col, precision=hi, preferred_element_type=_F32)
    b_row = jnp.dot(f_row, (row <= col).astype(_F32), precision=hi, preferred_element_type=_F32)
    g_tot = b_col[L - 1:L, :]
    m_prev = m_s[...]

    log_d = jnp.where(causal, b_col - b_row + i_row, -jnp.inf)
    m_inter = b_col + m_prev
    m_t = jnp.maximum(jnp.max(log_d, axis=1, keepdims=True), m_inter)
    d_m = jnp.exp(log_d - m_t)
    inter = jnp.exp(m_inter - m_t)
    s = lax.dot_general(qb, k.astype(_BF16), _NT, preferred_element_type=_F32) * inv_sqrt_d
    s_m = s * d_m
    num = (jnp.dot(s_m.astype(_BF16), vb, preferred_element_type=_F32)
           + inter * jnp.dot(qb, c_s[...].astype(_BF16), preferred_element_type=_F32))
    den = jnp.sum(s_m, axis=1, keepdims=True) + inter * jnp.sum(q * n_s[...], axis=1, keepdims=True)
    h = num / jnp.maximum(jnp.abs(den), jnp.exp(-m_t))

    log_w = g_tot - b_col + i_col
    m_new = jnp.maximum(g_tot + m_prev, jnp.max(log_w, axis=0, keepdims=True))
    w_col = jnp.exp(log_w - m_new)
    decay = jnp.exp(g_tot + m_prev - m_new)
    kw = k * (w_col * inv_sqrt_d)
    c_s[...] = decay * c_s[...] + lax.dot_general(kw.astype(_BF16), vb, _TN, preferred_element_type=_F32)
    n_s[...] = decay * n_s[...] + jnp.sum(kw, axis=0, keepdims=True)
    m_s[...] = m_new

    mix_ref[:, :dh] = _rms(_sigmoid(o_ref[...]) * h, gm_ref[...]).astype(mix_ref.dtype)

    half = dh // 2
    cos = cos_ref[...]
    sin = sin_ref[...]

    def rope(x):
        x1 = x[:, :half]
        x2 = x[:, half:]
        return jnp.concatenate([x1 * cos - x2 * sin, x2 * cos + x1 * sin], axis=1)

    rq = rope(rq_ref[...]).astype(_BF16)
    rk = rope(rk_ref[...])
    rvb = rv_ref[...].astype(_BF16)
    lg = lg_ref[...]
    dist = (row - col).astype(_F32)
    d_r = jnp.where(causal, jnp.exp(lg * jnp.maximum(dist, 0.0)), 0.0)
    t_col = lax.broadcasted_iota(jnp.int32, (L, 1), 0).astype(_F32)
    q_decay = jnp.exp(lg * (t_col + 1.0))
    k_decay = jnp.exp(lg * (L - 1.0 - t_col))
    chunk_decay = jnp.exp(lg * L)
    s_r = lax.dot_general(rq, rk.astype(_BF16), _NT, preferred_element_type=_F32) * inv_sqrt_d * d_r
    out_r = (jnp.dot(s_r.astype(_BF16), rvb, preferred_element_type=_F32)
             + jnp.dot(rq, r_s[...].astype(_BF16), preferred_element_type=_F32) * q_decay)
    r_s[...] = chunk_decay * r_s[...] + lax.dot_general(
        (rk * (k_decay * inv_sqrt_d)).astype(_BF16), rvb, _TN, preferred_element_type=_F32)
    rg = rg_ref[...]
    mix_ref[:, dh:] = (rg * _sigmoid(rg) * _rms(out_r, gr_ref[...])).astype(mix_ref.dtype)


def recurrent_mixer(proj_m, proj_r, gates, cos, sin, conv_w, g_mlstm_out, g_ret_out, log_gamma):
    B, S, _ = proj_m.shape
    H, dh, L = MLSTM_HEADS, HEAD_DIM, REC_CHUNK
    g4 = gates.reshape(B, S, 2, H)
    gt = g4.transpose(0, 3, 2, 1)
    gc = g4.transpose(0, 3, 1, 2)

    def pspec(group):
        return pl.BlockSpec((None, L, dh), lambda b, h, c: (b, c, group * H + h))

    in_specs = [pspec(g) for g in range(4)] * 2 + [
        pl.BlockSpec((None, None, 2, L), lambda b, h, c: (b, h, 0, c)),
        pl.BlockSpec((None, None, L, 2), lambda b, h, c: (b, h, c, 0)),
        pl.BlockSpec((None, L, dh // 2), lambda b, h, c: (b, c, 0)),
        pl.BlockSpec((None, L, dh // 2), lambda b, h, c: (b, c, 0)),
        pl.BlockSpec((CONV_WIDTH, dh), lambda b, h, c: (0, h)),
        pl.BlockSpec((CONV_WIDTH, dh), lambda b, h, c: (0, H + h)),
        pl.BlockSpec((1, dh), lambda b, h, c: (0, h)),
        pl.BlockSpec((1, dh), lambda b, h, c: (0, h)),
        pl.BlockSpec((None, 1, 1), lambda b, h, c: (h, 0, 0)),
    ]
    return pl.pallas_call(
        _recurrent_kernel,
        out_shape=jax.ShapeDtypeStruct((B, S, 2 * H * dh), _BF16),
        grid=(B, H, S // L),
        in_specs=in_specs,
        out_specs=pl.BlockSpec((None, L, 2 * dh), lambda b, h, c: (b, c, h)),
        scratch_shapes=[pltpu.VMEM((dh, dh), _F32), pltpu.VMEM((1, dh), _F32), pltpu.VMEM((1, 1), _F32),
                        pltpu.VMEM((dh, dh), _F32),
                        pltpu.VMEM((L + SUBLANES, dh), _F32), pltpu.VMEM((L + SUBLANES, dh), _F32)],
        compiler_params=_params("parallel", "parallel", "arbitrary"),
        name="recurrent_mixer",
    )(*([proj_m] * 4), *([proj_r] * 4), gt, gc, cos, sin, conv_w, conv_w,
      g_mlstm_out.reshape(1, H * dh), g_ret_out.reshape(1, H * dh), log_gamma.reshape(H, 1, 1))


def _rope_pair(z2, cs, gvec):
    lane = lax.broadcasted_iota(jnp.int32, z2.shape, 1)
    first = lane < QK_ROPE
    ms = jnp.sum(jnp.where(first, z2 * z2, 0.0), axis=1, keepdims=True) * (1.0 / QK_ROPE)
    t = z2 * lax.rsqrt(ms + EPS) * (cs * gvec)
    return jnp.where(first, t + pltpu.roll(t, QK_ROPE, 1), 0.0)


def _mla_q_kernel(c_ref, gcq_ref, w_ref, gqn_ref, gqr_ref, cs_ref, q_ref, cn_s, *, scale):
    @pl.when(pl.program_id(1) == 0)
    def _():
        cn_s[...] = _rms(c_ref[...], gcq_ref[...]).astype(_BF16)

    dq = q_ref.shape[-1]
    z = jnp.dot(cn_s[...], w_ref[...].astype(_BF16), preferred_element_type=_F32)
    cs = cs_ref[...]
    for h in range(q_ref.shape[0]):
        zh = z[:, h * dq:(h + 1) * dq]
        qn = _rms(zh[:, :QK_NOPE], gqn_ref[...])
        qr = _rope_pair(zh[:, QK_NOPE:], cs, gqr_ref[...])
        q_ref[h] = (jnp.concatenate([qn, qr], axis=1) * scale).astype(q_ref.dtype)


def _mla_kv_kernel(c_ref, kr_ref, gckv_ref, w_ref, gkn_ref, gkr_ref, cs_ref, k_ref, v_ref, cn_s):
    @pl.when(pl.program_id(1) == 0)
    def _():
        cn_s[...] = _rms(c_ref[...], gckv_ref[...]).astype(_BF16)

    dk = k_ref.shape[-1]
    z = jnp.dot(cn_s[...], w_ref[...].astype(_BF16), preferred_element_type=_F32)
    kr = _rope_pair(kr_ref[...], cs_ref[...], gkr_ref[...])
    ones = jnp.ones((z.shape[0], LANES), _F32)
    for h in range(k_ref.shape[0]):
        zh = z[:, h * dk:(h + 1) * dk]
        kn = _rms(zh[:, :QK_NOPE], gkn_ref[...])
        k_ref[h] = jnp.concatenate([kn, kr], axis=1).astype(k_ref.dtype)
        v_ref[h] = jnp.concatenate([zh[:, QK_NOPE:], ones], axis=1).astype(v_ref.dtype)


def _mla_attn_kernel(q_ref, k_ref, v_ref, o_ref, m_s, acc_s):
    qi = pl.program_id(2)
    hb, tq, _ = q_ref.shape
    dv = v_ref.shape[-1]
    for h in range(hb):
        m_s[h] = jnp.full((tq, LANES), -jnp.inf, _F32)
        acc_s[h] = jnp.zeros((tq, dv), _F32)

    def block(h, start, diagonal):
        kb = k_ref[h, pl.ds(start, tq), :]
        vb = v_ref[h, pl.ds(start, tq), :]
        s = lax.dot_general(q_ref[h], kb, _NT, preferred_element_type=_F32)
        if diagonal:
            row = lax.broadcasted_iota(jnp.int32, s.shape, 0)
            col = lax.broadcasted_iota(jnp.int32, s.shape, 1)
            s = jnp.where(col <= row, s, -jnp.inf)
        m_prev = m_s[h]
        m_new = jnp.maximum(m_prev, jnp.max(s, axis=1, keepdims=True))
        p = jnp.exp2(s - jnp.concatenate([m_new] * (tq // LANES), axis=1))
        alpha = jnp.exp2(m_prev - m_new)
        acc_s[h] = jnp.concatenate([alpha] * (dv // LANES), axis=1) * acc_s[h] + jnp.dot(
            p.astype(_BF16), vb, preferred_element_type=_F32)
        m_s[h] = m_new

    def pair(j2, carry):
        for u in range(2):
            for h in range(hb):
                block(h, pl.multiple_of((2 * j2 + u) * tq, tq), False)
        return carry

    def single(j, carry):
        for h in range(hb):
            block(h, pl.multiple_of(j * tq, tq), False)
        return carry

    n_pairs = lax.div(qi, 2)
    lax.fori_loop(0, n_pairs, pair, 0)
    lax.fori_loop(2 * n_pairs, qi, single, 0)
    for h in range(hb):
        block(h, pl.multiple_of(qi * tq, tq), True)
    for h in range(hb):
        acc = acc_s[h]
        o_ref[:, h * V_HEAD:(h + 1) * V_HEAD] = (acc[:, :V_HEAD] / acc[:, V_HEAD:]).astype(o_ref.dtype)


def mla_attention(c, cs, w_q, w_kv, g_cq, g_ckv, g_qn, gv_q, g_kn, gv_k, B, S):
    T = c.shape[0]
    H, tm, hb = MLA_HEADS, ROW_TILE, MLA_HEAD_GROUP
    dq = QK_NOPE + 2 * QK_ROPE
    dv = 2 * V_HEAD
    scale = (QK_NOPE + QK_ROPE) ** -0.5 * LOG2_E
    nt = S // tm
    vec = lambda n: pl.BlockSpec((1, n), lambda i, h: (0, 0))
    head_out = lambda d: pl.BlockSpec((None, hb, tm, d), lambda i, h: (i // nt, h, i % nt, 0))
    q = pl.pallas_call(
        functools.partial(_mla_q_kernel, scale=scale),
        out_shape=jax.ShapeDtypeStruct((B, H, S, dq), _BF16),
        grid=(T // tm, H // hb),
        in_specs=[pl.BlockSpec((tm, Q_LORA), lambda i, h: (i, 0)), vec(Q_LORA),
                  pl.BlockSpec((Q_LORA, hb * dq), lambda i, h: (0, h)),
                  vec(QK_NOPE), vec(LANES),
                  pl.BlockSpec((tm, LANES), lambda i, h: (i, 0))],
        out_specs=head_out(dq),
        scratch_shapes=[pltpu.VMEM((tm, Q_LORA), _BF16)],
        compiler_params=_params("parallel", "arbitrary"),
        name="mla_q_proj",
    )(c, g_cq.reshape(1, -1), w_q, g_qn.reshape(1, -1), gv_q, cs)
    k, v = pl.pallas_call(
        _mla_kv_kernel,
        out_shape=[jax.ShapeDtypeStruct((B, H, S, dq), _BF16), jax.ShapeDtypeStruct((B, H, S, dv), _BF16)],
        grid=(T // tm, H // hb),
        in_specs=[pl.BlockSpec((tm, KV_LORA), lambda i, h: (i, 1)),
                  pl.BlockSpec((tm, LANES), lambda i, h: (i, (Q_LORA + KV_LORA) // LANES)),
                  vec(KV_LORA),
                  pl.BlockSpec((KV_LORA, hb * (QK_NOPE + V_HEAD)), lambda i, h: (0, h)),
                  vec(QK_NOPE), vec(LANES),
                  pl.BlockSpec((tm, LANES), lambda i, h: (i, 0))],
        out_specs=[head_out(dq), head_out(dv)],
        scratch_shapes=[pltpu.VMEM((tm, KV_LORA), _BF16)],
        compiler_params=_params("parallel", "arbitrary"),
        name="mla_kv_proj",
    )(c, c, g_ckv.reshape(1, -1), w_kv, g_kn.reshape(1, -1), gv_k, cs)
    tq = ATTN_TILE
    o = pl.pallas_call(
        _mla_attn_kernel,
        out_shape=jax.ShapeDtypeStruct((B, S, H * V_HEAD), _BF16),
        grid=(B, H // hb, S // tq),
        in_specs=[pl.BlockSpec((None, hb, tq, dq), lambda b, h, i: (b, h, i, 0)),
                  pl.BlockSpec((None, hb, S, dq), lambda b, h, i: (b, h, 0, 0)),
                  pl.BlockSpec((None, hb, S, dv), lambda b, h, i: (b, h, 0, 0))],
        out_specs=pl.BlockSpec((None, tq, hb * V_HEAD), lambda b, h, i: (b, i, h)),
        scratch_shapes=[pltpu.VMEM((hb, tq, LANES), _F32), pltpu.VMEM((hb, tq, dv), _F32)],
        compiler_params=_params("parallel", "parallel", "arbitrary"),
        name="mla_attention",
    )(q, k, v)
    return o.reshape(T, H * V_HEAD)


def _router_kernel(x_ref, g_ref, w_ref, r_ref, hn_ref):
    hn = _rms(x_ref[...], g_ref[...])
    hn_ref[...] = hn.astype(hn_ref.dtype)
    logits = jnp.dot(hn, w_ref[...], precision=lax.Precision.HIGHEST, preferred_element_type=_F32)
    lane = lax.broadcasted_iota(jnp.int32, logits.shape, 1)
    lane_f = lane.astype(_F32)
    neg = -jnp.inf
    lg = jnp.where(lane < N_EXPERTS, logits, neg)
    v1 = jnp.max(lg, axis=1, keepdims=True)
    i1 = jnp.min(jnp.where(lg == v1, lane_f, float(LANES)), axis=1, keepdims=True)
    lg2 = jnp.where(lane_f == i1, neg, lg)
    v2 = jnp.max(lg2, axis=1, keepdims=True)
    i2 = jnp.min(jnp.where(lg2 == v2, lane_f, float(LANES)), axis=1, keepdims=True)
    e2 = jnp.exp(v2 - v1)
    w1 = 1.0 / (1.0 + e2)
    w2 = e2 / (1.0 + e2)
    out = jnp.where(lane < N_EXPERTS, logits, 0.0)
    out = jnp.where(lane == N_EXPERTS, i1, out)
    out = jnp.where(lane == N_EXPERTS + 1, i2, out)
    out = jnp.where(lane == N_EXPERTS + 2, w1, out)
    out = jnp.where(lane == N_EXPERTS + 3, w2, out)
    r_ref[...] = out


def router(x, g, w_router_padded):
    T, d = x.shape
    return pl.pallas_call(
        _router_kernel,
        out_shape=[jax.ShapeDtypeStruct((T, LANES), _F32), jax.ShapeDtypeStruct((T, d), _BF16)],
        grid=(T // ROW_TILE,),
        in_specs=[pl.BlockSpec((ROW_TILE, d), lambda i: (i, 0)),
                  pl.BlockSpec((1, d), lambda i: (0, 0)),
                  pl.BlockSpec((d, LANES), lambda i: (0, 0))],
        out_specs=[pl.BlockSpec((ROW_TILE, LANES), lambda i: (i, 0)), pl.BlockSpec((ROW_TILE, d), lambda i: (i, 0))],
        compiler_params=_params("parallel"),
        name="moe_router",
    )(x, g.reshape(1, d), w_router_padded)


def _row_copy(src_hbm, dst_vmem, sem, src_row, dst_row):
    return pltpu.make_async_copy(src_hbm.at[pl.ds(src_row, 1)], dst_vmem.at[pl.ds(dst_row, 1)], sem)


def _index_spec(n):
    return pl.BlockSpec((None, 1, n), lambda i: (i, 0, 0), memory_space=pltpu.SMEM)


def _gather_kernel(tile_ref, chunk_ref, flag_ref, pos_ref, x_ref, o_ref, acc):
    del chunk_ref
    s = pl.program_id(0)
    flags = flag_ref[s]
    tr, ch = acc.shape[0], x_ref.shape[0]

    @pl.when((flags & _G_VALID) != 0)
    def _():
        rows = tile_ref[s] * tr + lax.broadcasted_iota(jnp.int32, (tr, ch), 0)
        pos = pos_ref[...]
        hit = jnp.logical_or(rows == pos[0:1, :], rows == pos[1:2, :])
        onehot = jnp.where(hit, 1.0, 0.0).astype(_BF16)
        part = jnp.dot(onehot, x_ref[...], preferred_element_type=_F32)

        @pl.when((flags & _G_FIRST) != 0)
        def _():
            acc[...] = part

        @pl.when((flags & _G_FIRST) == 0)
        def _():
            acc[...] = acc[...] + part

        @pl.when((flags & _G_LAST) != 0)
        def _():
            o_ref[...] = acc[...].astype(o_ref.dtype)


_G_VALID, _G_FIRST, _G_LAST = 1, 2, 4


def gather_rows(x, pos, n_tiles, tr):
    T, d = x.shape
    r = n_tiles * tr
    ch = GATHER_CHUNK
    n_steps = N_EXPERTS * (T // ch) + n_tiles
    tile_ids = jnp.arange(n_tiles, dtype=jnp.int32)
    member = (pos.reshape(-1) // tr)[:, None] == tile_ids[None, :]
    token = (jnp.arange(T * TOP_K, dtype=jnp.int32) // TOP_K)[:, None]
    first_tok = jnp.min(jnp.where(member, token, T), axis=0)
    last_tok = jnp.max(jnp.where(member, token, -1), axis=0)
    lo = jnp.where(last_tok >= 0, first_tok, 0) // ch
    hi = jnp.maximum(last_tok, 0) // ch
    count = hi - lo + 1
    end = jnp.cumsum(count)
    steps = jnp.arange(n_steps, dtype=jnp.int32)
    valid = steps < end[-1]
    tile = jnp.minimum(jnp.sum((end[None, :] <= steps[:, None]).astype(jnp.int32), axis=1), n_tiles - 1)
    tile_oh = (tile[:, None] == jnp.arange(n_tiles, dtype=jnp.int32)[None, :]).astype(jnp.int32)
    pick = lambda v: jnp.sum(tile_oh * v[None, :], axis=1)
    start = pick(end - count)
    chunk = jnp.where(valid, pick(lo) + steps - start, hi[-1])
    flags = (valid * _G_VALID + jnp.logical_and(valid, steps == start) * _G_FIRST
             + jnp.logical_and(valid, steps == pick(end) - 1) * _G_LAST)
    return pl.pallas_call(
        _gather_kernel,
        out_shape=jax.ShapeDtypeStruct((r, d), x.dtype),
        grid_spec=pltpu.PrefetchScalarGridSpec(
            num_scalar_prefetch=3,
            grid=(n_steps,),
            in_specs=[pl.BlockSpec((TOP_K, ch), lambda s, tile, chunk, flag: (0, chunk[s])),
                      pl.BlockSpec((ch, d), lambda s, tile, chunk, flag: (chunk[s], 0))],
            out_specs=pl.BlockSpec((tr, d), lambda s, tile, chunk, flag: (tile[s], 0)),
            scratch_shapes=[pltpu.VMEM((tr, d), _F32)],
        ),
        compiler_params=_params("arbitrary"),
        name="moe_gather",
    )(tile.astype(jnp.int32), chunk.astype(jnp.int32), flags.astype(jnp.int32), pos.T, x)


def _combine_kernel(pos_ref, x_ref, w_ref, y_hbm, o_ref, buf_a, buf_b, sem):
    tt = buf_a.shape[0]

    def start(r, carry):
        _row_copy(y_hbm, buf_a, sem, pos_ref[0, TOP_K * r], r).start()
        _row_copy(y_hbm, buf_b, sem, pos_ref[0, TOP_K * r + 1], r).start()
        return carry

    def wait(r, carry):
        _row_copy(y_hbm, buf_a, sem, 0, r).wait()
        _row_copy(y_hbm, buf_b, sem, 0, r).wait()
        return carry

    lax.fori_loop(0, tt, start, 0)
    lax.fori_loop(0, tt, wait, 0)
    w = w_ref[...]
    o_ref[...] = x_ref[...] + w[:, 0:1] * buf_a[...] + w[:, 1:2] * buf_b[...]


def moe_combine(x, y, pos_flat, weights, tt):
    T, d = x.shape
    return pl.pallas_call(
        _combine_kernel,
        out_shape=jax.ShapeDtypeStruct((T, d), _F32),
        grid=(T // tt,),
        in_specs=[_index_spec(tt * TOP_K), pl.BlockSpec((tt, d), lambda i: (i, 0)),
                  pl.BlockSpec((tt, TOP_K), lambda i: (i, 0)), pl.BlockSpec(memory_space=pl.ANY)],
        out_specs=pl.BlockSpec((tt, d), lambda i: (i, 0)),
        scratch_shapes=[pltpu.VMEM((tt, d), _F32), pltpu.VMEM((tt, d), _F32), pltpu.SemaphoreType.DMA],
        compiler_params=_params("arbitrary"),
        name="moe_combine",
    )(pos_flat.reshape(T // tt, 1, tt * TOP_K), x, weights, y)


def _moe_plan(route, tm, n_tiles):
    T = route.shape[0]
    e_flat = route[:, N_EXPERTS:N_EXPERTS + TOP_K].astype(jnp.int32).reshape(-1)
    onehot = (e_flat[:, None] == jnp.arange(N_EXPERTS, dtype=jnp.int32)[None, :]).astype(jnp.int32)
    csum = jnp.cumsum(onehot, axis=0)
    rank = jnp.sum(csum * onehot, axis=1) - 1
    tiles_e = (csum[-1] + tm - 1) // tm
    tile_end = jnp.cumsum(tiles_e)
    tile_start = tile_end - tiles_e
    pos = jnp.sum(onehot * tile_start[None, :], axis=1) * tm + rank
    tile_ids = jnp.arange(n_tiles, dtype=jnp.int32)
    tile_expert = jnp.minimum(jnp.sum((tile_ids[:, None] >= tile_end[None, :]).astype(jnp.int32), axis=1),
                              N_EXPERTS - 1).astype(jnp.int32)
    n_used = tile_end[-1:].astype(jnp.int32)
    experts = jnp.arange(N_EXPERTS, dtype=jnp.int32)
    has = tiles_e > 0
    later = jnp.where(jnp.logical_and(has[None, :], experts[None, :] > experts[:, None]), experts[None, :], N_EXPERTS)
    next_e = jnp.min(later, axis=1)
    last_e = next_e == N_EXPERTS
    next_e = jnp.where(last_e, jnp.min(jnp.where(has, experts, N_EXPERTS)), next_e)
    group_e = jnp.cumsum(has.astype(jnp.int32)) - 1
    tile_onehot = (tile_expert[:, None] == experts[None, :]).astype(jnp.int32)

    def of_tile(per_expert):
        return jnp.sum(tile_onehot * per_expert.astype(jnp.int32)[None, :], axis=1)

    index = tile_ids - of_tile(tile_start)
    groups = (tile_expert, (index == 0).astype(jnp.int32), index, jnp.maximum(of_tile(tiles_e), 1),
              of_tile(next_e), of_tile(last_e), of_tile(group_e),
              jnp.full((n_tiles,), jnp.sum(has.astype(jnp.int32)), jnp.int32))
    return groups, n_used, pos.astype(jnp.int32).reshape(T, TOP_K)


def moe_layer(x, g, w_router, we_gate, we_up, we_down):
    T, d = x.shape
    tm = MOE_ROW_TILE
    n_tiles = (T * TOP_K) // tm + N_EXPERTS
    w_r = jnp.zeros((d, LANES), _F32).at[:, :N_EXPERTS].set(w_router)
    route, hn = router(x, g, w_r)
    plan, n_used, pos = _moe_plan(route, tm, n_tiles)
    xs = gather_rows(hn, pos, n_tiles, tm)
    h = streamed_matmul(xs, [we_gate, we_up], plan, n_used, tm=tm, tn=UP_COLS, n_chunks=UP_CHUNKS, out_dtype=_BF16)
    y = streamed_matmul(h, [we_down], plan, n_used, tm=tm, tn=DOWN_COLS, n_chunks=DOWN_CHUNKS, out_dtype=_F32)
    weights = route[:, N_EXPERTS + TOP_K:N_EXPERTS + 2 * TOP_K]
    return moe_combine(x, y, pos.reshape(-1), weights, tm)


def _rope_tables(positions, half):
    inv = ROPE_BASE ** (-jnp.arange(half, dtype=_F32) / half)
    ang = positions.astype(_F32)[..., None] * inv
    return jnp.cos(ang), jnp.sin(ang)


def _rotate_half_cols(w, width):
    lead = w.shape[:-1]
    w2 = w.reshape(lead + (-1, 2, width // 2))
    return jnp.stack([-w2[..., 1, :], w2[..., 0, :]], axis=-2).reshape(w.shape)


def _swap_halves(g):
    half = g.shape[-1] // 2
    return jnp.concatenate([g[..., half:], g[..., :half]], axis=-1)


def kernel(x, positions, g_mix_norm, g_ffn_norm, w_in, conv_w, b_gates, g_mlstm_out, g_ret_out, w_mix_out,
           ffn_gate, ffn_up, ffn_down, w_dqkv, g_cq, g_ckv, w_uq, w_ukv, g_qn, g_qr, g_kn, g_kr, w_o,
           w_router, we_gate, we_up, we_down):
    B, S, D = x.shape
    T = B * S
    xf = x.reshape(T, D)
    n_gate = 2 * MLSTM_HEADS
    main = 4 * MLSTM_HEADS * HEAD_DIM

    w = w_in[0]
    w_gate = jnp.zeros((D, LANES), _F32).at[:, :n_gate].set(w[:, main:main + n_gate])
    g0 = g_mix_norm[0]
    plan, nu = _dense_stream_plan(T, ROW_TILE)
    proj_m = streamed_matmul(xf, [w_in], plan, nu, tm=ROW_TILE, tn=PROJ_COLS, n_chunks=UP_CHUNKS, out_dtype=_F32,
                             gain=g0, n_out=main)
    proj_r = streamed_matmul(xf, [w[None, :, main + n_gate:]], plan, nu, tm=ROW_TILE, tn=PROJ_COLS,
                             n_chunks=UP_CHUNKS, out_dtype=_F32, gain=g0)
    gates = norm_matmul(xf, g0, w_gate[None], tn=LANES)[:, :n_gate] + b_gates[0][None, :]
    cos_r, sin_r = _rope_tables(positions, HEAD_DIM // 2)
    log_gamma = jnp.log1p(-jnp.exp2(-5.0 - jnp.arange(RET_HEADS, dtype=_F32)))
    mix = recurrent_mixer(proj_m.reshape(B, S, -1), proj_r.reshape(B, S, -1), gates.reshape(B, S, n_gate),
                          cos_r, sin_r, conv_w[0], g_mlstm_out[0], g_ret_out[0], log_gamma).reshape(T, -1)
    w_mix = w_mix_out[0].reshape(2, MLSTM_HEADS, HEAD_DIM, D).transpose(1, 0, 2, 3).reshape(-1, D)
    xf = dense_matmul(mix, w_mix, tn=1024, out_dtype=_F32, residual=xf)

    hmid = streamed_matmul(xf, [ffn_gate, ffn_up], plan, nu, tm=ROW_TILE, tn=UP_COLS, n_chunks=UP_CHUNKS,
                           out_dtype=_BF16, gain=g_ffn_norm[0])
    xf = streamed_matmul(hmid, [ffn_down], plan, nu, tm=ROW_TILE, tn=DENSE_DOWN_COLS, n_chunks=DOWN_CHUNKS,
                         out_dtype=_F32, residual=xf)

    H = MLA_HEADS
    wd = w_dqkv[0]
    w_kr = wd[:, Q_LORA + KV_LORA:]
    wd_full = jnp.concatenate([wd, _rotate_half_cols(w_kr, QK_ROPE)], axis=1)
    c = norm_matmul(xf, g_mix_norm[1], wd_full[None], tn=wd_full.shape[1] // 3)
    wq = w_uq[0].reshape(Q_LORA, H, QK_NOPE + QK_ROPE)
    wq_r = wq[..., QK_NOPE:]
    w_q = jnp.concatenate([wq, _rotate_half_cols(wq_r, QK_ROPE)], axis=-1).reshape(Q_LORA, -1)
    cos_m, sin_m = _rope_tables(positions, QK_ROPE // 2)
    cs = jnp.concatenate([cos_m, cos_m, sin_m, sin_m], axis=-1).reshape(T, LANES)
    gv_q = jnp.concatenate([g_qr[0], _swap_halves(g_qr[0])]).reshape(1, LANES)
    gv_k = jnp.concatenate([g_kr[0], _swap_halves(g_kr[0])]).reshape(1, LANES)
    attn = mla_attention(c, cs, w_q, w_ukv[0], g_cq[0], g_ckv[0], g_qn[0], gv_q, g_kn[0], gv_k, B, S)
    xf = dense_matmul(attn, w_o[0], tn=1024, out_dtype=_F32, residual=xf)

    xf = moe_layer(xf, g_ffn_norm[1], w_router[0], we_gate[0], we_up[0], we_down[0])
    return xf.reshape(B, S, D)
```

```python
import functools

import jax
import jax.numpy as jnp
import numpy as np
from jax import lax
from jax.experimental import pallas as pl
from jax.experimental.pallas import tpu as pltpu

EPS = 1e-6
ROPE_BASE = 10000.0
CONV_WIDTH = 4
MLSTM_HEADS = 4
RET_HEADS = 4
HEAD_DIM = 256
REC_CHUNK = 256
MLA_HEADS = 16
Q_LORA = 512
KV_LORA = 512
QK_NOPE = 128
QK_ROPE = 64
V_HEAD = 128
N_EXPERTS = 8
TOP_K = 2

LANES = 128
SUBLANES = 8
VMEM_LIMIT_BYTES = 48 * 1024 * 1024

ROW_TILE = 512
NORM_ROW_TILE = 1024
NORM_VMEM_LIMIT_BYTES = 56 * 1024 * 1024
MOE_ROW_TILE = 256
UP_COLS = 1408
UP_CHUNKS = 8
PROJ_COLS = 2048
DOWN_COLS = 1024
DENSE_DOWN_COLS = 512
DOWN_CHUNKS = 11
ATTN_TILE = 512
MLA_HEAD_GROUP = 4
GATHER_CHUNK = 512
LOG2_E = 1.4426950408889634

_F32 = jnp.float32
_BF16 = jnp.bfloat16
_NT = (((1,), (1,)), ((), ()))
_TN = (((0,), (0,)), ((), ()))


def _params(*semantics):
    return pltpu.CompilerParams(dimension_semantics=semantics, vmem_limit_bytes=VMEM_LIMIT_BYTES)


def _sigmoid(x):
    return 1.0 / (1.0 + jnp.exp(-x))


def _rms(x, g):
    return x * lax.rsqrt(jnp.mean(x * x, axis=-1, keepdims=True) + EPS) * g


def _gmm_kernel(te_ref, nu_ref, a_ref, w_ref, *rest):
    del te_ref
    o_ref = rest[-1]

    @pl.when(pl.program_id(1) < nu_ref[0])
    def _():
        acc = jnp.dot(a_ref[...], w_ref[...].astype(_BF16), preferred_element_type=_F32)
        if len(rest) == 2:
            acc = acc + rest[0][...]
        o_ref[...] = acc.astype(o_ref.dtype)

    @pl.when(pl.program_id(1) >= nu_ref[0])
    def _():
        o_ref[...] = jnp.zeros_like(o_ref)


def grouped_matmul(a, w, tile_expert, n_used, *, tm, tn, out_dtype, residual=None):
    m, k = a.shape
    n = w.shape[2]
    n_tiles = m // tm

    def row(j, i, te, nu):
        return jnp.minimum(i, nu[0] - 1)

    in_specs = [pl.BlockSpec((tm, k), lambda j, i, te, nu: (row(j, i, te, nu), 0)),
                pl.BlockSpec((None, k, tn), lambda j, i, te, nu: (te[row(j, i, te, nu)], 0, j))]
    args = [a, w]
    if residual is not None:
        in_specs.append(pl.BlockSpec((tm, tn), lambda j, i, te, nu: (row(j, i, te, nu), j)))
        args.append(residual)
    return pl.pallas_call(
        _gmm_kernel,
        out_shape=jax.ShapeDtypeStruct((m, n), out_dtype),
        grid_spec=pltpu.PrefetchScalarGridSpec(
            num_scalar_prefetch=2,
            grid=(n // tn, n_tiles),
            in_specs=in_specs,
            out_specs=pl.BlockSpec((tm, tn), lambda j, i, te, nu: (i, j)),
        ),
        compiler_params=_params("arbitrary", "arbitrary"),
        name="grouped_matmul",
    )(tile_expert, n_used, *args)


_P_EXPERT, _P_FIRST, _P_LO, _P_HI, _P_NEXT, _P_LAST, _P_GROUP, _P_NGROUPS = range(8)


def _streamed_kernel(plan_ref, nu_ref, a_ref, *refs, n_mats, has_gain, has_res, n_col_tiles):
    w_hbm = refs[:n_mats]
    gain_ref = refs[n_mats] if has_gain else None
    res_ref = refs[n_mats + int(has_gain)] if has_res else None
    o_ref, wbf, stage, sem = refs[n_mats + int(has_gain) + int(has_res):]
    j = pl.program_id(0)
    i = pl.program_id(1)
    _, _, k, tn = wbf.shape
    kc = stage.shape[2]
    n_chunks = k // kc

    def chunk_copy(e, jj, c, m):
        src = w_hbm[m].at[e, pl.ds(pl.multiple_of(c * kc, kc), kc), pl.ds(pl.multiple_of(jj * tn, LANES), tn)]
        return pltpu.make_async_copy(src, stage.at[c & 1, m], sem.at[c & 1, m])

    def prime(e, jj):
        for c in range(2):
            for m in range(n_mats):
                chunk_copy(e, jj, c, m).start()

    def fetch(e, jj, slot, lo, hi):
        def body(c, carry):
            for m in range(n_mats):
                chunk_copy(e, jj, c, m).wait()
                wbf[slot, m, pl.ds(pl.multiple_of(c * kc, kc), kc), :] = stage[c & 1, m].astype(_BF16)

                @pl.when(c + 2 < n_chunks)
                def _():
                    chunk_copy(e, jj, c + 2, m).start()
            return carry

        lax.fori_loop(lo, hi, body, 0)

    expert = plan_ref[_P_EXPERT, i]
    used = i < nu_ref[0]
    cur = (j * plan_ref[_P_NGROUPS, 0] + plan_ref[_P_GROUP, i]) & 1
    in_last_group = plan_ref[_P_LAST, i] == 1
    has_next = jnp.logical_and(used, jnp.logical_not(jnp.logical_and(in_last_group, j == n_col_tiles - 1)))
    next_e = plan_ref[_P_NEXT, i]
    next_j = jnp.where(in_last_group, j + 1, j)

    @pl.when(jnp.logical_and(j == 0, i == 0))
    def _():
        prime(expert, 0)
        fetch(expert, 0, 0, 0, n_chunks)

    @pl.when(jnp.logical_and(has_next, plan_ref[_P_FIRST, i] == 1))
    def _():
        prime(next_e, next_j)

    @pl.when(used)
    def _():
        a = a_ref[...]
        if has_gain:
            a = _rms(a, gain_ref[...]).astype(_BF16)
        if n_mats == 2:
            g = jnp.dot(a, wbf[cur, 0], preferred_element_type=_F32)
            u = jnp.dot(a, wbf[cur, 1], preferred_element_type=_F32)
            out = g * _sigmoid(g) * u
        else:
            out = jnp.dot(a, wbf[cur, 0], preferred_element_type=_F32)
            if has_res:
                out = out + res_ref[...]
        o_ref[...] = out.astype(o_ref.dtype)

    @pl.when(has_next)
    def _():
        fetch(next_e, next_j, 1 - cur, plan_ref[_P_LO, i], plan_ref[_P_HI, i])

    @pl.when(jnp.logical_not(used))
    def _():
        o_ref[...] = jnp.zeros_like(o_ref)


def _chunk_shares(index, size, n_chunks):
    return (index * n_chunks) // size, ((index + 1) * n_chunks) // size


def streamed_matmul(a, weights, groups, n_used, *, tm, tn, n_chunks, out_dtype, gain=None, residual=None,
                    n_out=None):
    expert, first, index, size, nxt, last, group, n_groups = groups
    lo, hi = _chunk_shares(index, size, n_chunks)
    plan = jnp.stack([expert, first, lo, hi, nxt, last, group, n_groups]).astype(jnp.int32)
    m, k = a.shape
    n = weights[0].shape[2] if n_out is None else n_out
    n_mats = len(weights)

    def row(j, i, plan, nu):
        return jnp.minimum(i, nu[0] - 1)

    in_specs = [pl.BlockSpec((tm, k), lambda j, i, plan, nu: (row(j, i, plan, nu), 0))]
    in_specs += [pl.BlockSpec(memory_space=pl.ANY)] * n_mats
    args = [a, *weights]
    if gain is not None:
        in_specs.append(pl.BlockSpec((1, k), lambda j, i, plan, nu: (0, 0)))
        args.append(gain.reshape(1, k))
    if residual is not None:
        in_specs.append(pl.BlockSpec((tm, tn), lambda j, i, plan, nu: (row(j, i, plan, nu), j)))
        args.append(residual)
    return pl.pallas_call(
        functools.partial(_streamed_kernel, n_mats=n_mats, has_gain=gain is not None, has_res=residual is not None,
                          n_col_tiles=n // tn),
        out_shape=jax.ShapeDtypeStruct((m, n), out_dtype),
        grid_spec=pltpu.PrefetchScalarGridSpec(
            num_scalar_prefetch=2,
            grid=(n // tn, m // tm),
            in_specs=in_specs,
            out_specs=pl.BlockSpec((tm, tn), lambda j, i, plan, nu: (i, j)),
            scratch_shapes=[pltpu.VMEM((2, n_mats, k, tn), _BF16),
                            pltpu.VMEM((2, n_mats, k // n_chunks, tn), _F32),
                            pltpu.SemaphoreType.DMA((2, n_mats))],
        ),
        compiler_params=_params("arbitrary", "arbitrary"),
        name="streamed_swiglu_up" if n_mats == 2 else "streamed_matmul",
    )(plan, n_used, *args)


def _dense_stream_plan(m, tm):
    n_tiles = m // tm
    ids = jnp.arange(n_tiles, dtype=jnp.int32)
    zero = jnp.zeros_like(ids)
    one = jnp.ones_like(ids)
    groups = (zero, (ids == 0).astype(jnp.int32), ids, one * n_tiles, zero, one, zero, one)
    return groups, jnp.full((1,), n_tiles, jnp.int32)


def _dense_plan(m, tm):
    n_tiles = m // tm
    return jnp.zeros((n_tiles,), jnp.int32), jnp.full((1,), n_tiles, jnp.int32)


def dense_matmul(a, w, *, tn, out_dtype, residual=None):
    te, nu = _dense_plan(a.shape[0], ROW_TILE)
    return grouped_matmul(a, w[None], te, nu, tm=ROW_TILE, tn=tn, out_dtype=out_dtype, residual=residual)


def _norm_matmul_kernel(x_ref, g_ref, w_ref, o_ref, xn_s):
    @pl.when(pl.program_id(1) == 0)
    def _():
        xn_s[...] = _rms(x_ref[...], g_ref[...]).astype(_BF16)

    o_ref[...] = jnp.dot(xn_s[...], w_ref[...].astype(_BF16), preferred_element_type=_F32)


def norm_matmul(x, gain, w, *, tn):
    m, k = x.shape
    n = w.shape[2]
    tm = NORM_ROW_TILE
    return pl.pallas_call(
        _norm_matmul_kernel,
        out_shape=jax.ShapeDtypeStruct((m, n), _F32),
        grid=(m // tm, n // tn),
        in_specs=[pl.BlockSpec((tm, k), lambda i, j: (i, 0)),
                  pl.BlockSpec((1, k), lambda i, j: (0, 0)),
                  pl.BlockSpec((None, k, tn), lambda i, j: (0, 0, j))],
        out_specs=pl.BlockSpec((tm, tn), lambda i, j: (i, j)),
        scratch_shapes=[pltpu.VMEM((tm, k), _BF16)],
        compiler_params=pltpu.CompilerParams(dimension_semantics=("parallel", "arbitrary"),
                                             vmem_limit_bytes=NORM_VMEM_LIMIT_BYTES),
        name="norm_matmul",
    )(x, gain.reshape(1, k), w)


def _recurrent_kernel(q_ref, k_ref, v_ref, o_ref, rq_ref, rk_ref, rv_ref, rg_ref,
                      gt_ref, gc_ref, cos_ref, sin_ref, cwq_ref, cwk_ref, gm_ref, gr_ref, lg_ref,
                      mix_ref,
                      c_s, n_s, m_s, r_s, qbuf, kbuf):
    L, dh = q_ref.shape
    halo = SUBLANES
    inv_sqrt_d = dh ** -0.5

    @pl.when(pl.program_id(2) == 0)
    def _():
        c_s[...] = jnp.zeros_like(c_s)
        n_s[...] = jnp.zeros_like(n_s)
        m_s[...] = jnp.full_like(m_s, -jnp.inf)
        r_s[...] = jnp.zeros_like(r_s)
        qbuf[0:halo, :] = jnp.zeros((halo, dh), _F32)
        kbuf[0:halo, :] = jnp.zeros((halo, dh), _F32)

    def conv_silu(x_ref, buf, w_ref):
        buf[halo:halo + L, :] = x_ref[...]
        w = w_ref[...]
        y = w[CONV_WIDTH - 1:CONV_WIDTH, :] * buf[halo:halo + L, :]
        for j in range(CONV_WIDTH - 1):
            off = halo - (CONV_WIDTH - 1) + j
            y = y + w[j:j + 1, :] * buf[off:off + L, :]
        buf[0:halo, :] = buf[L:L + halo, :]
        return y * _sigmoid(y)

    row = lax.broadcasted_iota(jnp.int32, (L, L), 0)
    col = lax.broadcasted_iota(jnp.int32, (L, L), 1)
    causal = col <= row

    q = conv_silu(q_ref, qbuf, cwq_ref)
    k = conv_silu(k_ref, kbuf, cwk_ref)
    qb = q.astype(_BF16)
    vb = v_ref[...].astype(_BF16)

    def log_sigmoid(x):
        return jnp.minimum(x, 0.0) - jnp.log1p(jnp.exp(-jnp.abs(x)))

    gt = gt_ref[...]
    gc = gc_ref[...]
    i_row = gt[0:1, :]
    f_row = log_sigmoid(gt[1:2, :])
    i_col = gc[:, 0:1]
    f_col = log_sigmoid(gc[:, 1:2])
    hi = lax.Precision.HIGHEST
    b_col = jnp.dot(causal.astype(_F32), f_col, precision=hi, preferred_element_type=_F32)
    b_row = jnp.dot(f_row, (row <= col).astype(_F32), precision=hi, preferred_element_type=_F32)
    g_tot = b_col[L - 1:L, :]
    m_prev = m_s[...]

    log_d = jnp.where(causal, b_col - b_row + i_row, -jnp.inf)
    m_inter = b_col + m_prev
    m_t = jnp.maximum(jnp.max(log_d, axis=1, keepdims=True), m_inter)
    d_m = jnp.exp(log_d - m_t)
    inter = jnp.exp(m_inter - m_t)
    s = lax.dot_general(qb, k.astype(_BF16), _NT, preferred_element_type=_F32) * inv_sqrt_d
    s_m = s * d_m
    num = (jnp.dot(s_m.astype(_BF16), vb, preferred_element_type=_F32)
           + inter * jnp.dot(qb, c_s[...].astype(_BF16), preferred_element_type=_F32))
    den = jnp.sum(s_m, axis=1, keepdims=True) + inter * jnp.sum(q * n_s[...], axis=1, keepdims=True)
    h = num / jnp.maximum(jnp.abs(den), jnp.exp(-m_t))

    log_w = g_tot - b_col + i_col
    m_new = jnp.maximum(g_tot + m_prev, jnp.max(log_w, axis=0, keepdims=True))
    w_col = jnp.exp(log_w - m_new)
    decay = jnp.exp(g_tot + m_prev - m_new)
    kw = k * (w_col * inv_sqrt_d)
    c_s[...] = decay * c_s[...] + lax.dot_general(kw.astype(_BF16), vb, _TN, preferred_element_type=_F32)
    n_s[...] = decay * n_s[...] + jnp.sum(kw, axis=0, keepdims=True)
    m_s[...] = m_new

    mix_ref[:, :dh] = _rms(_sigmoid(o_ref[...]) * h, gm_ref[...]).astype(mix_ref.dtype)

    half = dh // 2
    cos = cos_ref[...]
    sin = sin_ref[...]

    def rope(x):
        x1 = x[:, :half]
        x2 = x[:, half:]
        return jnp.concatenate([x1 * cos - x2 * sin, x2 * cos + x1 * sin], axis=1)

    rq = rope(rq_ref[...]).astype(_BF16)
    rk = rope(rk_ref[...])
    rvb = rv_ref[...].astype(_BF16)
    lg = lg_ref[...]
    dist = (row - col).astype(_F32)
    d_r = jnp.where(causal, jnp.exp(lg * jnp.maximum(dist, 0.0)), 0.0)
    t_col = lax.broadcasted_iota(jnp.int32, (L, 1), 0).astype(_F32)
    q_decay = jnp.exp(lg * (t_col + 1.0))
    k_decay = jnp.exp(lg * (L - 1.0 - t_col))
    chunk_decay = jnp.exp(lg * L)
    s_r = lax.dot_general(rq, rk.astype(_BF16), _NT, preferred_element_type=_F32) * inv_sqrt_d * d_r
    out_r = (jnp.dot(s_r.astype(_BF16), rvb, preferred_element_type=_F32)
             + jnp.dot(rq, r_s[...].astype(_BF16), preferred_element_type=_F32) * q_decay)
    r_s[...] = chunk_decay * r_s[...] + lax.dot_general(
        (rk * (k_decay * inv_sqrt_d)).astype(_BF16), rvb, _TN, preferred_element_type=_F32)
    rg = rg_ref[...]
    mix_ref[:, dh:] = (rg * _sigmoid(rg) * _rms(out_r, gr_ref[...])).astype(mix_ref.dtype)


def recurrent_mixer(proj_m, proj_r, gates, cos, sin, conv_w, g_mlstm_out, g_ret_out, log_gamma):
    B, S, _ = proj_m.shape
    H, dh, L = MLSTM_HEADS, HEAD_DIM, REC_CHUNK
    g4 = gates.reshape(B, S, 2, H)
    gt = g4.transpose(0, 3, 2, 1)
    gc = g4.transpose(0, 3, 1, 2)

    def pspec(group):
        return pl.BlockSpec((None, L, dh), lambda b, h, c: (b, c, group * H + h))

    in_specs = [pspec(g) for g in range(4)] * 2 + [
        pl.BlockSpec((None, None, 2, L), lambda b, h, c: (b, h, 0, c)),
        pl.BlockSpec((None, None, L, 2), lambda b, h, c: (b, h, c, 0)),
        pl.BlockSpec((None, L, dh // 2), lambda b, h, c: (b, c, 0)),
        pl.BlockSpec((None, L, dh // 2), lambda b, h, c: (b, c, 0)),
        pl.BlockSpec((CONV_WIDTH, dh), lambda b, h, c: (0, h)),
        pl.BlockSpec((CONV_WIDTH, dh), lambda b, h, c: (0, H + h)),
        pl.BlockSpec((1, dh), lambda b, h, c: (0, h)),
        pl.BlockSpec((1, dh), lambda b, h, c: (0, h)),
        pl.BlockSpec((None, 1, 1), lambda b, h, c: (h, 0, 0)),
    ]
    return pl.pallas_call(
        _recurrent_kernel,
        out_shape=jax.ShapeDtypeStruct((B, S, 2 * H * dh), _BF16),
        grid=(B, H, S // L),
        in_specs=in_specs,
        out_specs=pl.BlockSpec((None, L, 2 * dh), lambda b, h, c: (b, c, h)),
        scratch_shapes=[pltpu.VMEM((dh, dh), _F32), pltpu.VMEM((1, dh), _F32), pltpu.VMEM((1, 1), _F32),
                        pltpu.VMEM((dh, dh), _F32),
                        pltpu.VMEM((L + SUBLANES, dh), _F32), pltpu.VMEM((L + SUBLANES, dh), _F32)],
        compiler_params=_params("parallel", "parallel", "arbitrary"),
        name="recurrent_mixer",
    )(*([proj_m] * 4), *([proj_r] * 4), gt, gc, cos, sin, conv_w, conv_w,
      g_mlstm_out.reshape(1, H * dh), g_ret_out.reshape(1, H * dh), log_gamma.reshape(H, 1, 1))


def _rope_pair(z2, cs, gvec, sumsq=None):
    lane = lax.broadcasted_iota(jnp.int32, z2.shape, 1)
    first = lane < QK_ROPE
    if sumsq is None:
        sumsq = jnp.sum(jnp.where(first, z2 * z2, 0.0), axis=1, keepdims=True)
    t = z2 * lax.rsqrt(sumsq * (1.0 / QK_ROPE) + EPS) * (cs * gvec)
    return jnp.where(first, t + pltpu.roll(t, QK_ROPE, 1), 0.0)


def _group_sumsq(z, ones_ref):
    return jnp.dot((z * z).astype(_BF16), ones_ref[...], preferred_element_type=_F32)


def _group_ones(width, pairs):
    m = np.zeros((width, width), np.float32)
    for (k0, k1), (n0, n1) in pairs:
        m[k0:k1, n0:n1] = 1.0
    return jnp.asarray(m, _BF16)


def _mla_q_kernel(c_ref, gcq_ref, w_ref, ones_ref, gqn_ref, gqr_ref, cs_ref, q_ref, cn_s, *, scale):
    @pl.when(pl.program_id(1) == 0)
    def _():
        cn_s[...] = _rms(c_ref[...], gcq_ref[...]).astype(_BF16)

    dq = q_ref.shape[-1]
    z = jnp.dot(cn_s[...], w_ref[...].astype(_BF16), preferred_element_type=_F32)
    cs = cs_ref[...]
    for h in range(q_ref.shape[0]):
        zh = z[:, h * dq:(h + 1) * dq]
        ssh = _group_sumsq(zh, ones_ref)
        qn = zh[:, :QK_NOPE] * lax.rsqrt(ssh[:, :QK_NOPE] * (1.0 / QK_NOPE) + EPS) * gqn_ref[...]
        qr = _rope_pair(zh[:, QK_NOPE:], cs, gqr_ref[...], ssh[:, QK_NOPE:])
        q_ref[h] = (jnp.concatenate([qn, qr], axis=1) * scale).astype(q_ref.dtype)


def _mla_kv_kernel(c_ref, kr_ref, gckv_ref, w_ref, gkn_ref, gkr_ref, cs_ref, k_ref, v_ref, cn_s):
    @pl.when(pl.program_id(1) == 0)
    def _():
        cn_s[...] = _rms(c_ref[...], gckv_ref[...]).astype(_BF16)

    dk = k_ref.shape[-1]
    z = jnp.dot(cn_s[...], w_ref[...].astype(_BF16), preferred_element_type=_F32)
    kr = _rope_pair(kr_ref[...], cs_ref[...], gkr_ref[...])
    ones = jnp.ones((z.shape[0], LANES), _F32)
    for h in range(k_ref.shape[0]):
        zh = z[:, h * dk:(h + 1) * dk]
        kn = _rms(zh[:, :QK_NOPE], gkn_ref[...])
        k_ref[h] = jnp.concatenate([kn, kr], axis=1).astype(k_ref.dtype)
        v_ref[h] = jnp.concatenate([zh[:, QK_NOPE:], ones], axis=1).astype(v_ref.dtype)


def _mla_attn_kernel(q_ref, k_ref, v_ref, o_ref, m_s, acc_s):
    qi = pl.program_id(2)
    hb, tq, _ = q_ref.shape
    dv = v_ref.shape[-1]
    for h in range(hb):
        m_s[h] = jnp.full((tq, LANES), -jnp.inf, _F32)
        acc_s[h] = jnp.zeros((tq, dv), _F32)

    def block(h, start, diagonal):
        kb = k_ref[h, pl.ds(start, tq), :]
        vb = v_ref[h, pl.ds(start, tq), :]
        s = lax.dot_general(q_ref[h], kb, _NT, preferred_element_type=_F32)
        if diagonal:
            row = lax.broadcasted_iota(jnp.int32, s.shape, 0)
            col = lax.broadcasted_iota(jnp.int32, s.shape, 1)
            s = jnp.where(col <= row, s, -jnp.inf)
        m_prev = m_s[h]
        m_new = jnp.maximum(m_prev, jnp.max(s, axis=1, keepdims=True))
        p = jnp.exp2(s - jnp.concatenate([m_new] * (tq // LANES), axis=1))
        alpha = jnp.exp2(m_prev - m_new)
        acc_s[h] = jnp.concatenate([alpha] * (dv // LANES), axis=1) * acc_s[h] + jnp.dot(
            p.astype(_BF16), vb, preferred_element_type=_F32)
        m_s[h] = m_new

    def pair(j2, carry):
        for u in range(2):
            for h in range(hb):
                block(h, pl.multiple_of((2 * j2 + u) * tq, tq), False)
        return carry

    def single(j, carry):
        for h in range(hb):
            block(h, pl.multiple_of(j * tq, tq), False)
        return carry

    n_pairs = lax.div(qi, 2)
    lax.fori_loop(0, n_pairs, pair, 0)
    lax.fori_loop(2 * n_pairs, qi, single, 0)
    for h in range(hb):
        block(h, pl.multiple_of(qi * tq, tq), True)
    for h in range(hb):
        acc = acc_s[h]
        o_ref[:, h * V_HEAD:(h + 1) * V_HEAD] = (acc[:, :V_HEAD] / acc[:, V_HEAD:]).astype(o_ref.dtype)


def mla_attention(c, cs, w_q, w_kv, g_cq, g_ckv, g_qn, gv_q, g_kn, gv_k, B, S):
    T = c.shape[0]
    H, tm, hb = MLA_HEADS, ROW_TILE, MLA_HEAD_GROUP
    dq = QK_NOPE + 2 * QK_ROPE
    dv = 2 * V_HEAD
    scale = (QK_NOPE + QK_ROPE) ** -0.5 * LOG2_E
    nt = S // tm
    vec = lambda n: pl.BlockSpec((1, n), lambda i, h: (0, 0))
    nope, rope = (0, QK_NOPE), (QK_NOPE, QK_NOPE + QK_ROPE)
    ones_q = _group_ones(dq, [(nope, nope), (rope, (QK_NOPE, dq))])
    head_out = lambda d: pl.BlockSpec((None, hb, tm, d), lambda i, h: (i // nt, h, i % nt, 0))
    q = pl.pallas_call(
        functools.partial(_mla_q_kernel, scale=scale),
        out_shape=jax.ShapeDtypeStruct((B, H, S, dq), _BF16),
        grid=(T // tm, H // hb),
        in_specs=[pl.BlockSpec((tm, Q_LORA), lambda i, h: (i, 0)), vec(Q_LORA),
                  pl.BlockSpec((Q_LORA, hb * dq), lambda i, h: (0, h)), pl.BlockSpec((dq, dq), lambda i, h: (0, 0)),
                  vec(QK_NOPE), vec(LANES),
                  pl.BlockSpec((tm, LANES), lambda i, h: (i, 0))],
        out_specs=head_out(dq),
        scratch_shapes=[pltpu.VMEM((tm, Q_LORA), _BF16)],
        compiler_params=_params("parallel", "arbitrary"),
        name="mla_q_proj",
    )(c, g_cq.reshape(1, -1), w_q, ones_q, g_qn.reshape(1, -1), gv_q, cs)
    k, v = pl.pallas_call(
        _mla_kv_kernel,
        out_shape=[jax.ShapeDtypeStruct((B, H, S, dq), _BF16), jax.ShapeDtypeStruct((B, H, S, dv), _BF16)],
        grid=(T // tm, H // hb),
        in_specs=[pl.BlockSpec((tm, KV_LORA), lambda i, h: (i, 1)),
                  pl.BlockSpec((tm, LANES), lambda i, h: (i, (Q_LORA + KV_LORA) // LANES)),
                  vec(KV_LORA),
                  pl.BlockSpec((KV_LORA, hb * (QK_NOPE + V_HEAD)), lambda i, h: (0, h)),
                  vec(QK_NOPE), vec(LANES),
                  pl.BlockSpec((tm, LANES), lambda i, h: (i, 0))],
        out_specs=[head_out(dq), head_out(dv)],
        scratch_shapes=[pltpu.VMEM((tm, KV_LORA), _BF16)],
        compiler_params=_params("parallel", "arbitrary"),
        name="mla_kv_proj",
    )(c, c, g_ckv.reshape(1, -1), w_kv, g_kn.reshape(1, -1), gv_k, cs)
    tq = ATTN_TILE
    o = pl.pallas_call(
        _mla_attn_kernel,
        out_shape=jax.ShapeDtypeStruct((B, S, H * V_HEAD), _BF16),
        grid=(B, H // hb, S // tq),
        in_specs=[pl.BlockSpec((None, hb, tq, dq), lambda b, h, i: (b, h, i, 0)),
                  pl.BlockSpec((None, hb, S, dq), lambda b, h, i: (b, h, 0, 0)),
                  pl.BlockSpec((None, hb, S, dv), lambda b, h, i: (b, h, 0, 0))],
        out_specs=pl.BlockSpec((None, tq, hb * V_HEAD), lambda b, h, i: (b, i, h)),
        scratch_shapes=[pltpu.VMEM((hb, tq, LANES), _F32), pltpu.VMEM((hb, tq, dv), _F32)],
        compiler_params=_params("parallel", "parallel", "arbitrary"),
        name="mla_attention",
    )(q, k, v)
    return o.reshape(T, H * V_HEAD)


def _router_kernel(x_ref, g_ref, w_ref, r_ref, hn_ref):
    hn = _rms(x_ref[...], g_ref[...])
    hn_ref[...] = hn.astype(hn_ref.dtype)
    logits = jnp.dot(hn, w_ref[...], precision=lax.Precision.HIGHEST, preferred_element_type=_F32)
    lane = lax.broadcasted_iota(jnp.int32, logits.shape, 1)
    lane_f = lane.astype(_F32)
    neg = -jnp.inf
    lg = jnp.where(lane < N_EXPERTS, logits, neg)
    v1 = jnp.max(lg, axis=1, keepdims=True)
    i1 = jnp.min(jnp.where(lg == v1, lane_f, float(LANES)), axis=1, keepdims=True)
    lg2 = jnp.where(lane_f == i1, neg, lg)
    v2 = jnp.max(lg2, axis=1, keepdims=True)
    i2 = jnp.min(jnp.where(lg2 == v2, lane_f, float(LANES)), axis=1, keepdims=True)
    e2 = jnp.exp(v2 - v1)
    w1 = 1.0 / (1.0 + e2)
    w2 = e2 / (1.0 + e2)
    out = jnp.where(lane < N_EXPERTS, logits, 0.0)
    out = jnp.where(lane == N_EXPERTS, i1, out)
    out = jnp.where(lane == N_EXPERTS + 1, i2, out)
    out = jnp.where(lane == N_EXPERTS + 2, w1, out)
    out = jnp.where(lane == N_EXPERTS + 3, w2, out)
    r_ref[...] = out


def router(x, g, w_router_padded):
    T, d = x.shape
    return pl.pallas_call(
        _router_kernel,
        out_shape=[jax.ShapeDtypeStruct((T, LANES), _F32), jax.ShapeDtypeStruct((T, d), _BF16)],
        grid=(T // ROW_TILE,),
        in_specs=[pl.BlockSpec((ROW_TILE, d), lambda i: (i, 0)),
                  pl.BlockSpec((1, d), lambda i: (0, 0)),
                  pl.BlockSpec((d, LANES), lambda i: (0, 0))],
        out_specs=[pl.BlockSpec((ROW_TILE, LANES), lambda i: (i, 0)), pl.BlockSpec((ROW_TILE, d), lambda i: (i, 0))],
        compiler_params=_params("parallel"),
        name="moe_router",
    )(x, g.reshape(1, d), w_router_padded)


def _row_copy(src_hbm, dst_vmem, sem, src_row, dst_row):
    return pltpu.make_async_copy(src_hbm.at[pl.ds(src_row, 1)], dst_vmem.at[pl.ds(dst_row, 1)], sem)


def _index_spec(n):
    return pl.BlockSpec((None, 1, n), lambda i: (i, 0, 0), memory_space=pltpu.SMEM)


def _gather_kernel(tile_ref, chunk_ref, flag_ref, pos_ref, x_ref, o_ref, acc):
    del chunk_ref
    s = pl.program_id(0)
    flags = flag_ref[s]
    tr, ch = acc.shape[0], x_ref.shape[0]

    @pl.when((flags & _G_VALID) != 0)
    def _():
        rows = tile_ref[s] * tr + lax.broadcasted_iota(jnp.int32, (tr, ch), 0)
        pos = pos_ref[...]
        hit = jnp.logical_or(rows == pos[0:1, :], rows == pos[1:2, :])
        onehot = jnp.where(hit, 1.0, 0.0).astype(_BF16)
        part = jnp.dot(onehot, x_ref[...], preferred_element_type=_F32)

        @pl.when((flags & _G_FIRST) != 0)
        def _():
            acc[...] = part

        @pl.when((flags & _G_FIRST) == 0)
        def _():
            acc[...] = acc[...] + part

        @pl.when((flags & _G_LAST) != 0)
        def _():
            o_ref[...] = acc[...].astype(o_ref.dtype)


_G_VALID, _G_FIRST, _G_LAST = 1, 2, 4


def gather_rows(x, pos, n_tiles, tr):
    T, d = x.shape
    r = n_tiles * tr
    ch = GATHER_CHUNK
    n_steps = N_EXPERTS * (T // ch) + n_tiles
    tile_ids = jnp.arange(n_tiles, dtype=jnp.int32)
    member = (pos.reshape(-1) // tr)[:, None] == tile_ids[None, :]
    token = (jnp.arange(T * TOP_K, dtype=jnp.int32) // TOP_K)[:, None]
    first_tok = jnp.min(jnp.where(member, token, T), axis=0)
    last_tok = jnp.max(jnp.where(member, token, -1), axis=0)
    lo = jnp.where(last_tok >= 0, first_tok, 0) // ch
    hi = jnp.maximum(last_tok, 0) // ch
    count = hi - lo + 1
    end = jnp.cumsum(count)
    steps = jnp.arange(n_steps, dtype=jnp.int32)
    valid = steps < end[-1]
    tile = jnp.minimum(jnp.sum((end[None, :] <= steps[:, None]).astype(jnp.int32), axis=1), n_tiles - 1)
    tile_oh = (tile[:, None] == jnp.arange(n_tiles, dtype=jnp.int32)[None, :]).astype(jnp.int32)
    pick = lambda v: jnp.sum(tile_oh * v[None, :], axis=1)
    start = pick(end - count)
    chunk = jnp.where(valid, pick(lo) + steps - start, hi[-1])
    flags = (valid * _G_VALID + jnp.logical_and(valid, steps == start) * _G_FIRST
             + jnp.logical_and(valid, steps == pick(end) - 1) * _G_LAST)
    return pl.pallas_call(
        _gather_kernel,
        out_shape=jax.ShapeDtypeStruct((r, d), x.dtype),
        grid_spec=pltpu.PrefetchScalarGridSpec(
            num_scalar_prefetch=3,
            grid=(n_steps,),
            in_specs=[pl.BlockSpec((TOP_K, ch), lambda s, tile, chunk, flag: (0, chunk[s])),
                      pl.BlockSpec((ch, d), lambda s, tile, chunk, flag: (chunk[s], 0))],
            out_specs=pl.BlockSpec((tr, d), lambda s, tile, chunk, flag: (tile[s], 0)),
            scratch_shapes=[pltpu.VMEM((tr, d), _F32)],
        ),
        compiler_params=_params("arbitrary"),
        name="moe_gather",
    )(tile.astype(jnp.int32), chunk.astype(jnp.int32), flags.astype(jnp.int32), pos.T, x)


def _combine_kernel(pos_ref, x_ref, w_ref, y_hbm, o_ref, buf_a, buf_b, sem):
    tt = buf_a.shape[0]

    def start(r, carry):
        _row_copy(y_hbm, buf_a, sem, pos_ref[0, TOP_K * r], r).start()
        _row_copy(y_hbm, buf_b, sem, pos_ref[0, TOP_K * r + 1], r).start()
        return carry

    def wait(r, carry):
        _row_copy(y_hbm, buf_a, sem, 0, r).wait()
        _row_copy(y_hbm, buf_b, sem, 0, r).wait()
        return carry

    lax.fori_loop(0, tt, start, 0)
    lax.fori_loop(0, tt, wait, 0)
    w = w_ref[...]
    o_ref[...] = x_ref[...] + w[:, 0:1] * buf_a[...] + w[:, 1:2] * buf_b[...]


def moe_combine(x, y, pos_flat, weights, tt):
    T, d = x.shape
    return pl.pallas_call(
        _combine_kernel,
        out_shape=jax.ShapeDtypeStruct((T, d), _F32),
        grid=(T // tt,),
        in_specs=[_index_spec(tt * TOP_K), pl.BlockSpec((tt, d), lambda i: (i, 0)),
                  pl.BlockSpec((tt, TOP_K), lambda i: (i, 0)), pl.BlockSpec(memory_space=pl.ANY)],
        out_specs=pl.BlockSpec((tt, d), lambda i: (i, 0)),
        scratch_shapes=[pltpu.VMEM((tt, d), _F32), pltpu.VMEM((tt, d), _F32), pltpu.SemaphoreType.DMA],
        compiler_params=_params("arbitrary"),
        name="moe_combine",
    )(pos_flat.reshape(T // tt, 1, tt * TOP_K), x, weights, y)


def _moe_plan(route, tm, n_tiles):
    T = route.shape[0]
    e_flat = route[:, N_EXPERTS:N_EXPERTS + TOP_K].astype(jnp.int32).reshape(-1)
    onehot = (e_flat[:, None] == jnp.arange(N_EXPERTS, dtype=jnp.int32)[None, :]).astype(jnp.int32)
    csum = jnp.cumsum(onehot, axis=0)
    rank = jnp.sum(csum * onehot, axis=1) - 1
    tiles_e = (csum[-1] + tm - 1) // tm
    tile_end = jnp.cumsum(tiles_e)
    tile_start = tile_end - tiles_e
    pos = jnp.sum(onehot * tile_start[None, :], axis=1) * tm + rank
    tile_ids = jnp.arange(n_tiles, dtype=jnp.int32)
    tile_expert = jnp.minimum(jnp.sum((tile_ids[:, None] >= tile_end[None, :]).astype(jnp.int32), axis=1),
                              N_EXPERTS - 1).astype(jnp.int32)
    n_used = tile_end[-1:].astype(jnp.int32)
    experts = jnp.arange(N_EXPERTS, dtype=jnp.int32)
    has = tiles_e > 0
    later = jnp.where(jnp.logical_and(has[None, :], experts[None, :] > experts[:, None]), experts[None, :], N_EXPERTS)
    next_e = jnp.min(later, axis=1)
    last_e = next_e == N_EXPERTS
    next_e = jnp.where(last_e, jnp.min(jnp.where(has, experts, N_EXPERTS)), next_e)
    group_e = jnp.cumsum(has.astype(jnp.int32)) - 1
    tile_onehot = (tile_expert[:, None] == experts[None, :]).astype(jnp.int32)

    def of_tile(per_expert):
        return jnp.sum(tile_onehot * per_expert.astype(jnp.int32)[None, :], axis=1)

    index = tile_ids - of_tile(tile_start)
    groups = (tile_expert, (index == 0).astype(jnp.int32), index, jnp.maximum(of_tile(tiles_e), 1),
              of_tile(next_e), of_tile(last_e), of_tile(group_e),
              jnp.full((n_tiles,), jnp.sum(has.astype(jnp.int32)), jnp.int32))
    return groups, n_used, pos.astype(jnp.int32).reshape(T, TOP_K)


def moe_layer(x, g, w_router, we_gate, we_up, we_down):
    T, d = x.shape
    tm = MOE_ROW_TILE
    n_tiles = (T * TOP_K) // tm + N_EXPERTS
    w_r = jnp.zeros((d, LANES), _F32).at[:, :N_EXPERTS].set(w_router)
    route, hn = router(x, g, w_r)
    plan, n_used, pos = _moe_plan(route, tm, n_tiles)
    xs = gather_rows(hn, pos, n_tiles, tm)
    h = streamed_matmul(xs, [we_gate, we_up], plan, n_used, tm=tm, tn=UP_COLS, n_chunks=UP_CHUNKS, out_dtype=_BF16)
    y = streamed_matmul(h, [we_down], plan, n_used, tm=tm, tn=DOWN_COLS, n_chunks=DOWN_CHUNKS, out_dtype=_F32)
    weights = route[:, N_EXPERTS + TOP_K:N_EXPERTS + 2 * TOP_K]
    return moe_combine(x, y, pos.reshape(-1), weights, tm)


def _rope_tables(positions, half):
    inv = ROPE_BASE ** (-jnp.arange(half, dtype=_F32) / half)
    ang = positions.astype(_F32)[..., None] * inv
    return jnp.cos(ang), jnp.sin(ang)


def _rotate_half_cols(w, width):
    lead = w.shape[:-1]
    w2 = w.reshape(lead + (-1, 2, width // 2))
    return jnp.stack([-w2[..., 1, :], w2[..., 0, :]], axis=-2).reshape(w.shape)


def _swap_halves(g):
    half = g.shape[-1] // 2
    return jnp.concatenate([g[..., half:], g[..., :half]], axis=-1)


def kernel(x, positions, g_mix_norm, g_ffn_norm, w_in, conv_w, b_gates, g_mlstm_out, g_ret_out, w_mix_out,
           ffn_gate, ffn_up, ffn_down, w_dqkv, g_cq, g_ckv, w_uq, w_ukv, g_qn, g_qr, g_kn, g_kr, w_o,
           w_router, we_gate, we_up, we_down):
    B, S, D = x.shape
    T = B * S
    xf = x.reshape(T, D)
    n_gate = 2 * MLSTM_HEADS
    main = 4 * MLSTM_HEADS * HEAD_DIM

    w = w_in[0]
    w_gate = jnp.zeros((D, LANES), _F32).at[:, :n_gate].set(w[:, main:main + n_gate])
    g0 = g_mix_norm[0]
    plan, nu = _dense_stream_plan(T, ROW_TILE)
    proj_m = streamed_matmul(xf, [w_in], plan, nu, tm=ROW_TILE, tn=PROJ_COLS, n_chunks=UP_CHUNKS, out_dtype=_F32,
                             gain=g0, n_out=main)
    proj_r = streamed_matmul(xf, [w[None, :, main + n_gate:]], plan, nu, tm=ROW_TILE, tn=PROJ_COLS,
                             n_chunks=UP_CHUNKS, out_dtype=_F32, gain=g0)
    gates = norm_matmul(xf, g0, w_gate[None], tn=LANES)[:, :n_gate] + b_gates[0][None, :]
    cos_r, sin_r = _rope_tables(positions, HEAD_DIM // 2)
    log_gamma = jnp.log1p(-jnp.exp2(-5.0 - jnp.arange(RET_HEADS, dtype=_F32)))
    mix = recurrent_mixer(proj_m.reshape(B, S, -1), proj_r.reshape(B, S, -1), gates.reshape(B, S, n_gate),
                          cos_r, sin_r, conv_w[0], g_mlstm_out[0], g_ret_out[0], log_gamma).reshape(T, -1)
    w_mix = w_mix_out[0].reshape(2, MLSTM_HEADS, HEAD_DIM, D).transpose(1, 0, 2, 3).reshape(-1, D)
    xf = dense_matmul(mix, w_mix, tn=1024, out_dtype=_F32, residual=xf)

    hmid = streamed_matmul(xf, [ffn_gate, ffn_up], plan, nu, tm=ROW_TILE, tn=UP_COLS, n_chunks=UP_CHUNKS,
                           out_dtype=_BF16, gain=g_ffn_norm[0])
    xf = streamed_matmul(hmid, [ffn_down], plan, nu, tm=ROW_TILE, tn=DENSE_DOWN_COLS, n_chunks=DOWN_CHUNKS,
                         out_dtype=_F32, residual=xf)

    H = MLA_HEADS
    wd = w_dqkv[0]
    w_kr = wd[:, Q_LORA + KV_LORA:]
    wd_full = jnp.concatenate([wd, _rotate_half_cols(w_kr, QK_ROPE)], axis=1)
    c = norm_matmul(xf, g_mix_norm[1], wd_full[None], tn=wd_full.shape[1] // 3)
    wq = w_uq[0].reshape(Q_LORA, H, QK_NOPE + QK_ROPE)
    wq_r = wq[..., QK_NOPE:]
    w_q = jnp.concatenate([wq, _rotate_half_cols(wq_r, QK_ROPE)], axis=-1).reshape(Q_LORA, -1)
    cos_m, sin_m = _rope_tables(positions, QK_ROPE // 2)
    cs = jnp.concatenate([cos_m, cos_m, sin_m, sin_m], axis=-1).reshape(T, LANES)
    gv_q = jnp.concatenate([g_qr[0], _swap_halves(g_qr[0])]).reshape(1, LANES)
    gv_k = jnp.concatenate([g_kr[0], _swap_halves(g_kr[0])]).reshape(1, LANES)
    attn = mla_attention(c, cs, w_q, w_ukv[0], g_cq[0], g_ckv[0], g_qn[0], gv_q, g_kn[0], gv_k, B, S)
    xf = dense_matmul(attn, w_o[0], tn=1024, out_dtype=_F32, residual=xf)

    xf = moe_layer(xf, g_ffn_norm[1], w_router[0], we_gate[0], we_up[0], we_down[0])
    return xf.reshape(B, S, D)
```

```python
import functools

import jax
import jax.numpy as jnp
import numpy as np
from jax import lax
from jax.experimental import pallas as pl
from jax.experimental.pallas import tpu as pltpu

EPS = 1e-6
ROPE_BASE = 10000.0
CONV_WIDTH = 4
MLSTM_HEADS = 4
RET_HEADS = 4
HEAD_DIM = 256
REC_CHUNK = 256
MLA_HEADS = 16
Q_LORA = 512
KV_LORA = 512
QK_NOPE = 128
QK_ROPE = 64
V_HEAD = 128
N_EXPERTS = 8
TOP_K = 2

LANES = 128
SUBLANES = 8
VMEM_LIMIT_BYTES = 48 * 1024 * 1024

ROW_TILE = 512
DENSE_ROW_TILE = 1024
DENSE_VMEM_LIMIT_BYTES = 58 * 1024 * 1024
NORM_ROW_TILE = 1024
NORM_VMEM_LIMIT_BYTES = 56 * 1024 * 1024
MOE_ROW_TILE = 256
UP_COLS = 1408
UP_CHUNKS = 8
PROJ_COLS = 2048
DOWN_COLS = 1024
DENSE_DOWN_COLS = 512
DOWN_CHUNKS = 11
ATTN_TILE = 512
MLA_HEAD_GROUP = 4
GATHER_CHUNK = 512
LOG2_E = 1.4426950408889634

_F32 = jnp.float32
_BF16 = jnp.bfloat16
_NT = (((1,), (1,)), ((), ()))
_TN = (((0,), (0,)), ((), ()))


def _params(*semantics, vmem_limit_bytes=VMEM_LIMIT_BYTES):
    return pltpu.CompilerParams(dimension_semantics=semantics, vmem_limit_bytes=vmem_limit_bytes)


def _sigmoid(x):
    return 1.0 / (1.0 + jnp.exp(-x))


def _rms(x, g):
    return x * lax.rsqrt(jnp.mean(x * x, axis=-1, keepdims=True) + EPS) * g


def _gmm_kernel(te_ref, nu_ref, a_ref, w_ref, *rest):
    del te_ref
    o_ref = rest[-1]

    @pl.when(pl.program_id(1) < nu_ref[0])
    def _():
        acc = jnp.dot(a_ref[...], w_ref[...].astype(_BF16), preferred_element_type=_F32)
        if len(rest) == 2:
            acc = acc + rest[0][...]
        o_ref[...] = acc.astype(o_ref.dtype)

    @pl.when(pl.program_id(1) >= nu_ref[0])
    def _():
        o_ref[...] = jnp.zeros_like(o_ref)


def grouped_matmul(a, w, tile_expert, n_used, *, tm, tn, out_dtype, residual=None):
    m, k = a.shape
    n = w.shape[2]
    n_tiles = m // tm

    def row(j, i, te, nu):
        return jnp.minimum(i, nu[0] - 1)

    in_specs = [pl.BlockSpec((tm, k), lambda j, i, te, nu: (row(j, i, te, nu), 0)),
                pl.BlockSpec((None, k, tn), lambda j, i, te, nu: (te[row(j, i, te, nu)], 0, j))]
    args = [a, w]
    if residual is not None:
        in_specs.append(pl.BlockSpec((tm, tn), lambda j, i, te, nu: (row(j, i, te, nu), j)))
        args.append(residual)
    return pl.pallas_call(
        _gmm_kernel,
        out_shape=jax.ShapeDtypeStruct((m, n), out_dtype),
        grid_spec=pltpu.PrefetchScalarGridSpec(
            num_scalar_prefetch=2,
            grid=(n // tn, n_tiles),
            in_specs=in_specs,
            out_specs=pl.BlockSpec((tm, tn), lambda j, i, te, nu: (i, j)),
        ),
        compiler_params=_params("arbitrary", "arbitrary"),
        name="grouped_matmul",
    )(tile_expert, n_used, *args)


_P_EXPERT, _P_FIRST, _P_LO, _P_HI, _P_NEXT, _P_LAST, _P_GROUP, _P_NGROUPS = range(8)


def _streamed_kernel(plan_ref, nu_ref, a_ref, *refs, n_mats, has_gain, has_res, n_col_tiles):
    w_hbm = refs[:n_mats]
    gain_ref = refs[n_mats] if has_gain else None
    res_ref = refs[n_mats + int(has_gain)] if has_res else None
    o_ref, wbf, stage, sem = refs[n_mats + int(has_gain) + int(has_res):]
    j = pl.program_id(0)
    i = pl.program_id(1)
    _, _, k, tn = wbf.shape
    kc = stage.shape[2]
    n_chunks = k // kc

    def chunk_copy(e, jj, c, m):
        src = w_hbm[m].at[e, pl.ds(pl.multiple_of(c * kc, kc), kc), pl.ds(pl.multiple_of(jj * tn, LANES), tn)]
        return pltpu.make_async_copy(src, stage.at[c & 1, m], sem.at[c & 1, m])

    def prime(e, jj):
        for c in range(2):
            for m in range(n_mats):
                chunk_copy(e, jj, c, m).start()

    def fetch(e, jj, slot, lo, hi):
        def body(c, carry):
            for m in range(n_mats):
                chunk_copy(e, jj, c, m).wait()
                wbf[slot, m, pl.ds(pl.multiple_of(c * kc, kc), kc), :] = stage[c & 1, m].astype(_BF16)

                @pl.when(c + 2 < n_chunks)
                def _():
                    chunk_copy(e, jj, c + 2, m).start()
            return carry

        lax.fori_loop(lo, hi, body, 0)

    expert = plan_ref[_P_EXPERT, i]
    used = i < nu_ref[0]
    cur = (j * plan_ref[_P_NGROUPS, 0] + plan_ref[_P_GROUP, i]) & 1
    in_last_group = plan_ref[_P_LAST, i] == 1
    has_next = jnp.logical_and(used, jnp.logical_not(jnp.logical_and(in_last_group, j == n_col_tiles - 1)))
    next_e = plan_ref[_P_NEXT, i]
    next_j = jnp.where(in_last_group, j + 1, j)

    @pl.when(jnp.logical_and(j == 0, i == 0))
    def _():
        prime(expert, 0)
        fetch(expert, 0, 0, 0, n_chunks)

    @pl.when(jnp.logical_and(has_next, plan_ref[_P_FIRST, i] == 1))
    def _():
        prime(next_e, next_j)

    @pl.when(used)
    def _():
        a = a_ref[...]
        if has_gain:
            a = _rms(a, gain_ref[...]).astype(_BF16)
        if n_mats == 2:
            g = jnp.dot(a, wbf[cur, 0], preferred_element_type=_F32)
            u = jnp.dot(a, wbf[cur, 1], preferred_element_type=_F32)
            out = g * _sigmoid(g) * u
        else:
            out = jnp.dot(a, wbf[cur, 0], preferred_element_type=_F32)
            if has_res:
                out = out + res_ref[...]
        o_ref[...] = out.astype(o_ref.dtype)

    @pl.when(has_next)
    def _():
        fetch(next_e, next_j, 1 - cur, plan_ref[_P_LO, i], plan_ref[_P_HI, i])

    @pl.when(jnp.logical_not(used))
    def _():
        o_ref[...] = jnp.zeros_like(o_ref)


def _chunk_shares(index, size, n_chunks):
    return (index * n_chunks) // size, ((index + 1) * n_chunks) // size


def streamed_matmul(a, weights, groups, n_used, *, tm, tn, n_chunks, out_dtype, gain=None, residual=None,
                    n_out=None, vmem_limit_bytes=VMEM_LIMIT_BYTES):
    expert, first, index, size, nxt, last, group, n_groups = groups
    lo, hi = _chunk_shares(index, size, n_chunks)
    plan = jnp.stack([expert, first, lo, hi, nxt, last, group, n_groups]).astype(jnp.int32)
    m, k = a.shape
    n = weights[0].shape[2] if n_out is None else n_out
    n_mats = len(weights)

    def row(j, i, plan, nu):
        return jnp.minimum(i, nu[0] - 1)

    in_specs = [pl.BlockSpec((tm, k), lambda j, i, plan, nu: (row(j, i, plan, nu), 0))]
    in_specs += [pl.BlockSpec(memory_space=pl.ANY)] * n_mats
    args = [a, *weights]
    if gain is not None:
        in_specs.append(pl.BlockSpec((1, k), lambda j, i, plan, nu: (0, 0)))
        args.append(gain.reshape(1, k))
    if residual is not None:
        in_specs.append(pl.BlockSpec((tm, tn), lambda j, i, plan, nu: (row(j, i, plan, nu), j)))
        args.append(residual)
    return pl.pallas_call(
        functools.partial(_streamed_kernel, n_mats=n_mats, has_gain=gain is not None, has_res=residual is not None,
                          n_col_tiles=n // tn),
        out_shape=jax.ShapeDtypeStruct((m, n), out_dtype),
        grid_spec=pltpu.PrefetchScalarGridSpec(
            num_scalar_prefetch=2,
            grid=(n // tn, m // tm),
            in_specs=in_specs,
            out_specs=pl.BlockSpec((tm, tn), lambda j, i, plan, nu: (i, j)),
            scratch_shapes=[pltpu.VMEM((2, n_mats, k, tn), _BF16),
                            pltpu.VMEM((2, n_mats, k // n_chunks, tn), _F32),
                            pltpu.SemaphoreType.DMA((2, n_mats))],
        ),
        compiler_params=_params("arbitrary", "arbitrary", vmem_limit_bytes=vmem_limit_bytes),
        name="streamed_swiglu_up" if n_mats == 2 else "streamed_matmul",
    )(plan, n_used, *args)


def _dense_stream_plan(m, tm):
    n_tiles = m // tm
    ids = jnp.arange(n_tiles, dtype=jnp.int32)
    zero = jnp.zeros_like(ids)
    one = jnp.ones_like(ids)
    groups = (zero, (ids == 0).astype(jnp.int32), ids, one * n_tiles, zero, one, zero, one)
    return groups, jnp.full((1,), n_tiles, jnp.int32)


def _dense_plan(m, tm):
    n_tiles = m // tm
    return jnp.zeros((n_tiles,), jnp.int32), jnp.full((1,), n_tiles, jnp.int32)


def dense_matmul(a, w, *, tn, out_dtype, residual=None):
    te, nu = _dense_plan(a.shape[0], ROW_TILE)
    return grouped_matmul(a, w[None], te, nu, tm=ROW_TILE, tn=tn, out_dtype=out_dtype, residual=residual)


def _norm_matmul_kernel(x_ref, g_ref, w_ref, o_ref, xn_s):
    @pl.when(pl.program_id(1) == 0)
    def _():
        xn_s[...] = _rms(x_ref[...], g_ref[...]).astype(_BF16)

    o_ref[...] = jnp.dot(xn_s[...], w_ref[...].astype(_BF16), preferred_element_type=_F32)


def norm_matmul(x, gain, w, *, tn):
    m, k = x.shape
    n = w.shape[2]
    tm = NORM_ROW_TILE
    return pl.pallas_call(
        _norm_matmul_kernel,
        out_shape=jax.ShapeDtypeStruct((m, n), _F32),
        grid=(m // tm, n // tn),
        in_specs=[pl.BlockSpec((tm, k), lambda i, j: (i, 0)),
                  pl.BlockSpec((1, k), lambda i, j: (0, 0)),
                  pl.BlockSpec((None, k, tn), lambda i, j: (0, 0, j))],
        out_specs=pl.BlockSpec((tm, tn), lambda i, j: (i, j)),
        scratch_shapes=[pltpu.VMEM((tm, k), _BF16)],
        compiler_params=pltpu.CompilerParams(dimension_semantics=("parallel", "arbitrary"),
                                             vmem_limit_bytes=NORM_VMEM_LIMIT_BYTES),
        name="norm_matmul",
    )(x, gain.reshape(1, k), w)


def _recurrent_kernel(q_ref, k_ref, v_ref, o_ref, rq_ref, rk_ref, rv_ref, rg_ref,
                      gt_ref, gc_ref, cos_ref, sin_ref, cwq_ref, cwk_ref, gm_ref, gr_ref, lg_ref,
                      mix_ref,
                      c_s, n_s, m_s, r_s, qbuf, kbuf):
    L, dh = q_ref.shape
    halo = SUBLANES
    inv_sqrt_d = dh ** -0.5

    @pl.when(pl.program_id(2) == 0)
    def _():
        c_s[...] = jnp.zeros_like(c_s)
        n_s[...] = jnp.zeros_like(n_s)
        m_s[...] = jnp.full_like(m_s, -jnp.inf)
        r_s[...] = jnp.zeros_like(r_s)
        qbuf[0:halo, :] = jnp.zeros((halo, dh), _F32)
        kbuf[0:halo, :] = jnp.zeros((halo, dh), _F32)

    def conv_silu(x_ref, buf, w_ref):
        buf[halo:halo + L, :] = x_ref[...]
        w = w_ref[...]
        y = w[CONV_WIDTH - 1:CONV_WIDTH, :] * buf[halo:halo + L, :]
        for j in range(CONV_WIDTH - 1):
            off = halo - (CONV_WIDTH - 1) + j
            y = y + w[j:j + 1, :] * buf[off:off + L, :]
        buf[0:halo, :] = buf[L:L + halo, :]
        return y * _sigmoid(y)

    row = lax.broadcasted_iota(jnp.int32, (L, L), 0)
    col = lax.broadcasted_iota(jnp.int32, (L, L), 1)
    causal = col <= row

    q = conv_silu(q_ref, qbuf, cwq_ref)
    k = conv_silu(k_ref, kbuf, cwk_ref)
    qb = q.astype(_BF16)
    vb = v_ref[...].astype(_BF16)

    def log_sigmoid(x):
        return jnp.minimum(x, 0.0) - jnp.log1p(jnp.exp(-jnp.abs(x)))

    gt = gt_ref[...]
    gc = gc_ref[...]
    i_row = gt[0:1, :]
    f_row = log_sigmoid(gt[1:2, :])
    i_col = gc[:, 0:1]
    f_col = log_sigmoid(gc[:, 1:2])
    hi = lax.Precision.HIGHEST
    b_col = jnp.dot(causal.astype(_F32), f_col, precision=hi, preferred_element_type=_F32)
    b_row = jnp.dot(f_row, (row <= col).astype(_F32), precision=hi, preferred_element_type=_F32)
    g_tot = b_col[L - 1:L, :]
    m_prev = m_s[...]

    log_d = jnp.where(causal, b_col - b_row + i_row, -jnp.inf)
    m_inter = b_col + m_prev
    m_t = jnp.maximum(jnp.max(log_d, axis=1, keepdims=True), m_inter)
    d_m = jnp.exp(log_d - m_t)
    inter = jnp.exp(m_inter - m_t)
    s = lax.dot_general(qb, k.astype(_BF16), _NT, preferred_element_type=_F32) * inv_sqrt_d
    s_m = s * d_m
    num = (jnp.dot(s_m.astype(_BF16), vb, preferred_element_type=_F32)
           + inter * jnp.dot(qb, c_s[...].astype(_BF16), preferred_element_type=_F32))
    den = jnp.sum(s_m, axis=1, keepdims=True) + inter * jnp.sum(q * n_s[...], axis=1, keepdims=True)
    h = num / jnp.maximum(jnp.abs(den), jnp.exp(-m_t))

    log_w = g_tot - b_col + i_col
    m_new = jnp.maximum(g_tot + m_prev, jnp.max(log_w, axis=0, keepdims=True))
    w_col = jnp.exp(log_w - m_new)
    decay = jnp.exp(g_tot + m_prev - m_new)
    kw = k * (w_col * inv_sqrt_d)
    c_s[...] = decay * c_s[...] + lax.dot_general(kw.astype(_BF16), vb, _TN, preferred_element_type=_F32)
    n_s[...] = decay * n_s[...] + jnp.sum(kw, axis=0, keepdims=True)
    m_s[...] = m_new

    mix_ref[:, :dh] = _rms(_sigmoid(o_ref[...]) * h, gm_ref[...]).astype(mix_ref.dtype)

    half = dh // 2
    cos = cos_ref[...]
    sin = sin_ref[...]

    def rope(x):
        x1 = x[:, :half]
        x2 = x[:, half:]
        return jnp.concatenate([x1 * cos - x2 * sin, x2 * cos + x1 * sin], axis=1)

    rq = rope(rq_ref[...]).astype(_BF16)
    rk = rope(rk_ref[...])
    rvb = rv_ref[...].astype(_BF16)
    lg = lg_ref[...]
    dist = (row - col).astype(_F32)
    d_r = jnp.where(causal, jnp.exp(lg * jnp.maximum(dist, 0.0)), 0.0)
    t_col = lax.broadcasted_iota(jnp.int32, (L, 1), 0).astype(_F32)
    q_decay = jnp.exp(lg * (t_col + 1.0))
    k_decay = jnp.exp(lg * (L - 1.0 - t_col))
    chunk_decay = jnp.exp(lg * L)
    s_r = lax.dot_general(rq, rk.astype(_BF16), _NT, preferred_element_type=_F32) * inv_sqrt_d * d_r
    out_r = (jnp.dot(s_r.astype(_BF16), rvb, preferred_element_type=_F32)
             + jnp.dot(rq, r_s[...].astype(_BF16), preferred_element_type=_F32) * q_decay)
    r_s[...] = chunk_decay * r_s[...] + lax.dot_general(
        (rk * (k_decay * inv_sqrt_d)).astype(_BF16), rvb, _TN, preferred_element_type=_F32)
    rg = rg_ref[...]
    mix_ref[:, dh:] = (rg * _sigmoid(rg) * _rms(out_r, gr_ref[...])).astype(mix_ref.dtype)


def recurrent_mixer(proj_m, proj_r, gates, cos, sin, conv_w, g_mlstm_out, g_ret_out, log_gamma):
    B, S, _ = proj_m.shape
    H, dh, L = MLSTM_HEADS, HEAD_DIM, REC_CHUNK
    g4 = gates.reshape(B, S, 2, H)
    gt = g4.transpose(0, 3, 2, 1)
    gc = g4.transpose(0, 3, 1, 2)

    def pspec(group):
        return pl.BlockSpec((None, L, dh), lambda b, h, c: (b, c, group * H + h))

    in_specs = [pspec(g) for g in range(4)] * 2 + [
        pl.BlockSpec((None, None, 2, L), lambda b, h, c: (b, h, 0, c)),
        pl.BlockSpec((None, None, L, 2), lambda b, h, c: (b, h, c, 0)),
        pl.BlockSpec((None, L, dh // 2), lambda b, h, c: (b, c, 0)),
        pl.BlockSpec((None, L, dh // 2), lambda b, h, c: (b, c, 0)),
        pl.BlockSpec((CONV_WIDTH, dh), lambda b, h, c: (0, h)),
        pl.BlockSpec((CONV_WIDTH, dh), lambda b, h, c: (0, H + h)),
        pl.BlockSpec((1, dh), lambda b, h, c: (0, h)),
        pl.BlockSpec((1, dh), lambda b, h, c: (0, h)),
        pl.BlockSpec((None, 1, 1), lambda b, h, c: (h, 0, 0)),
    ]
    return pl.pallas_call(
        _recurrent_kernel,
        out_shape=jax.ShapeDtypeStruct((B, S, 2 * H * dh), _BF16),
        grid=(B, H, S // L),
        in_specs=in_specs,
        out_specs=pl.BlockSpec((None, L, 2 * dh), lambda b, h, c: (b, c, h)),
        scratch_shapes=[pltpu.VMEM((dh, dh), _F32), pltpu.VMEM((1, dh), _F32), pltpu.VMEM((1, 1), _F32),
                        pltpu.VMEM((dh, dh), _F32),
                        pltpu.VMEM((L + SUBLANES, dh), _F32), pltpu.VMEM((L + SUBLANES, dh), _F32)],
        compiler_params=_params("parallel", "parallel", "arbitrary"),
        name="recurrent_mixer",
    )(*([proj_m] * 4), *([proj_r] * 4), gt, gc, cos, sin, conv_w, conv_w,
      g_mlstm_out.reshape(1, H * dh), g_ret_out.reshape(1, H * dh), log_gamma.reshape(H, 1, 1))


def _rope_pair(z2, cs, gvec, sumsq=None):
    lane = lax.broadcasted_iota(jnp.int32, z2.shape, 1)
    first = lane < QK_ROPE
    if sumsq is None:
        sumsq = jnp.sum(jnp.where(first, z2 * z2, 0.0), axis=1, keepdims=True)
    t = z2 * lax.rsqrt(sumsq * (1.0 / QK_ROPE) + EPS) * (cs * gvec)
    return jnp.where(first, t + pltpu.roll(t, QK_ROPE, 1), 0.0)


def _group_sumsq(z, ones_ref):
    return jnp.dot((z * z).astype(_BF16), ones_ref[...], preferred_element_type=_F32)


def _group_ones(width, pairs):
    m = np.zeros((width, width), np.float32)
    for (k0, k1), (n0, n1) in pairs:
        m[k0:k1, n0:n1] = 1.0
    return jnp.asarray(m, _BF16)


def _mla_q_kernel(c_ref, gcq_ref, w_ref, ones_ref, gqn_ref, gqr_ref, cs_ref, q_ref, cn_s, *, scale):
    @pl.when(pl.program_id(1) == 0)
    def _():
        cn_s[...] = _rms(c_ref[...], gcq_ref[...]).astype(_BF16)

    dq = q_ref.shape[-1]
    z = jnp.dot(cn_s[...], w_ref[...].astype(_BF16), preferred_element_type=_F32)
    cs = cs_ref[...]
    for h in range(q_ref.shape[0]):
        zh = z[:, h * dq:(h + 1) * dq]
        ssh = _group_sumsq(zh, ones_ref)
        qn = zh[:, :QK_NOPE] * lax.rsqrt(ssh[:, :QK_NOPE] * (1.0 / QK_NOPE) + EPS) * gqn_ref[...]
        qr = _rope_pair(zh[:, QK_NOPE:], cs, gqr_ref[...], ssh[:, QK_NOPE:])
        q_ref[h] = (jnp.concatenate([qn, qr], axis=1) * scale).astype(q_ref.dtype)


def _mla_kv_kernel(c_ref, kr_ref, gckv_ref, w_ref, gkn_ref, gkr_ref, cs_ref, k_ref, v_ref, cn_s):
    @pl.when(pl.program_id(1) == 0)
    def _():
        cn_s[...] = _rms(c_ref[...], gckv_ref[...]).astype(_BF16)

    dk = k_ref.shape[-1]
    z = jnp.dot(cn_s[...], w_ref[...].astype(_BF16), preferred_element_type=_F32)
    kr = _rope_pair(kr_ref[...], cs_ref[...], gkr_ref[...])
    ones = jnp.ones((z.shape[0], LANES), _F32)
    for h in range(k_ref.shape[0]):
        zh = z[:, h * dk:(h + 1) * dk]
        kn = _rms(zh[:, :QK_NOPE], gkn_ref[...])
        k_ref[h] = jnp.concatenate([kn, kr], axis=1).astype(k_ref.dtype)
        v_ref[h] = jnp.concatenate([zh[:, QK_NOPE:], ones], axis=1).astype(v_ref.dtype)


def _mla_attn_kernel(q_ref, k_ref, v_ref, o_ref, m_s, acc_s):
    qi = pl.program_id(2)
    hb, tq, _ = q_ref.shape
    dv = v_ref.shape[-1]
    for h in range(hb):
        m_s[h] = jnp.full((tq, LANES), -jnp.inf, _F32)
        acc_s[h] = jnp.zeros((tq, dv), _F32)

    def block(h, start, diagonal):
        kb = k_ref[h, pl.ds(start, tq), :]
        vb = v_ref[h, pl.ds(start, tq), :]
        s = lax.dot_general(q_ref[h], kb, _NT, preferred_element_type=_F32)
        if diagonal:
            row = lax.broadcasted_iota(jnp.int32, s.shape, 0)
            col = lax.broadcasted_iota(jnp.int32, s.shape, 1)
            s = jnp.where(col <= row, s, -jnp.inf)
        m_prev = m_s[h]
        m_new = jnp.maximum(m_prev, jnp.max(s, axis=1, keepdims=True))
        p = jnp.exp2(s - jnp.concatenate([m_new] * (tq // LANES), axis=1))
        alpha = jnp.exp2(m_prev - m_new)
        acc_s[h] = jnp.concatenate([alpha] * (dv // LANES), axis=1) * acc_s[h] + jnp.dot(
            p.astype(_BF16), vb, preferred_element_type=_F32)
        m_s[h] = m_new

    def pair(j2, carry):
        for u in range(2):
            for h in range(hb):
                block(h, pl.multiple_of((2 * j2 + u) * tq, tq), False)
        return carry

    def single(j, carry):
        for h in range(hb):
            block(h, pl.multiple_of(j * tq, tq), False)
        return carry

    n_pairs = lax.div(qi, 2)
    lax.fori_loop(0, n_pairs, pair, 0)
    lax.fori_loop(2 * n_pairs, qi, single, 0)
    for h in range(hb):
        block(h, pl.multiple_of(qi * tq, tq), True)
    for h in range(hb):
        acc = acc_s[h]
        o_ref[:, h * V_HEAD:(h + 1) * V_HEAD] = (acc[:, :V_HEAD] / acc[:, V_HEAD:]).astype(o_ref.dtype)


def mla_attention(c, cs, w_q, w_kv, g_cq, g_ckv, g_qn, gv_q, g_kn, gv_k, B, S):
    T = c.shape[0]
    H, tm, hb = MLA_HEADS, ROW_TILE, MLA_HEAD_GROUP
    dq = QK_NOPE + 2 * QK_ROPE
    dv = 2 * V_HEAD
    scale = (QK_NOPE + QK_ROPE) ** -0.5 * LOG2_E
    nt = S // tm
    vec = lambda n: pl.BlockSpec((1, n), lambda i, h: (0, 0))
    nope, rope = (0, QK_NOPE), (QK_NOPE, QK_NOPE + QK_ROPE)
    ones_q = _group_ones(dq, [(nope, nope), (rope, (QK_NOPE, dq))])
    head_out = lambda d: pl.BlockSpec((None, hb, tm, d), lambda i, h: (i // nt, h, i % nt, 0))
    q = pl.pallas_call(
        functools.partial(_mla_q_kernel, scale=scale),
        out_shape=jax.ShapeDtypeStruct((B, H, S, dq), _BF16),
        grid=(T // tm, H // hb),
        in_specs=[pl.BlockSpec((tm, Q_LORA), lambda i, h: (i, 0)), vec(Q_LORA),
                  pl.BlockSpec((Q_LORA, hb * dq), lambda i, h: (0, h)), pl.BlockSpec((dq, dq), lambda i, h: (0, 0)),
                  vec(QK_NOPE), vec(LANES),
                  pl.BlockSpec((tm, LANES), lambda i, h: (i, 0))],
        out_specs=head_out(dq),
        scratch_shapes=[pltpu.VMEM((tm, Q_LORA), _BF16)],
        compiler_params=_params("parallel", "arbitrary"),
        name="mla_q_proj",
    )(c, g_cq.reshape(1, -1), w_q, ones_q, g_qn.reshape(1, -1), gv_q, cs)
    k, v = pl.pallas_call(
        _mla_kv_kernel,
        out_shape=[jax.ShapeDtypeStruct((B, H, S, dq), _BF16), jax.ShapeDtypeStruct((B, H, S, dv), _BF16)],
        grid=(T // tm, H // hb),
        in_specs=[pl.BlockSpec((tm, KV_LORA), lambda i, h: (i, 1)),
                  pl.BlockSpec((tm, LANES), lambda i, h: (i, (Q_LORA + KV_LORA) // LANES)),
                  vec(KV_LORA),
                  pl.BlockSpec((KV_LORA, hb * (QK_NOPE + V_HEAD)), lambda i, h: (0, h)),
                  vec(QK_NOPE), vec(LANES),
                  pl.BlockSpec((tm, LANES), lambda i, h: (i, 0))],
        out_specs=[head_out(dq), head_out(dv)],
        scratch_shapes=[pltpu.VMEM((tm, KV_LORA), _BF16)],
        compiler_params=_params("parallel", "arbitrary"),
        name="mla_kv_proj",
    )(c, c, g_ckv.reshape(1, -1), w_kv, g_kn.reshape(1, -1), gv_k, cs)
    tq = ATTN_TILE
    o = pl.pallas_call(
        _mla_attn_kernel,
        out_shape=jax.ShapeDtypeStruct((B, S, H * V_HEAD), _BF16),
        grid=(B, H // hb, S // tq),
        in_specs=[pl.BlockSpec((None, hb, tq, dq), lambda b, h, i: (b, h, i, 0)),
                  pl.BlockSpec((None, hb, S, dq), lambda b, h, i: (b, h, 0, 0)),
                  pl.BlockSpec((None, hb, S, dv), lambda b, h, i: (b, h, 0, 0))],
        out_specs=pl.BlockSpec((None, tq, hb * V_HEAD), lambda b, h, i: (b, i, h)),
        scratch_shapes=[pltpu.VMEM((hb, tq, LANES), _F32), pltpu.VMEM((hb, tq, dv), _F32)],
        compiler_params=_params("parallel", "parallel", "arbitrary"),
        name="mla_attention",
    )(q, k, v)
    return o.reshape(T, H * V_HEAD)


def _router_kernel(x_ref, g_ref, w_ref, r_ref, hn_ref):
    hn = _rms(x_ref[...], g_ref[...])
    hn_ref[...] = hn.astype(hn_ref.dtype)
    w = w_ref[...]
    h_hi = hn.astype(_BF16)
    h_lo = (hn - h_hi.astype(_F32)).astype(_BF16)
    w_hi = w.astype(_BF16)
    w_lo = (w - w_hi.astype(_F32)).astype(_BF16)
    logits = (jnp.dot(h_hi, w_hi, preferred_element_type=_F32) + jnp.dot(h_hi, w_lo, preferred_element_type=_F32)
              + jnp.dot(h_lo, w_hi, preferred_element_type=_F32))
    lane = lax.broadcasted_iota(jnp.int32, logits.shape, 1)
    lane_f = lane.astype(_F32)
    neg = -jnp.inf
    lg = jnp.where(lane < N_EXPERTS, logits, neg)
    v1 = jnp.max(lg, axis=1, keepdims=True)
    i1 = jnp.min(jnp.where(lg == v1, lane_f, float(LANES)), axis=1, keepdims=True)
    lg2 = jnp.where(lane_f == i1, neg, lg)
    v2 = jnp.max(lg2, axis=1, keepdims=True)
    i2 = jnp.min(jnp.where(lg2 == v2, lane_f, float(LANES)), axis=1, keepdims=True)
    e2 = jnp.exp(v2 - v1)
    w1 = 1.0 / (1.0 + e2)
    w2 = e2 / (1.0 + e2)
    out = jnp.where(lane < N_EXPERTS, logits, 0.0)
    out = jnp.where(lane == N_EXPERTS, i1, out)
    out = jnp.where(lane == N_EXPERTS + 1, i2, out)
    out = jnp.where(lane == N_EXPERTS + 2, w1, out)
    out = jnp.where(lane == N_EXPERTS + 3, w2, out)
    r_ref[...] = out


def router(x, g, w_router_padded):
    T, d = x.shape
    return pl.pallas_call(
        _router_kernel,
        out_shape=[jax.ShapeDtypeStruct((T, LANES), _F32), jax.ShapeDtypeStruct((T, d), _BF16)],
        grid=(T // ROW_TILE,),
        in_specs=[pl.BlockSpec((ROW_TILE, d), lambda i: (i, 0)),
                  pl.BlockSpec((1, d), lambda i: (0, 0)),
                  pl.BlockSpec((d, LANES), lambda i: (0, 0))],
        out_specs=[pl.BlockSpec((ROW_TILE, LANES), lambda i: (i, 0)), pl.BlockSpec((ROW_TILE, d), lambda i: (i, 0))],
        compiler_params=_params("parallel"),
        name="moe_router",
    )(x, g.reshape(1, d), w_router_padded)


def _row_copy(src_hbm, dst_vmem, sem, src_row, dst_row):
    return pltpu.make_async_copy(src_hbm.at[pl.ds(src_row, 1)], dst_vmem.at[pl.ds(dst_row, 1)], sem)


def _index_spec(n):
    return pl.BlockSpec((None, 1, n), lambda i: (i, 0, 0), memory_space=pltpu.SMEM)


def _gather_kernel(tile_ref, chunk_ref, flag_ref, pos_ref, x_ref, o_ref, acc):
    del chunk_ref
    s = pl.program_id(0)
    flags = flag_ref[s]
    tr, ch = acc.shape[0], x_ref.shape[0]

    @pl.when((flags & _G_VALID) != 0)
    def _():
        rows = tile_ref[s] * tr + lax.broadcasted_iota(jnp.int32, (tr, ch), 0)
        pos = pos_ref[...]
        hit = jnp.logical_or(rows == pos[0:1, :], rows == pos[1:2, :])
        onehot = jnp.where(hit, 1.0, 0.0).astype(_BF16)
        part = jnp.dot(onehot, x_ref[...], preferred_element_type=_F32)

        @pl.when((flags & _G_FIRST) != 0)
        def _():
            acc[...] = part

        @pl.when((flags & _G_FIRST) == 0)
        def _():
            acc[...] = acc[...] + part

        @pl.when((flags & _G_LAST) != 0)
        def _():
            o_ref[...] = acc[...].astype(o_ref.dtype)


_G_VALID, _G_FIRST, _G_LAST = 1, 2, 4


def gather_rows(x, pos, n_tiles, tr):
    T, d = x.shape
    r = n_tiles * tr
    ch = GATHER_CHUNK
    n_steps = N_EXPERTS * (T // ch) + n_tiles
    tile_ids = jnp.arange(n_tiles, dtype=jnp.int32)
    member = (pos.reshape(-1) // tr)[:, None] == tile_ids[None, :]
    token = (jnp.arange(T * TOP_K, dtype=jnp.int32) // TOP_K)[:, None]
    first_tok = jnp.min(jnp.where(member, token, T), axis=0)
    last_tok = jnp.max(jnp.where(member, token, -1), axis=0)
    lo = jnp.where(last_tok >= 0, first_tok, 0) // ch
    hi = jnp.maximum(last_tok, 0) // ch
    count = hi - lo + 1
    end = jnp.cumsum(count)
    steps = jnp.arange(n_steps, dtype=jnp.int32)
    valid = steps < end[-1]
    tile = jnp.minimum(jnp.sum((end[None, :] <= steps[:, None]).astype(jnp.int32), axis=1), n_tiles - 1)
    tile_oh = (tile[:, None] == jnp.arange(n_tiles, dtype=jnp.int32)[None, :]).astype(jnp.int32)
    pick = lambda v: jnp.sum(tile_oh * v[None, :], axis=1)
    start = pick(end - count)
    chunk = jnp.where(valid, pick(lo) + steps - start, hi[-1])
    flags = (valid * _G_VALID + jnp.logical_and(valid, steps == start) * _G_FIRST
             + jnp.logical_and(valid, steps == pick(end) - 1) * _G_LAST)
    return pl.pallas_call(
        _gather_kernel,
        out_shape=jax.ShapeDtypeStruct((r, d), x.dtype),
        grid_spec=pltpu.PrefetchScalarGridSpec(
            num_scalar_prefetch=3,
            grid=(n_steps,),
            in_specs=[pl.BlockSpec((TOP_K, ch), lambda s, tile, chunk, flag: (0, chunk[s])),
                      pl.BlockSpec((ch, d), lambda s, tile, chunk, flag: (chunk[s], 0))],
            out_specs=pl.BlockSpec((tr, d), lambda s, tile, chunk, flag: (tile[s], 0)),
            scratch_shapes=[pltpu.VMEM((tr, d), _F32)],
        ),
        compiler_params=_params("arbitrary"),
        name="moe_gather",
    )(tile.astype(jnp.int32), chunk.astype(jnp.int32), flags.astype(jnp.int32), pos.T, x)


def _combine_kernel(pos_ref, x_ref, w_ref, y_hbm, o_ref, buf_a, buf_b, sem):
    tt = buf_a.shape[0]

    def start(r, carry):
        _row_copy(y_hbm, buf_a, sem, pos_ref[0, TOP_K * r], r).start()
        _row_copy(y_hbm, buf_b, sem, pos_ref[0, TOP_K * r + 1], r).start()
        return carry

    def wait(r, carry):
        _row_copy(y_hbm, buf_a, sem, 0, r).wait()
        _row_copy(y_hbm, buf_b, sem, 0, r).wait()
        return carry

    lax.fori_loop(0, tt, start, 0)
    lax.fori_loop(0, tt, wait, 0)
    w = w_ref[...]
    o_ref[...] = x_ref[...] + w[:, 0:1] * buf_a[...] + w[:, 1:2] * buf_b[...]


def moe_combine(x, y, pos_flat, weights, tt):
    T, d = x.shape
    return pl.pallas_call(
        _combine_kernel,
        out_shape=jax.ShapeDtypeStruct((T, d), _F32),
        grid=(T // tt,),
        in_specs=[_index_spec(tt * TOP_K), pl.BlockSpec((tt, d), lambda i: (i, 0)),
                  pl.BlockSpec((tt, TOP_K), lambda i: (i, 0)), pl.BlockSpec(memory_space=pl.ANY)],
        out_specs=pl.BlockSpec((tt, d), lambda i: (i, 0)),
        scratch_shapes=[pltpu.VMEM((tt, d), _F32), pltpu.VMEM((tt, d), _F32), pltpu.SemaphoreType.DMA],
        compiler_params=_params("arbitrary"),
        name="moe_combine",
    )(pos_flat.reshape(T // tt, 1, tt * TOP_K), x, weights, y)


def _moe_plan(route, tm, n_tiles):
    T = route.shape[0]
    e_flat = route[:, N_EXPERTS:N_EXPERTS + TOP_K].astype(jnp.int32).reshape(-1)
    onehot = (e_flat[:, None] == jnp.arange(N_EXPERTS, dtype=jnp.int32)[None, :]).astype(jnp.int32)
    csum = jnp.cumsum(onehot, axis=0)
    rank = jnp.sum(csum * onehot, axis=1) - 1
    tiles_e = (csum[-1] + tm - 1) // tm
    tile_end = jnp.cumsum(tiles_e)
    tile_start = tile_end - tiles_e
    pos = jnp.sum(onehot * tile_start[None, :], axis=1) * tm + rank
    tile_ids = jnp.arange(n_tiles, dtype=jnp.int32)
    tile_expert = jnp.minimum(jnp.sum((tile_ids[:, None] >= tile_end[None, :]).astype(jnp.int32), axis=1),
                              N_EXPERTS - 1).astype(jnp.int32)
    n_used = tile_end[-1:].astype(jnp.int32)
    experts = jnp.arange(N_EXPERTS, dtype=jnp.int32)
    has = tiles_e > 0
    later = jnp.where(jnp.logical_and(has[None, :], experts[None, :] > experts[:, None]), experts[None, :], N_EXPERTS)
    next_e = jnp.min(later, axis=1)
    last_e = next_e == N_EXPERTS
    next_e = jnp.where(last_e, jnp.min(jnp.where(has, experts, N_EXPERTS)), next_e)
    group_e = jnp.cumsum(has.astype(jnp.int32)) - 1
    tile_onehot = (tile_expert[:, None] == experts[None, :]).astype(jnp.int32)

    def of_tile(per_expert):
        return jnp.sum(tile_onehot * per_expert.astype(jnp.int32)[None, :], axis=1)

    index = tile_ids - of_tile(tile_start)
    groups = (tile_expert, (index == 0).astype(jnp.int32), index, jnp.maximum(of_tile(tiles_e), 1),
              of_tile(next_e), of_tile(last_e), of_tile(group_e),
              jnp.full((n_tiles,), jnp.sum(has.astype(jnp.int32)), jnp.int32))
    return groups, n_used, pos.astype(jnp.int32).reshape(T, TOP_K)


def moe_layer(x, g, w_router, we_gate, we_up, we_down):
    T, d = x.shape
    tm = MOE_ROW_TILE
    n_tiles = (T * TOP_K) // tm + N_EXPERTS
    w_r = jnp.zeros((d, LANES), _F32).at[:, :N_EXPERTS].set(w_router)
    route, hn = router(x, g, w_r)
    plan, n_used, pos = _moe_plan(route, tm, n_tiles)
    xs = gather_rows(hn, pos, n_tiles, tm)
    h = streamed_matmul(xs, [we_gate, we_up], plan, n_used, tm=tm, tn=UP_COLS, n_chunks=UP_CHUNKS, out_dtype=_BF16)
    y = streamed_matmul(h, [we_down], plan, n_used, tm=tm, tn=DOWN_COLS, n_chunks=DOWN_CHUNKS, out_dtype=_F32)
    weights = route[:, N_EXPERTS + TOP_K:N_EXPERTS + 2 * TOP_K]
    return moe_combine(x, y, pos.reshape(-1), weights, tm)


def _rope_tables(positions, half):
    inv = ROPE_BASE ** (-jnp.arange(half, dtype=_F32) / half)
    ang = positions.astype(_F32)[..., None] * inv
    return jnp.cos(ang), jnp.sin(ang)


def _rotate_half_cols(w, width):
    lead = w.shape[:-1]
    w2 = w.reshape(lead + (-1, 2, width // 2))
    return jnp.stack([-w2[..., 1, :], w2[..., 0, :]], axis=-2).reshape(w.shape)


def _swap_halves(g):
    half = g.shape[-1] // 2
    return jnp.concatenate([g[..., half:], g[..., :half]], axis=-1)


def kernel(x, positions, g_mix_norm, g_ffn_norm, w_in, conv_w, b_gates, g_mlstm_out, g_ret_out, w_mix_out,
           ffn_gate, ffn_up, ffn_down, w_dqkv, g_cq, g_ckv, w_uq, w_ukv, g_qn, g_qr, g_kn, g_kr, w_o,
           w_router, we_gate, we_up, we_down):
    B, S, D = x.shape
    T = B * S
    xf = x.reshape(T, D)
    n_gate = 2 * MLSTM_HEADS
    main = 4 * MLSTM_HEADS * HEAD_DIM

    w = w_in[0]
    w_gate = jnp.zeros((D, LANES), _F32).at[:, :n_gate].set(w[:, main:main + n_gate])
    g0 = g_mix_norm[0]
    dense = dict(tm=DENSE_ROW_TILE, vmem_limit_bytes=DENSE_VMEM_LIMIT_BYTES)
    plan, nu = _dense_stream_plan(T, DENSE_ROW_TILE)
    proj_m = streamed_matmul(xf, [w_in], plan, nu, tn=PROJ_COLS, n_chunks=UP_CHUNKS, out_dtype=_F32,
                             gain=g0, n_out=main, **dense)
    proj_r = streamed_matmul(xf, [w[None, :, main + n_gate:]], plan, nu, tn=PROJ_COLS,
                             n_chunks=UP_CHUNKS, out_dtype=_F32, gain=g0, **dense)
    gates = norm_matmul(xf, g0, w_gate[None], tn=LANES)[:, :n_gate] + b_gates[0][None, :]
    cos_r, sin_r = _rope_tables(positions, HEAD_DIM // 2)
    log_gamma = jnp.log1p(-jnp.exp2(-5.0 - jnp.arange(RET_HEADS, dtype=_F32)))
    mix = recurrent_mixer(proj_m.reshape(B, S, -1), proj_r.reshape(B, S, -1), gates.reshape(B, S, n_gate),
                          cos_r, sin_r, conv_w[0], g_mlstm_out[0], g_ret_out[0], log_gamma).reshape(T, -1)
    w_mix = w_mix_out[0].reshape(2, MLSTM_HEADS, HEAD_DIM, D).transpose(1, 0, 2, 3).reshape(-1, D)
    xf = dense_matmul(mix, w_mix, tn=1024, out_dtype=_F32, residual=xf)

    plan_up, nu_up = _dense_stream_plan(T, ROW_TILE)
    hmid = streamed_matmul(xf, [ffn_gate, ffn_up], plan_up, nu_up, tm=ROW_TILE, tn=UP_COLS, n_chunks=UP_CHUNKS,
                           out_dtype=_BF16, gain=g_ffn_norm[0])
    xf = streamed_matmul(hmid, [ffn_down], plan, nu, tn=DENSE_DOWN_COLS, n_chunks=DOWN_CHUNKS,
                         out_dtype=_F32, residual=xf, **dense)

    H = MLA_HEADS
    wd = w_dqkv[0]
    w_kr = wd[:, Q_LORA + KV_LORA:]
    wd_full = jnp.concatenate([wd, _rotate_half_cols(w_kr, QK_ROPE)], axis=1)
    c = norm_matmul(xf, g_mix_norm[1], wd_full[None], tn=wd_full.shape[1] // 3)
    wq = w_uq[0].reshape(Q_LORA, H, QK_NOPE + QK_ROPE)
    wq_r = wq[..., QK_NOPE:]
    w_q = jnp.concatenate([wq, _rotate_half_cols(wq_r, QK_ROPE)], axis=-1).reshape(Q_LORA, -1)
    cos_m, sin_m = _rope_tables(positions, QK_ROPE // 2)
    cs = jnp.concatenate([cos_m, cos_m, sin_m, sin_m], axis=-1).reshape(T, LANES)
    gv_q = jnp.concatenate([g_qr[0], _swap_halves(g_qr[0])]).reshape(1, LANES)
    gv_k = jnp.concatenate([g_kr[0], _swap_halves(g_kr[0])]).reshape(1, LANES)
    attn = mla_attention(c, cs, w_q, w_ukv[0], g_cq[0], g_ckv[0], g_qn[0], gv_q, g_kn[0], gv_k, B, S)
    xf = dense_matmul(attn, w_o[0], tn=1024, out_dtype=_F32, residual=xf)

    xf = moe_layer(xf, g_ffn_norm[1], w_router[0], we_gate[0], we_up[0], we_down[0])
    return xf.reshape(B, S, D)
```

```python
import functools

import jax
import jax.numpy as jnp
import numpy as np
from jax import lax
from jax.experimental import pallas as pl
from jax.experimental.pallas import tpu as pltpu

EPS = 1e-6
ROPE_BASE = 10000.0
CONV_WIDTH = 4
MLSTM_HEADS = 4
RET_HEADS = 4
HEAD_DIM = 256
REC_CHUNK = 256
MLA_HEADS = 16
Q_LORA = 512
KV_LORA = 512
QK_NOPE = 128
QK_ROPE = 64
V_HEAD = 128
N_EXPERTS = 8
TOP_K = 2

LANES = 128
SUBLANES = 8
VMEM_LIMIT_BYTES = 48 * 1024 * 1024

ROW_TILE = 512
DENSE_ROW_TILE = 1024
DENSE_VMEM_LIMIT_BYTES = 58 * 1024 * 1024
NORM_ROW_TILE = 1024
NORM_VMEM_LIMIT_BYTES = 56 * 1024 * 1024
MOE_ROW_TILE = 256
UP_COLS = 1408
UP_CHUNKS = 8
PROJ_COLS = 2048
DOWN_COLS = 1024
DENSE_DOWN_COLS = 512
DOWN_CHUNKS = 11
ATTN_TILE = 512
MLA_HEAD_GROUP = 4
GATHER_CHUNK = 512
LOG2_E = 1.4426950408889634

_F32 = jnp.float32
_BF16 = jnp.bfloat16
_NT = (((1,), (1,)), ((), ()))
_TN = (((0,), (0,)), ((), ()))


def _params(*semantics, vmem_limit_bytes=VMEM_LIMIT_BYTES):
    return pltpu.CompilerParams(dimension_semantics=semantics, vmem_limit_bytes=vmem_limit_bytes)


def _sigmoid(x):
    return 1.0 / (1.0 + jnp.exp(-x))


def _rms(x, g):
    return x * lax.rsqrt(jnp.mean(x * x, axis=-1, keepdims=True) + EPS) * g


def _gmm_kernel(te_ref, nu_ref, a_ref, w_ref, *rest):
    del te_ref
    o_ref = rest[-1]

    @pl.when(pl.program_id(1) < nu_ref[0])
    def _():
        acc = jnp.dot(a_ref[...], w_ref[...].astype(_BF16), preferred_element_type=_F32)
        if len(rest) == 2:
            acc = acc + rest[0][...]
        o_ref[...] = acc.astype(o_ref.dtype)

    @pl.when(pl.program_id(1) >= nu_ref[0])
    def _():
        o_ref[...] = jnp.zeros_like(o_ref)


def grouped_matmul(a, w, tile_expert, n_used, *, tm, tn, out_dtype, residual=None):
    m, k = a.shape
    n = w.shape[2]
    n_tiles = m // tm

    def row(j, i, te, nu):
        return jnp.minimum(i, nu[0] - 1)

    in_specs = [pl.BlockSpec((tm, k), lambda j, i, te, nu: (row(j, i, te, nu), 0)),
                pl.BlockSpec((None, k, tn), lambda j, i, te, nu: (te[row(j, i, te, nu)], 0, j))]
    args = [a, w]
    if residual is not None:
        in_specs.append(pl.BlockSpec((tm, tn), lambda j, i, te, nu: (row(j, i, te, nu), j)))
        args.append(residual)
    return pl.pallas_call(
        _gmm_kernel,
        out_shape=jax.ShapeDtypeStruct((m, n), out_dtype),
        grid_spec=pltpu.PrefetchScalarGridSpec(
            num_scalar_prefetch=2,
            grid=(n // tn, n_tiles),
            in_specs=in_specs,
            out_specs=pl.BlockSpec((tm, tn), lambda j, i, te, nu: (i, j)),
        ),
        compiler_params=_params("arbitrary", "arbitrary"),
        name="grouped_matmul",
    )(tile_expert, n_used, *args)


_P_EXPERT, _P_FIRST, _P_LO, _P_HI, _P_NEXT, _P_LAST, _P_GROUP, _P_NGROUPS = range(8)


def _streamed_kernel(plan_ref, nu_ref, a_ref, *refs, n_mats, has_gain, has_res, n_col_tiles):
    w_hbm = refs[:n_mats]
    gain_ref = refs[n_mats] if has_gain else None
    res_ref = refs[n_mats + int(has_gain)] if has_res else None
    o_ref, wbf, stage, sem = refs[n_mats + int(has_gain) + int(has_res):]
    j = pl.program_id(0)
    i = pl.program_id(1)
    _, _, k, tn = wbf.shape
    kc = stage.shape[2]
    n_chunks = k // kc

    def chunk_copy(e, jj, c, m):
        src = w_hbm[m].at[e, pl.ds(pl.multiple_of(c * kc, kc), kc), pl.ds(pl.multiple_of(jj * tn, LANES), tn)]
        return pltpu.make_async_copy(src, stage.at[c & 1, m], sem.at[c & 1, m])

    def prime(e, jj):
        for c in range(2):
            for m in range(n_mats):
                chunk_copy(e, jj, c, m).start()

    def fetch(e, jj, slot, lo, hi):
        def body(c, carry):
            for m in range(n_mats):
                chunk_copy(e, jj, c, m).wait()
                wbf[slot, m, pl.ds(pl.multiple_of(c * kc, kc), kc), :] = stage[c & 1, m].astype(_BF16)

                @pl.when(c + 2 < n_chunks)
                def _():
                    chunk_copy(e, jj, c + 2, m).start()
            return carry

        lax.fori_loop(lo, hi, body, 0)

    expert = plan_ref[_P_EXPERT, i]
    used = i < nu_ref[0]
    cur = (j * plan_ref[_P_NGROUPS, 0] + plan_ref[_P_GROUP, i]) & 1
    in_last_group = plan_ref[_P_LAST, i] == 1
    has_next = jnp.logical_and(used, jnp.logical_not(jnp.logical_and(in_last_group, j == n_col_tiles - 1)))
    next_e = plan_ref[_P_NEXT, i]
    next_j = jnp.where(in_last_group, j + 1, j)

    @pl.when(jnp.logical_and(j == 0, i == 0))
    def _():
        prime(expert, 0)
        fetch(expert, 0, 0, 0, n_chunks)

    @pl.when(jnp.logical_and(has_next, plan_ref[_P_FIRST, i] == 1))
    def _():
        prime(next_e, next_j)

    @pl.when(used)
    def _():
        a = a_ref[...]
        if has_gain:
            a = _rms(a, gain_ref[...]).astype(_BF16)
        if n_mats == 2:
            g = jnp.dot(a, wbf[cur, 0], preferred_element_type=_F32)
            u = jnp.dot(a, wbf[cur, 1], preferred_element_type=_F32)
            out = g * _sigmoid(g) * u
        else:
            out = jnp.dot(a, wbf[cur, 0], preferred_element_type=_F32)
            if has_res:
                out = out + res_ref[...]
        o_ref[...] = out.astype(o_ref.dtype)

    @pl.when(has_next)
    def _():
        fetch(next_e, next_j, 1 - cur, plan_ref[_P_LO, i], plan_ref[_P_HI, i])

    @pl.when(jnp.logical_not(used))
    def _():
        o_ref[...] = jnp.zeros_like(o_ref)


def _chunk_shares(index, size, n_chunks):
    return (index * n_chunks) // size, ((index + 1) * n_chunks) // size


def streamed_matmul(a, weights, groups, n_used, *, tm, tn, n_chunks, out_dtype, gain=None, residual=None,
                    n_out=None, vmem_limit_bytes=VMEM_LIMIT_BYTES):
    expert, first, index, size, nxt, last, group, n_groups = groups
    lo, hi = _chunk_shares(index, size, n_chunks)
    plan = jnp.stack([expert, first, lo, hi, nxt, last, group, n_groups]).astype(jnp.int32)
    m, k = a.shape
    n = weights[0].shape[2] if n_out is None else n_out
    n_mats = len(weights)

    def row(j, i, plan, nu):
        return jnp.minimum(i, nu[0] - 1)

    in_specs = [pl.BlockSpec((tm, k), lambda j, i, plan, nu: (row(j, i, plan, nu), 0))]
    in_specs += [pl.BlockSpec(memory_space=pl.ANY)] * n_mats
    args = [a, *weights]
    if gain is not None:
        in_specs.append(pl.BlockSpec((1, k), lambda j, i, plan, nu: (0, 0)))
        args.append(gain.reshape(1, k))
    if residual is not None:
        in_specs.append(pl.BlockSpec((tm, tn), lambda j, i, plan, nu: (row(j, i, plan, nu), j)))
        args.append(residual)
    return pl.pallas_call(
        functools.partial(_streamed_kernel, n_mats=n_mats, has_gain=gain is not None, has_res=residual is not None,
                          n_col_tiles=n // tn),
        out_shape=jax.ShapeDtypeStruct((m, n), out_dtype),
        grid_spec=pltpu.PrefetchScalarGridSpec(
            num_scalar_prefetch=2,
            grid=(n // tn, m // tm),
            in_specs=in_specs,
            out_specs=pl.BlockSpec((tm, tn), lambda j, i, plan, nu: (i, j)),
            scratch_shapes=[pltpu.VMEM((2, n_mats, k, tn), _BF16),
                            pltpu.VMEM((2, n_mats, k // n_chunks, tn), _F32),
                            pltpu.SemaphoreType.DMA((2, n_mats))],
        ),
        compiler_params=_params("arbitrary", "arbitrary", vmem_limit_bytes=vmem_limit_bytes),
        name="streamed_swiglu_up" if n_mats == 2 else "streamed_matmul",
    )(plan, n_used, *args)


def _dense_stream_plan(m, tm):
    n_tiles = m // tm
    ids = jnp.arange(n_tiles, dtype=jnp.int32)
    zero = jnp.zeros_like(ids)
    one = jnp.ones_like(ids)
    groups = (zero, (ids == 0).astype(jnp.int32), ids, one * n_tiles, zero, one, zero, one)
    return groups, jnp.full((1,), n_tiles, jnp.int32)


def _dense_plan(m, tm):
    n_tiles = m // tm
    return jnp.zeros((n_tiles,), jnp.int32), jnp.full((1,), n_tiles, jnp.int32)


def dense_matmul(a, w, *, tn, out_dtype, residual=None):
    te, nu = _dense_plan(a.shape[0], DENSE_ROW_TILE)
    return grouped_matmul(a, w[None], te, nu, tm=DENSE_ROW_TILE, tn=tn, out_dtype=out_dtype, residual=residual)


def _norm_matmul_kernel(x_ref, g_ref, w_ref, o_ref, xn_s):
    @pl.when(pl.program_id(1) == 0)
    def _():
        xn_s[...] = _rms(x_ref[...], g_ref[...]).astype(_BF16)

    o_ref[...] = jnp.dot(xn_s[...], w_ref[...].astype(_BF16), preferred_element_type=_F32)


def norm_matmul(x, gain, w, *, tn):
    m, k = x.shape
    n = w.shape[2]
    tm = NORM_ROW_TILE
    return pl.pallas_call(
        _norm_matmul_kernel,
        out_shape=jax.ShapeDtypeStruct((m, n), _F32),
        grid=(m // tm, n // tn),
        in_specs=[pl.BlockSpec((tm, k), lambda i, j: (i, 0)),
                  pl.BlockSpec((1, k), lambda i, j: (0, 0)),
                  pl.BlockSpec((None, k, tn), lambda i, j: (0, 0, j))],
        out_specs=pl.BlockSpec((tm, tn), lambda i, j: (i, j)),
        scratch_shapes=[pltpu.VMEM((tm, k), _BF16)],
        compiler_params=pltpu.CompilerParams(dimension_semantics=("parallel", "arbitrary"),
                                             vmem_limit_bytes=NORM_VMEM_LIMIT_BYTES),
        name="norm_matmul",
    )(x, gain.reshape(1, k), w)


def _recurrent_kernel(q_ref, k_ref, v_ref, o_ref, rq_ref, rk_ref, rv_ref, rg_ref,
                      gt_ref, gc_ref, cos_ref, sin_ref, cwq_ref, cwk_ref, gm_ref, gr_ref, lg_ref,
                      mix_ref,
                      c_s, n_s, m_s, r_s, qbuf, kbuf):
    L, dh = q_ref.shape
    halo = SUBLANES
    inv_sqrt_d = dh ** -0.5

    @pl.when(pl.program_id(2) == 0)
    def _():
        c_s[...] = jnp.zeros_like(c_s)
        n_s[...] = jnp.zeros_like(n_s)
        m_s[...] = jnp.full_like(m_s, -jnp.inf)
        r_s[...] = jnp.zeros_like(r_s)
        qbuf[0:halo, :] = jnp.zeros((halo, dh), _F32)
        kbuf[0:halo, :] = jnp.zeros((halo, dh), _F32)

    def conv_silu(x_ref, buf, w_ref):
        buf[halo:halo + L, :] = x_ref[...]
        w = w_ref[...]
        y = w[CONV_WIDTH - 1:CONV_WIDTH, :] * buf[halo:halo + L, :]
        for j in range(CONV_WIDTH - 1):
            off = halo - (CONV_WIDTH - 1) + j
            y = y + w[j:j + 1, :] * buf[off:off + L, :]
        buf[0:halo, :] = buf[L:L + halo, :]
        return y * _sigmoid(y)

    row = lax.broadcasted_iota(jnp.int32, (L, L), 0)
    col = lax.broadcasted_iota(jnp.int32, (L, L), 1)
    causal = col <= row

    q = conv_silu(q_ref, qbuf, cwq_ref)
    k = conv_silu(k_ref, kbuf, cwk_ref)
    qb = q.astype(_BF16)
    vb = v_ref[...].astype(_BF16)

    def log_sigmoid(x):
        return jnp.minimum(x, 0.0) - jnp.log1p(jnp.exp(-jnp.abs(x)))

    gt = gt_ref[...]
    gc = gc_ref[...]
    i_row = gt[0:1, :]
    f_row = log_sigmoid(gt[1:2, :])
    i_col = gc[:, 0:1]
    f_col = log_sigmoid(gc[:, 1:2])
    hi = lax.Precision.HIGHEST
    b_col = jnp.dot(causal.astype(_F32), f_col, precision=hi, preferred_element_type=_F32)
    b_row = jnp.dot(f_row, (row <= col).astype(_F32), precision=hi, preferred_element_type=_F32)
    g_tot = b_col[L - 1:L, :]
    m_prev = m_s[...]

    log_d = jnp.where(causal, b_col - b_row + i_row, -jnp.inf)
    m_inter = b_col + m_prev
    m_t = jnp.maximum(jnp.max(log_d, axis=1, keepdims=True), m_inter)
    d_m = jnp.exp(log_d - m_t)
    inter = jnp.exp(m_inter - m_t)
    s = lax.dot_general(qb, k.astype(_BF16), _NT, preferred_element_type=_F32) * inv_sqrt_d
    s_m = s * d_m
    num = (jnp.dot(s_m.astype(_BF16), vb, preferred_element_type=_F32)
           + inter * jnp.dot(qb, c_s[...].astype(_BF16), preferred_element_type=_F32))
    den = jnp.sum(s_m, axis=1, keepdims=True) + inter * jnp.sum(q * n_s[...], axis=1, keepdims=True)
    h = num / jnp.maximum(jnp.abs(den), jnp.exp(-m_t))

    log_w = g_tot - b_col + i_col
    m_new = jnp.maximum(g_tot + m_prev, jnp.max(log_w, axis=0, keepdims=True))
    w_col = jnp.exp(log_w - m_new)
    decay = jnp.exp(g_tot + m_prev - m_new)
    kw = k * (w_col * inv_sqrt_d)
    c_s[...] = decay * c_s[...] + lax.dot_general(kw.astype(_BF16), vb, _TN, preferred_element_type=_F32)
    n_s[...] = decay * n_s[...] + jnp.sum(kw, axis=0, keepdims=True)
    m_s[...] = m_new

    mix_ref[:, :dh] = _rms(_sigmoid(o_ref[...]) * h, gm_ref[...]).astype(mix_ref.dtype)

    half = dh // 2
    cos = cos_ref[...]
    sin = sin_ref[...]

    def rope(x):
        x1 = x[:, :half]
        x2 = x[:, half:]
        return jnp.concatenate([x1 * cos - x2 * sin, x2 * cos + x1 * sin], axis=1)

    rq = rope(rq_ref[...]).astype(_BF16)
    rk = rope(rk_ref[...])
    rvb = rv_ref[...].astype(_BF16)
    lg = lg_ref[...]
    dist = (row - col).astype(_F32)
    d_r = jnp.where(causal, jnp.exp(lg * jnp.maximum(dist, 0.0)), 0.0)
    t_col = lax.broadcasted_iota(jnp.int32, (L, 1), 0).astype(_F32)
    q_decay = jnp.exp(lg * (t_col + 1.0))
    k_decay = jnp.exp(lg * (L - 1.0 - t_col))
    chunk_decay = jnp.exp(lg * L)
    s_r = lax.dot_general(rq, rk.astype(_BF16), _NT, preferred_element_type=_F32) * inv_sqrt_d * d_r
    out_r = (jnp.dot(s_r.astype(_BF16), rvb, preferred_element_type=_F32)
             + jnp.dot(rq, r_s[...].astype(_BF16), preferred_element_type=_F32) * q_decay)
    r_s[...] = chunk_decay * r_s[...] + lax.dot_general(
        (rk * (k_decay * inv_sqrt_d)).astype(_BF16), rvb, _TN, preferred_element_type=_F32)
    rg = rg_ref[...]
    mix_ref[:, dh:] = (rg * _sigmoid(rg) * _rms(out_r, gr_ref[...])).astype(mix_ref.dtype)


def recurrent_mixer(proj_m, proj_r, gates, cos, sin, conv_w, g_mlstm_out, g_ret_out, log_gamma):
    B, S, _ = proj_m.shape
    H, dh, L = MLSTM_HEADS, HEAD_DIM, REC_CHUNK
    g4 = gates.reshape(B, S, 2, H)
    gt = g4.transpose(0, 3, 2, 1)
    gc = g4.transpose(0, 3, 1, 2)

    def pspec(group):
        return pl.BlockSpec((None, L, dh), lambda b, h, c: (b, c, group * H + h))

    in_specs = [pspec(g) for g in range(4)] * 2 + [
        pl.BlockSpec((None, None, 2, L), lambda b, h, c: (b, h, 0, c)),
        pl.BlockSpec((None, None, L, 2), lambda b, h, c: (b, h, c, 0)),
        pl.BlockSpec((None, L, dh // 2), lambda b, h, c: (b, c, 0)),
        pl.BlockSpec((None, L, dh // 2), lambda b, h, c: (b, c, 0)),
        pl.BlockSpec((CONV_WIDTH, dh), lambda b, h, c: (0, h)),
        pl.BlockSpec((CONV_WIDTH, dh), lambda b, h, c: (0, H + h)),
        pl.BlockSpec((1, dh), lambda b, h, c: (0, h)),
        pl.BlockSpec((1, dh), lambda b, h, c: (0, h)),
        pl.BlockSpec((None, 1, 1), lambda b, h, c: (h, 0, 0)),
    ]
    return pl.pallas_call(
        _recurrent_kernel,
        out_shape=jax.ShapeDtypeStruct((B, S, 2 * H * dh), _BF16),
        grid=(B, H, S // L),
        in_specs=in_specs,
        out_specs=pl.BlockSpec((None, L, 2 * dh), lambda b, h, c: (b, c, h)),
        scratch_shapes=[pltpu.VMEM((dh, dh), _F32), pltpu.VMEM((1, dh), _F32), pltpu.VMEM((1, 1), _F32),
                        pltpu.VMEM((dh, dh), _F32),
                        pltpu.VMEM((L + SUBLANES, dh), _F32), pltpu.VMEM((L + SUBLANES, dh), _F32)],
        compiler_params=_params("parallel", "parallel", "arbitrary"),
        name="recurrent_mixer",
    )(*([proj_m] * 4), *([proj_r] * 4), gt, gc, cos, sin, conv_w, conv_w,
      g_mlstm_out.reshape(1, H * dh), g_ret_out.reshape(1, H * dh), log_gamma.reshape(H, 1, 1))


def _rope_pair(z2, cs, gvec, sumsq=None):
    lane = lax.broadcasted_iota(jnp.int32, z2.shape, 1)
    first = lane < QK_ROPE
    if sumsq is None:
        sumsq = jnp.sum(jnp.where(first, z2 * z2, 0.0), axis=1, keepdims=True)
    t = z2 * lax.rsqrt(sumsq * (1.0 / QK_ROPE) + EPS) * (cs * gvec)
    return jnp.where(first, t + pltpu.roll(t, QK_ROPE, 1), 0.0)


def _group_sumsq(z, ones_ref):
    return jnp.dot((z * z).astype(_BF16), ones_ref[...], preferred_element_type=_F32)


def _group_ones(width, pairs):
    m = np.zeros((width, width), np.float32)
    for (k0, k1), (n0, n1) in pairs:
        m[k0:k1, n0:n1] = 1.0
    return jnp.asarray(m, _BF16)


def _mla_q_kernel(c_ref, gcq_ref, w_ref, ones_ref, gqn_ref, gqr_ref, cs_ref, q_ref, cn_s, *, scale):
    @pl.when(pl.program_id(1) == 0)
    def _():
        cn_s[...] = _rms(c_ref[...], gcq_ref[...]).astype(_BF16)

    dq = q_ref.shape[-1]
    z = jnp.dot(cn_s[...], w_ref[...].astype(_BF16), preferred_element_type=_F32)
    cs = cs_ref[...]
    for h in range(q_ref.shape[0]):
        zh = z[:, h * dq:(h + 1) * dq]
        ssh = _group_sumsq(zh, ones_ref)
        qn = zh[:, :QK_NOPE] * lax.rsqrt(ssh[:, :QK_NOPE] * (1.0 / QK_NOPE) + EPS) * gqn_ref[...]
        qr = _rope_pair(zh[:, QK_NOPE:], cs, gqr_ref[...], ssh[:, QK_NOPE:])
        q_ref[h] = (jnp.concatenate([qn, qr], axis=1) * scale).astype(q_ref.dtype)


def _mla_kv_kernel(c_ref, kr_ref, gckv_ref, w_ref, gkn_ref, gkr_ref, cs_ref, k_ref, v_ref, cn_s):
    @pl.when(pl.program_id(1) == 0)
    def _():
        cn_s[...] = _rms(c_ref[...], gckv_ref[...]).astype(_BF16)

    dk = k_ref.shape[-1]
    z = jnp.dot(cn_s[...], w_ref[...].astype(_BF16), preferred_element_type=_F32)
    kr = _rope_pair(kr_ref[...], cs_ref[...], gkr_ref[...])
    ones = jnp.ones((z.shape[0], LANES), _F32)
    for h in range(k_ref.shape[0]):
        zh = z[:, h * dk:(h + 1) * dk]
        kn = _rms(zh[:, :QK_NOPE], gkn_ref[...])
        k_ref[h] = jnp.concatenate([kn, kr], axis=1).astype(k_ref.dtype)
        v_ref[h] = jnp.concatenate([zh[:, QK_NOPE:], ones], axis=1).astype(v_ref.dtype)


def _mla_attn_kernel(q_ref, k_ref, v_ref, o_ref, m_s, acc_s):
    qi = pl.program_id(2)
    hb, tq, _ = q_ref.shape
    dv = v_ref.shape[-1]
    for h in range(hb):
        m_s[h] = jnp.full((tq, LANES), -jnp.inf, _F32)
        acc_s[h] = jnp.zeros((tq, dv), _F32)

    def block(h, start, diagonal):
        kb = k_ref[h, pl.ds(start, tq), :]
        vb = v_ref[h, pl.ds(start, tq), :]
        s = lax.dot_general(q_ref[h], kb, _NT, preferred_element_type=_F32)
        if diagonal:
            row = lax.broadcasted_iota(jnp.int32, s.shape, 0)
            col = lax.broadcasted_iota(jnp.int32, s.shape, 1)
            s = jnp.where(col <= row, s, -jnp.inf)
        m_prev = m_s[h]
        m_new = jnp.maximum(m_prev, jnp.max(s, axis=1, keepdims=True))
        p = jnp.exp2(s - jnp.concatenate([m_new] * (tq // LANES), axis=1))
        alpha = jnp.exp2(m_prev - m_new)
        acc_s[h] = jnp.concatenate([alpha] * (dv // LANES), axis=1) * acc_s[h] + jnp.dot(
            p.astype(_BF16), vb, preferred_element_type=_F32)
        m_s[h] = m_new

    def pair(j2, carry):
        for u in range(2):
            for h in range(hb):
                block(h, pl.multiple_of((2 * j2 + u) * tq, tq), False)
        return carry

    n_pairs = lax.div(qi, 2)
    lax.fori_loop(0, n_pairs, pair, 0)
    odd = qi - 2 * n_pairs

    @pl.when(odd == 1)
    def _():
        for h in range(hb):
            block(h, pl.multiple_of((qi - 1) * tq, tq), False)
        for h in range(hb):
            block(h, pl.multiple_of(qi * tq, tq), True)

    @pl.when(odd == 0)
    def _():
        for h in range(hb):
            block(h, pl.multiple_of(qi * tq, tq), True)
    for h in range(hb):
        acc = acc_s[h]
        o_ref[:, h * V_HEAD:(h + 1) * V_HEAD] = (acc[:, :V_HEAD] / acc[:, V_HEAD:]).astype(o_ref.dtype)


def mla_attention(c, cs, w_q, w_kv, g_cq, g_ckv, g_qn, gv_q, g_kn, gv_k, B, S):
    T = c.shape[0]
    H, tm, hb = MLA_HEADS, ROW_TILE, MLA_HEAD_GROUP
    dq = QK_NOPE + 2 * QK_ROPE
    dv = 2 * V_HEAD
    scale = (QK_NOPE + QK_ROPE) ** -0.5 * LOG2_E
    nt = S // tm
    vec = lambda n: pl.BlockSpec((1, n), lambda i, h: (0, 0))
    nope, rope = (0, QK_NOPE), (QK_NOPE, QK_NOPE + QK_ROPE)
    ones_q = _group_ones(dq, [(nope, nope), (rope, (QK_NOPE, dq))])
    head_out = lambda d: pl.BlockSpec((None, hb, tm, d), lambda i, h: (i // nt, h, i % nt, 0))
    q = pl.pallas_call(
        functools.partial(_mla_q_kernel, scale=scale),
        out_shape=jax.ShapeDtypeStruct((B, H, S, dq), _BF16),
        grid=(T // tm, H // hb),
        in_specs=[pl.BlockSpec((tm, Q_LORA), lambda i, h: (i, 0)), vec(Q_LORA),
                  pl.BlockSpec((Q_LORA, hb * dq), lambda i, h: (0, h)), pl.BlockSpec((dq, dq), lambda i, h: (0, 0)),
                  vec(QK_NOPE), vec(LANES),
                  pl.BlockSpec((tm, LANES), lambda i, h: (i, 0))],
        out_specs=head_out(dq),
        scratch_shapes=[pltpu.VMEM((tm, Q_LORA), _BF16)],
        compiler_params=_params("parallel", "arbitrary"),
        name="mla_q_proj",
    )(c, g_cq.reshape(1, -1), w_q, ones_q, g_qn.reshape(1, -1), gv_q, cs)
    k, v = pl.pallas_call(
        _mla_kv_kernel,
        out_shape=[jax.ShapeDtypeStruct((B, H, S, dq), _BF16), jax.ShapeDtypeStruct((B, H, S, dv), _BF16)],
        grid=(T // tm, H // hb),
        in_specs=[pl.BlockSpec((tm, KV_LORA), lambda i, h: (i, 1)),
                  pl.BlockSpec((tm, LANES), lambda i, h: (i, (Q_LORA + KV_LORA) // LANES)),
                  vec(KV_LORA),
                  pl.BlockSpec((KV_LORA, hb * (QK_NOPE + V_HEAD)), lambda i, h: (0, h)),
                  vec(QK_NOPE), vec(LANES),
                  pl.BlockSpec((tm, LANES), lambda i, h: (i, 0))],
        out_specs=[head_out(dq), head_out(dv)],
        scratch_shapes=[pltpu.VMEM((tm, KV_LORA), _BF16)],
        compiler_params=_params("parallel", "arbitrary"),
        name="mla_kv_proj",
    )(c, c, g_ckv.reshape(1, -1), w_kv, g_kn.reshape(1, -1), gv_k, cs)
    tq = ATTN_TILE
    o = pl.pallas_call(
        _mla_attn_kernel,
        out_shape=jax.ShapeDtypeStruct((B, S, H * V_HEAD), _BF16),
        grid=(B, H // hb, S // tq),
        in_specs=[pl.BlockSpec((None, hb, tq, dq), lambda b, h, i: (b, h, i, 0)),
                  pl.BlockSpec((None, hb, S, dq), lambda b, h, i: (b, h, 0, 0)),
                  pl.BlockSpec((None, hb, S, dv), lambda b, h, i: (b, h, 0, 0))],
        out_specs=pl.BlockSpec((None, tq, hb * V_HEAD), lambda b, h, i: (b, i, h)),
        scratch_shapes=[pltpu.VMEM((hb, tq, LANES), _F32), pltpu.VMEM((hb, tq, dv), _F32)],
        compiler_params=_params("parallel", "parallel", "arbitrary"),
        name="mla_attention",
    )(q, k, v)
    return o.reshape(T, H * V_HEAD)


def _router_kernel(x_ref, g_ref, w_ref, r_ref, hn_ref):
    hn = _rms(x_ref[...], g_ref[...])
    hn_ref[...] = hn.astype(hn_ref.dtype)
    w = w_ref[...]
    h_hi = hn.astype(_BF16)
    h_lo = (hn - h_hi.astype(_F32)).astype(_BF16)
    w_hi = w.astype(_BF16)
    w_lo = (w - w_hi.astype(_F32)).astype(_BF16)
    logits = (jnp.dot(h_hi, w_hi, preferred_element_type=_F32) + jnp.dot(h_hi, w_lo, preferred_element_type=_F32)
              + jnp.dot(h_lo, w_hi, preferred_element_type=_F32))
    lane = lax.broadcasted_iota(jnp.int32, logits.shape, 1)
    lane_f = lane.astype(_F32)
    neg = -jnp.inf
    lg = jnp.where(lane < N_EXPERTS, logits, neg)
    v1 = jnp.max(lg, axis=1, keepdims=True)
    i1 = jnp.min(jnp.where(lg == v1, lane_f, float(LANES)), axis=1, keepdims=True)
    lg2 = jnp.where(lane_f == i1, neg, lg)
    v2 = jnp.max(lg2, axis=1, keepdims=True)
    i2 = jnp.min(jnp.where(lg2 == v2, lane_f, float(LANES)), axis=1, keepdims=True)
    e2 = jnp.exp(v2 - v1)
    w1 = 1.0 / (1.0 + e2)
    w2 = e2 / (1.0 + e2)
    out = jnp.where(lane < N_EXPERTS, logits, 0.0)
    out = jnp.where(lane == N_EXPERTS, i1, out)
    out = jnp.where(lane == N_EXPERTS + 1, i2, out)
    out = jnp.where(lane == N_EXPERTS + 2, w1, out)
    out = jnp.where(lane == N_EXPERTS + 3, w2, out)
    r_ref[...] = out


def router(x, g, w_router_padded):
    T, d = x.shape
    return pl.pallas_call(
        _router_kernel,
        out_shape=[jax.ShapeDtypeStruct((T, LANES), _F32), jax.ShapeDtypeStruct((T, d), _BF16)],
        grid=(T // ROW_TILE,),
        in_specs=[pl.BlockSpec((ROW_TILE, d), lambda i: (i, 0)),
                  pl.BlockSpec((1, d), lambda i: (0, 0)),
                  pl.BlockSpec((d, LANES), lambda i: (0, 0))],
        out_specs=[pl.BlockSpec((ROW_TILE, LANES), lambda i: (i, 0)), pl.BlockSpec((ROW_TILE, d), lambda i: (i, 0))],
        compiler_params=_params("parallel"),
        name="moe_router",
    )(x, g.reshape(1, d), w_router_padded)


def _row_copy(src_hbm, dst_vmem, sem, src_row, dst_row):
    return pltpu.make_async_copy(src_hbm.at[pl.ds(src_row, 1)], dst_vmem.at[pl.ds(dst_row, 1)], sem)


def _index_spec(n):
    return pl.BlockSpec((None, 1, n), lambda i: (i, 0, 0), memory_space=pltpu.SMEM)


def _gather_kernel(tile_ref, chunk_ref, flag_ref, pos_ref, x_ref, o_ref, acc):
    del chunk_ref
    s = pl.program_id(0)
    flags = flag_ref[s]
    tr, ch = acc.shape[0], x_ref.shape[0]

    @pl.when((flags & _G_VALID) != 0)
    def _():
        rows = tile_ref[s] * tr + lax.broadcasted_iota(jnp.int32, (tr, ch), 0)
        pos = pos_ref[...]
        hit = jnp.logical_or(rows == pos[0:1, :], rows == pos[1:2, :])
        onehot = jnp.where(hit, 1.0, 0.0).astype(_BF16)
        part = jnp.dot(onehot, x_ref[...], preferred_element_type=_F32)

        @pl.when((flags & _G_FIRST) != 0)
        def _():
            acc[...] = part

        @pl.when((flags & _G_FIRST) == 0)
        def _():
            acc[...] = acc[...] + part

        @pl.when((flags & _G_LAST) != 0)
        def _():
            o_ref[...] = acc[...].astype(o_ref.dtype)


_G_VALID, _G_FIRST, _G_LAST = 1, 2, 4


def gather_rows(x, pos, n_tiles, tr):
    T, d = x.shape
    r = n_tiles * tr
    ch = GATHER_CHUNK
    n_steps = N_EXPERTS * (T // ch) + n_tiles
    tile_ids = jnp.arange(n_tiles, dtype=jnp.int32)
    member = (pos.reshape(-1) // tr)[:, None] == tile_ids[None, :]
    token = (jnp.arange(T * TOP_K, dtype=jnp.int32) // TOP_K)[:, None]
    first_tok = jnp.min(jnp.where(member, token, T), axis=0)
    last_tok = jnp.max(jnp.where(member, token, -1), axis=0)
    lo = jnp.where(last_tok >= 0, first_tok, 0) // ch
    hi = jnp.maximum(last_tok, 0) // ch
    count = hi - lo + 1
    end = jnp.cumsum(count)
    steps = jnp.arange(n_steps, dtype=jnp.int32)
    valid = steps < end[-1]
    tile = jnp.minimum(jnp.sum((end[None, :] <= steps[:, None]).astype(jnp.int32), axis=1), n_tiles - 1)
    tile_oh = (tile[:, None] == jnp.arange(n_tiles, dtype=jnp.int32)[None, :]).astype(jnp.int32)
    pick = lambda v: jnp.sum(tile_oh * v[None, :], axis=1)
    start = pick(end - count)
    chunk = jnp.where(valid, pick(lo) + steps - start, hi[-1])
    flags = (valid * _G_VALID + jnp.logical_and(valid, steps == start) * _G_FIRST
             + jnp.logical_and(valid, steps == pick(end) - 1) * _G_LAST)
    return pl.pallas_call(
        _gather_kernel,
        out_shape=jax.ShapeDtypeStruct((r, d), x.dtype),
        grid_spec=pltpu.PrefetchScalarGridSpec(
            num_scalar_prefetch=3,
            grid=(n_steps,),
            in_specs=[pl.BlockSpec((TOP_K, ch), lambda s, tile, chunk, flag: (0, chunk[s])),
                      pl.BlockSpec((ch, d), lambda s, tile, chunk, flag: (chunk[s], 0))],
            out_specs=pl.BlockSpec((tr, d), lambda s, tile, chunk, flag: (tile[s], 0)),
            scratch_shapes=[pltpu.VMEM((tr, d), _F32)],
        ),
        compiler_params=_params("arbitrary"),
        name="moe_gather",
    )(tile.astype(jnp.int32), chunk.astype(jnp.int32), flags.astype(jnp.int32), pos.T, x)


def _combine_kernel(pos_ref, x_ref, w_ref, y_hbm, o_ref, buf_a, buf_b, sem):
    tt = buf_a.shape[0]

    def start(r, carry):
        _row_copy(y_hbm, buf_a, sem, pos_ref[0, TOP_K * r], r).start()
        _row_copy(y_hbm, buf_b, sem, pos_ref[0, TOP_K * r + 1], r).start()
        return carry

    def wait(r, carry):
        _row_copy(y_hbm, buf_a, sem, 0, r).wait()
        _row_copy(y_hbm, buf_b, sem, 0, r).wait()
        return carry

    lax.fori_loop(0, tt, start, 0)
    lax.fori_loop(0, tt, wait, 0)
    w = w_ref[...]
    o_ref[...] = x_ref[...] + w[:, 0:1] * buf_a[...] + w[:, 1:2] * buf_b[...]


def moe_combine(x, y, pos_flat, weights, tt):
    T, d = x.shape
    return pl.pallas_call(
        _combine_kernel,
        out_shape=jax.ShapeDtypeStruct((T, d), _F32),
        grid=(T // tt,),
        in_specs=[_index_spec(tt * TOP_K), pl.BlockSpec((tt, d), lambda i: (i, 0)),
                  pl.BlockSpec((tt, TOP_K), lambda i: (i, 0)), pl.BlockSpec(memory_space=pl.ANY)],
        out_specs=pl.BlockSpec((tt, d), lambda i: (i, 0)),
        scratch_shapes=[pltpu.VMEM((tt, d), _F32), pltpu.VMEM((tt, d), _F32), pltpu.SemaphoreType.DMA],
        compiler_params=_params("arbitrary"),
        name="moe_combine",
    )(pos_flat.reshape(T // tt, 1, tt * TOP_K), x, weights, y)


def _moe_plan(route, tm, n_tiles):
    T = route.shape[0]
    e_flat = route[:, N_EXPERTS:N_EXPERTS + TOP_K].astype(jnp.int32).reshape(-1)
    onehot = (e_flat[:, None] == jnp.arange(N_EXPERTS, dtype=jnp.int32)[None, :]).astype(jnp.int32)
    csum = jnp.cumsum(onehot, axis=0)
    rank = jnp.sum(csum * onehot, axis=1) - 1
    tiles_e = (csum[-1] + tm - 1) // tm
    tile_end = jnp.cumsum(tiles_e)
    tile_start = tile_end - tiles_e
    pos = jnp.sum(onehot * tile_start[None, :], axis=1) * tm + rank
    tile_ids = jnp.arange(n_tiles, dtype=jnp.int32)
    tile_expert = jnp.minimum(jnp.sum((tile_ids[:, None] >= tile_end[None, :]).astype(jnp.int32), axis=1),
                              N_EXPERTS - 1).astype(jnp.int32)
    n_used = tile_end[-1:].astype(jnp.int32)
    experts = jnp.arange(N_EXPERTS, dtype=jnp.int32)
    has = tiles_e > 0
    later = jnp.where(jnp.logical_and(has[None, :], experts[None, :] > experts[:, None]), experts[None, :], N_EXPERTS)
    next_e = jnp.min(later, axis=1)
    last_e = next_e == N_EXPERTS
    next_e = jnp.where(last_e, jnp.min(jnp.where(has, experts, N_EXPERTS)), next_e)
    group_e = jnp.cumsum(has.astype(jnp.int32)) - 1
    tile_onehot = (tile_expert[:, None] == experts[None, :]).astype(jnp.int32)

    def of_tile(per_expert):
        return jnp.sum(tile_onehot * per_expert.astype(jnp.int32)[None, :], axis=1)

    index = tile_ids - of_tile(tile_start)
    groups = (tile_expert, (index == 0).astype(jnp.int32), index, jnp.maximum(of_tile(tiles_e), 1),
              of_tile(next_e), of_tile(last_e), of_tile(group_e),
              jnp.full((n_tiles,), jnp.sum(has.astype(jnp.int32)), jnp.int32))
    return groups, n_used, pos.astype(jnp.int32).reshape(T, TOP_K)


def moe_layer(x, g, w_router, we_gate, we_up, we_down):
    T, d = x.shape
    tm = MOE_ROW_TILE
    n_tiles = (T * TOP_K) // tm + N_EXPERTS
    w_r = jnp.zeros((d, LANES), _F32).at[:, :N_EXPERTS].set(w_router)
    route, hn = router(x, g, w_r)
    plan, n_used, pos = _moe_plan(route, tm, n_tiles)
    xs = gather_rows(hn, pos, n_tiles, tm)
    h = streamed_matmul(xs, [we_gate, we_up], plan, n_used, tm=tm, tn=UP_COLS, n_chunks=UP_CHUNKS, out_dtype=_BF16)
    y = streamed_matmul(h, [we_down], plan, n_used, tm=tm, tn=DOWN_COLS, n_chunks=DOWN_CHUNKS, out_dtype=_F32)
    weights = route[:, N_EXPERTS + TOP_K:N_EXPERTS + 2 * TOP_K]
    return moe_combine(x, y, pos.reshape(-1), weights, tm)


def _rope_tables(positions, half):
    inv = ROPE_BASE ** (-jnp.arange(half, dtype=_F32) / half)
    ang = positions.astype(_F32)[..., None] * inv
    return jnp.cos(ang), jnp.sin(ang)


def _rotate_half_cols(w, width):
    lead = w.shape[:-1]
    w2 = w.reshape(lead + (-1, 2, width // 2))
    return jnp.stack([-w2[..., 1, :], w2[..., 0, :]], axis=-2).reshape(w.shape)


def _swap_halves(g):
    half = g.shape[-1] // 2
    return jnp.concatenate([g[..., half:], g[..., :half]], axis=-1)


def kernel(x, positions, g_mix_norm, g_ffn_norm, w_in, conv_w, b_gates, g_mlstm_out, g_ret_out, w_mix_out,
           ffn_gate, ffn_up, ffn_down, w_dqkv, g_cq, g_ckv, w_uq, w_ukv, g_qn, g_qr, g_kn, g_kr, w_o,
           w_router, we_gate, we_up, we_down):
    B, S, D = x.shape
    T = B * S
    xf = x.reshape(T, D)
    n_gate = 2 * MLSTM_HEADS
    main = 4 * MLSTM_HEADS * HEAD_DIM

    w = w_in[0]
    w_gate = jnp.zeros((D, LANES), _F32).at[:, :n_gate].set(w[:, main:main + n_gate])
    g0 = g_mix_norm[0]
    dense = dict(tm=DENSE_ROW_TILE, vmem_limit_bytes=DENSE_VMEM_LIMIT_BYTES)
    plan, nu = _dense_stream_plan(T, DENSE_ROW_TILE)
    proj_m = streamed_matmul(xf, [w_in], plan, nu, tn=PROJ_COLS, n_chunks=UP_CHUNKS, out_dtype=_F32,
                             gain=g0, n_out=main, **dense)
    proj_r = streamed_matmul(xf, [w[None, :, main + n_gate:]], plan, nu, tn=PROJ_COLS,
                             n_chunks=UP_CHUNKS, out_dtype=_F32, gain=g0, **dense)
    gates = norm_matmul(xf, g0, w_gate[None], tn=LANES)[:, :n_gate] + b_gates[0][None, :]
    cos_r, sin_r = _rope_tables(positions, HEAD_DIM // 2)
    log_gamma = jnp.log1p(-jnp.exp2(-5.0 - jnp.arange(RET_HEADS, dtype=_F32)))
    mix = recurrent_mixer(proj_m.reshape(B, S, -1), proj_r.reshape(B, S, -1), gates.reshape(B, S, n_gate),
                          cos_r, sin_r, conv_w[0], g_mlstm_out[0], g_ret_out[0], log_gamma).reshape(T, -1)
    w_mix = w_mix_out[0].reshape(2, MLSTM_HEADS, HEAD_DIM, D).transpose(1, 0, 2, 3).reshape(-1, D)
    xf = dense_matmul(mix, w_mix, tn=1024, out_dtype=_F32, residual=xf)

    plan_up, nu_up = _dense_stream_plan(T, ROW_TILE)
    hmid = streamed_matmul(xf, [ffn_gate, ffn_up], plan_up, nu_up, tm=ROW_TILE, tn=UP_COLS, n_chunks=UP_CHUNKS,
                           out_dtype=_BF16, gain=g_ffn_norm[0])
    xf = streamed_matmul(hmid, [ffn_down], plan, nu, tn=DENSE_DOWN_COLS, n_chunks=DOWN_CHUNKS,
                         out_dtype=_F32, residual=xf, **dense)

    H = MLA_HEADS
    wd = w_dqkv[0]
    w_kr = wd[:, Q_LORA + KV_LORA:]
    wd_full = jnp.concatenate([wd, _rotate_half_cols(w_kr, QK_ROPE)], axis=1)
    c = norm_matmul(xf, g_mix_norm[1], wd_full[None], tn=wd_full.shape[1] // 3)
    wq = w_uq[0].reshape(Q_LORA, H, QK_NOPE + QK_ROPE)
    wq_r = wq[..., QK_NOPE:]
    w_q = jnp.concatenate([wq, _rotate_half_cols(wq_r, QK_ROPE)], axis=-1).reshape(Q_LORA, -1)
    cos_m, sin_m = _rope_tables(positions, QK_ROPE // 2)
    cs = jnp.concatenate([cos_m, cos_m, sin_m, sin_m], axis=-1).reshape(T, LANES)
    gv_q = jnp.concatenate([g_qr[0], _swap_halves(g_qr[0])]).reshape(1, LANES)
    gv_k = jnp.concatenate([g_kr[0], _swap_halves(g_kr[0])]).reshape(1, LANES)
    attn = mla_attention(c, cs, w_q, w_ukv[0], g_cq[0], g_ckv[0], g_qn[0], gv_q, g_kn[0], gv_k, B, S)
    xf = dense_matmul(attn, w_o[0], tn=1024, out_dtype=_F32, residual=xf)

    xf = moe_layer(xf, g_ffn_norm[1], w_router[0], we_gate[0], we_up[0], we_down[0])
    return xf.reshape(B, S, D)
```

```python
import functools

import jax
import jax.numpy as jnp
import numpy as np
from jax import lax
from jax.experimental import pallas as pl
from jax.experimental.pallas import tpu as pltpu

EPS = 1e-6
ROPE_BASE = 10000.0
CONV_WIDTH = 4
MLSTM_HEADS = 4
RET_HEADS = 4
HEAD_DIM = 256
REC_CHUNK = 256
MLA_HEADS = 16
Q_LORA = 512
KV_LORA = 512
QK_NOPE = 128
QK_ROPE = 64
V_HEAD = 128
N_EXPERTS = 8
TOP_K = 2

LANES = 128
SUBLANES = 8
VMEM_LIMIT_BYTES = 48 * 1024 * 1024

ROW_TILE = 512
DENSE_ROW_TILE = 1024
DENSE_VMEM_LIMIT_BYTES = 58 * 1024 * 1024
NORM_ROW_TILE = 1024
NORM_VMEM_LIMIT_BYTES = 56 * 1024 * 1024
MOE_ROW_TILE = 256
UP_COLS = 1408
UP_CHUNKS = 8
PROJ_COLS = 2048
DOWN_COLS = 1024
DENSE_DOWN_COLS = 512
DOWN_CHUNKS = 11
ATTN_TILE = 512
MLA_HEAD_GROUP = 4
GATHER_CHUNK = 512
LOG2_E = 1.4426950408889634

_F32 = jnp.float32
_BF16 = jnp.bfloat16
_NT = (((1,), (1,)), ((), ()))
_TN = (((0,), (0,)), ((), ()))


def _params(*semantics, vmem_limit_bytes=VMEM_LIMIT_BYTES):
    return pltpu.CompilerParams(dimension_semantics=semantics, vmem_limit_bytes=vmem_limit_bytes)


def _sigmoid(x):
    return 1.0 / (1.0 + jnp.exp(-x))


def _rms(x, g):
    return x * lax.rsqrt(jnp.mean(x * x, axis=-1, keepdims=True) + EPS) * g


def _gmm_kernel(te_ref, nu_ref, a_ref, w_ref, *rest):
    del te_ref
    o_ref = rest[-1]

    @pl.when(pl.program_id(1) < nu_ref[0])
    def _():
        acc = jnp.dot(a_ref[...], w_ref[...].astype(_BF16), preferred_element_type=_F32)
        if len(rest) == 2:
            acc = acc + rest[0][...]
        o_ref[...] = acc.astype(o_ref.dtype)

    @pl.when(pl.program_id(1) >= nu_ref[0])
    def _():
        o_ref[...] = jnp.zeros_like(o_ref)


def grouped_matmul(a, w, tile_expert, n_used, *, tm, tn, out_dtype, residual=None):
    m, k = a.shape
    n = w.shape[2]
    n_tiles = m // tm

    def row(j, i, te, nu):
        return jnp.minimum(i, nu[0] - 1)

    in_specs = [pl.BlockSpec((tm, k), lambda j, i, te, nu: (row(j, i, te, nu), 0)),
                pl.BlockSpec((None, k, tn), lambda j, i, te, nu: (te[row(j, i, te, nu)], 0, j))]
    args = [a, w]
    if residual is not None:
        in_specs.append(pl.BlockSpec((tm, tn), lambda j, i, te, nu: (row(j, i, te, nu), j)))
        args.append(residual)
    return pl.pallas_call(
        _gmm_kernel,
        out_shape=jax.ShapeDtypeStruct((m, n), out_dtype),
        grid_spec=pltpu.PrefetchScalarGridSpec(
            num_scalar_prefetch=2,
            grid=(n // tn, n_tiles),
            in_specs=in_specs,
            out_specs=pl.BlockSpec((tm, tn), lambda j, i, te, nu: (i, j)),
        ),
        compiler_params=_params("arbitrary", "arbitrary"),
        name="grouped_matmul",
    )(tile_expert, n_used, *args)


_P_EXPERT, _P_FIRST, _P_LO, _P_HI, _P_NEXT, _P_LAST, _P_GROUP, _P_NGROUPS = range(8)


def _streamed_kernel(plan_ref, nu_ref, a_ref, *refs, n_mats, has_gain, has_res, n_col_tiles):
    w_hbm = refs[:n_mats]
    gain_ref = refs[n_mats] if has_gain else None
    res_ref = refs[n_mats + int(has_gain)] if has_res else None
    o_ref, wbf, stage, sem = refs[n_mats + int(has_gain) + int(has_res):]
    j = pl.program_id(0)
    i = pl.program_id(1)
    _, _, k, tn = wbf.shape
    kc = stage.shape[2]
    n_chunks = k // kc

    def chunk_copy(e, jj, c, m):
        src = w_hbm[m].at[e, pl.ds(pl.multiple_of(c * kc, kc), kc), pl.ds(pl.multiple_of(jj * tn, LANES), tn)]
        return pltpu.make_async_copy(src, stage.at[c & 1, m], sem.at[c & 1, m])

    def prime(e, jj):
        for c in range(2):
            for m in range(n_mats):
                chunk_copy(e, jj, c, m).start()

    def fetch(e, jj, slot, lo, hi):
        def body(c, carry):
            for m in range(n_mats):
                chunk_copy(e, jj, c, m).wait()
                wbf[slot, m, pl.ds(pl.multiple_of(c * kc, kc), kc), :] = stage[c & 1, m].astype(_BF16)

                @pl.when(c + 2 < n_chunks)
                def _():
                    chunk_copy(e, jj, c + 2, m).start()
            return carry

        lax.fori_loop(lo, hi, body, 0)

    expert = plan_ref[_P_EXPERT, i]
    used = i < nu_ref[0]
    cur = (j * plan_ref[_P_NGROUPS, 0] + plan_ref[_P_GROUP, i]) & 1
    in_last_group = plan_ref[_P_LAST, i] == 1
    has_next = jnp.logical_and(used, jnp.logical_not(jnp.logical_and(in_last_group, j == n_col_tiles - 1)))
    next_e = plan_ref[_P_NEXT, i]
    next_j = jnp.where(in_last_group, j + 1, j)

    @pl.when(jnp.logical_and(j == 0, i == 0))
    def _():
        prime(expert, 0)
        fetch(expert, 0, 0, 0, n_chunks)

    @pl.when(jnp.logical_and(has_next, plan_ref[_P_FIRST, i] == 1))
    def _():
        prime(next_e, next_j)

    @pl.when(used)
    def _():
        a = a_ref[...]
        if has_gain:
            a = _rms(a, gain_ref[...]).astype(_BF16)
        if n_mats == 2:
            g = jnp.dot(a, wbf[cur, 0], preferred_element_type=_F32)
            u = jnp.dot(a, wbf[cur, 1], preferred_element_type=_F32)
            out = g * _sigmoid(g) * u
        else:
            out = jnp.dot(a, wbf[cur, 0], preferred_element_type=_F32)
            if has_res:
                out = out + res_ref[...]
        o_ref[...] = out.astype(o_ref.dtype)

    @pl.when(has_next)
    def _():
        fetch(next_e, next_j, 1 - cur, plan_ref[_P_LO, i], plan_ref[_P_HI, i])

    @pl.when(jnp.logical_not(used))
    def _():
        o_ref[...] = jnp.zeros_like(o_ref)


def _chunk_shares(index, size, n_chunks):
    return (index * n_chunks) // size, ((index + 1) * n_chunks) // size


def streamed_matmul(a, weights, groups, n_used, *, tm, tn, n_chunks, out_dtype, gain=None, residual=None,
                    n_out=None, vmem_limit_bytes=VMEM_LIMIT_BYTES):
    expert, first, index, size, nxt, last, group, n_groups = groups
    lo, hi = _chunk_shares(index, size, n_chunks)
    plan = jnp.stack([expert, first, lo, hi, nxt, last, group, n_groups]).astype(jnp.int32)
    m, k = a.shape
    n = weights[0].shape[2] if n_out is None else n_out
    n_mats = len(weights)

    def row(j, i, plan, nu):
        return jnp.minimum(i, nu[0] - 1)

    in_specs = [pl.BlockSpec((tm, k), lambda j, i, plan, nu: (row(j, i, plan, nu), 0))]
    in_specs += [pl.BlockSpec(memory_space=pl.ANY)] * n_mats
    args = [a, *weights]
    if gain is not None:
        in_specs.append(pl.BlockSpec((1, k), lambda j, i, plan, nu: (0, 0)))
        args.append(gain.reshape(1, k))
    if residual is not None:
        in_specs.append(pl.BlockSpec((tm, tn), lambda j, i, plan, nu: (row(j, i, plan, nu), j)))
        args.append(residual)
    return pl.pallas_call(
        functools.partial(_streamed_kernel, n_mats=n_mats, has_gain=gain is not None, has_res=residual is not None,
                          n_col_tiles=n // tn),
        out_shape=jax.ShapeDtypeStruct((m, n), out_dtype),
        grid_spec=pltpu.PrefetchScalarGridSpec(
            num_scalar_prefetch=2,
            grid=(n // tn, m // tm),
            in_specs=in_specs,
            out_specs=pl.BlockSpec((tm, tn), lambda j, i, plan, nu: (i, j)),
            scratch_shapes=[pltpu.VMEM((2, n_mats, k, tn), _BF16),
                            pltpu.VMEM((2, n_mats, k // n_chunks, tn), _F32),
                            pltpu.SemaphoreType.DMA((2, n_mats))],
        ),
        compiler_params=_params("arbitrary", "arbitrary", vmem_limit_bytes=vmem_limit_bytes),
        name="streamed_swiglu_up" if n_mats == 2 else "streamed_matmul",
    )(plan, n_used, *args)


def _dense_stream_plan(m, tm):
    n_tiles = m // tm
    ids = jnp.arange(n_tiles, dtype=jnp.int32)
    zero = jnp.zeros_like(ids)
    one = jnp.ones_like(ids)
    groups = (zero, (ids == 0).astype(jnp.int32), ids, one * n_tiles, zero, one, zero, one)
    return groups, jnp.full((1,), n_tiles, jnp.int32)


def _dense_plan(m, tm):
    n_tiles = m // tm
    return jnp.zeros((n_tiles,), jnp.int32), jnp.full((1,), n_tiles, jnp.int32)


def dense_matmul(a, w, *, tn, out_dtype, residual=None):
    te, nu = _dense_plan(a.shape[0], DENSE_ROW_TILE)
    return grouped_matmul(a, w[None], te, nu, tm=DENSE_ROW_TILE, tn=tn, out_dtype=out_dtype, residual=residual)


def _norm_matmul_kernel(x_ref, g_ref, w_ref, o_ref, xn_s):
    @pl.when(pl.program_id(1) == 0)
    def _():
        xn_s[...] = _rms(x_ref[...], g_ref[...]).astype(_BF16)

    o_ref[...] = jnp.dot(xn_s[...], w_ref[...].astype(_BF16), preferred_element_type=_F32)


def norm_matmul(x, gain, w, *, tn):
    m, k = x.shape
    n = w.shape[2]
    tm = NORM_ROW_TILE
    return pl.pallas_call(
        _norm_matmul_kernel,
        out_shape=jax.ShapeDtypeStruct((m, n), _F32),
        grid=(m // tm, n // tn),
        in_specs=[pl.BlockSpec((tm, k), lambda i, j: (i, 0)),
                  pl.BlockSpec((1, k), lambda i, j: (0, 0)),
                  pl.BlockSpec((None, k, tn), lambda i, j: (0, 0, j))],
        out_specs=pl.BlockSpec((tm, tn), lambda i, j: (i, j)),
        scratch_shapes=[pltpu.VMEM((tm, k), _BF16)],
        compiler_params=pltpu.CompilerParams(dimension_semantics=("parallel", "arbitrary"),
                                             vmem_limit_bytes=NORM_VMEM_LIMIT_BYTES),
        name="norm_matmul",
    )(x, gain.reshape(1, k), w)


def _recurrent_kernel(q_ref, k_ref, v_ref, o_ref, rq_ref, rk_ref, rv_ref, rg_ref,
                      gt_ref, gc_ref, cos_ref, sin_ref, cwq_ref, cwk_ref, gm_ref, gr_ref, lg_ref,
                      mix_ref,
                      c_s, n_s, m_s, r_s, qbuf, kbuf):
    L, dh = q_ref.shape
    halo = SUBLANES
    inv_sqrt_d = dh ** -0.5

    @pl.when(pl.program_id(2) == 0)
    def _():
        c_s[...] = jnp.zeros_like(c_s)
        n_s[...] = jnp.zeros_like(n_s)
        m_s[...] = jnp.full_like(m_s, -jnp.inf)
        r_s[...] = jnp.zeros_like(r_s)
        qbuf[0:halo, :] = jnp.zeros((halo, dh), _F32)
        kbuf[0:halo, :] = jnp.zeros((halo, dh), _F32)

    def conv_silu(x_ref, buf, w_ref):
        buf[halo:halo + L, :] = x_ref[...]
        w = w_ref[...]
        y = w[CONV_WIDTH - 1:CONV_WIDTH, :] * buf[halo:halo + L, :]
        for j in range(CONV_WIDTH - 1):
            off = halo - (CONV_WIDTH - 1) + j
            y = y + w[j:j + 1, :] * buf[off:off + L, :]
        buf[0:halo, :] = buf[L:L + halo, :]
        return y * _sigmoid(y)

    row = lax.broadcasted_iota(jnp.int32, (L, L), 0)
    col = lax.broadcasted_iota(jnp.int32, (L, L), 1)
    causal = col <= row

    q = conv_silu(q_ref, qbuf, cwq_ref)
    k = conv_silu(k_ref, kbuf, cwk_ref)
    qb = q.astype(_BF16)
    vb = v_ref[...].astype(_BF16)

    def log_sigmoid(x):
        return jnp.minimum(x, 0.0) - jnp.log1p(jnp.exp(-jnp.abs(x)))

    gt = gt_ref[...]
    gc = gc_ref[...]
    i_row = gt[0:1, :]
    f_row = log_sigmoid(gt[1:2, :])
    i_col = gc[:, 0:1]
    f_col = log_sigmoid(gc[:, 1:2])
    hi = lax.Precision.HIGHEST
    b_col = jnp.dot(causal.astype(_F32), f_col, precision=hi, preferred_element_type=_F32)
    b_row = jnp.dot(f_row, (row <= col).astype(_F32), precision=hi, preferred_element_type=_F32)
    g_tot = b_col[L - 1:L, :]
    m_prev = m_s[...]

    log_d = jnp.where(causal, b_col - b_row + i_row, -jnp.inf)
    m_inter = b_col + m_prev
    m_t = jnp.maximum(jnp.max(log_d, axis=1, keepdims=True), m_inter)
    d_m = jnp.exp(log_d - m_t)
    inter = jnp.exp(m_inter - m_t)
    s = lax.dot_general(qb, k.astype(_BF16), _NT, preferred_element_type=_F32) * inv_sqrt_d
    s_m = s * d_m
    num = (jnp.dot(s_m.astype(_BF16), vb, preferred_element_type=_F32)
           + inter * jnp.dot(qb, c_s[...].astype(_BF16), preferred_element_type=_F32))
    den = jnp.sum(s_m, axis=1, keepdims=True) + inter * jnp.sum(q * n_s[...], axis=1, keepdims=True)
    h = num / jnp.maximum(jnp.abs(den), jnp.exp(-m_t))

    log_w = g_tot - b_col + i_col
    m_new = jnp.maximum(g_tot + m_prev, jnp.max(log_w, axis=0, keepdims=True))
    w_col = jnp.exp(log_w - m_new)
    decay = jnp.exp(g_tot + m_prev - m_new)
    kw = k * (w_col * inv_sqrt_d)
    c_s[...] = decay * c_s[...] + lax.dot_general(kw.astype(_BF16), vb, _TN, preferred_element_type=_F32)
    n_s[...] = decay * n_s[...] + jnp.sum(kw, axis=0, keepdims=True)
    m_s[...] = m_new

    mix_ref[:, :dh] = _rms(_sigmoid(o_ref[...]) * h, gm_ref[...]).astype(mix_ref.dtype)

    half = dh // 2
    cos = cos_ref[...]
    sin = sin_ref[...]

    def rope(x):
        x1 = x[:, :half]
        x2 = x[:, half:]
        return jnp.concatenate([x1 * cos - x2 * sin, x2 * cos + x1 * sin], axis=1)

    rq = rope(rq_ref[...]).astype(_BF16)
    rk = rope(rk_ref[...])
    rvb = rv_ref[...].astype(_BF16)
    lg = lg_ref[...]
    dist = (row - col).astype(_F32)
    d_r = jnp.where(causal, jnp.exp(lg * jnp.maximum(dist, 0.0)), 0.0)
    t_col = lax.broadcasted_iota(jnp.int32, (L, 1), 0).astype(_F32)
    q_decay = jnp.exp(lg * (t_col + 1.0))
    k_decay = jnp.exp(lg * (L - 1.0 - t_col))
    chunk_decay = jnp.exp(lg * L)
    s_r = lax.dot_general(rq, rk.astype(_BF16), _NT, preferred_element_type=_F32) * inv_sqrt_d * d_r
    out_r = (jnp.dot(s_r.astype(_BF16), rvb, preferred_element_type=_F32)
             + jnp.dot(rq, r_s[...].astype(_BF16), preferred_element_type=_F32) * q_decay)
    r_s[...] = chunk_decay * r_s[...] + lax.dot_general(
        (rk * (k_decay * inv_sqrt_d)).astype(_BF16), rvb, _TN, preferred_element_type=_F32)
    rg = rg_ref[...]
    mix_ref[:, dh:] = (rg * _sigmoid(rg) * _rms(out_r, gr_ref[...])).astype(mix_ref.dtype)


def recurrent_mixer(proj_m, proj_r, gates, cos, sin, conv_w, g_mlstm_out, g_ret_out, log_gamma):
    B, S, _ = proj_m.shape
    H, dh, L = MLSTM_HEADS, HEAD_DIM, REC_CHUNK
    g4 = gates.reshape(B, S, 2, H)
    gt = g4.transpose(0, 3, 2, 1)
    gc = g4.transpose(0, 3, 1, 2)

    def pspec(group):
        return pl.BlockSpec((None, L, dh), lambda b, h, c: (b, c, group * H + h))

    in_specs = [pspec(g) for g in range(4)] * 2 + [
        pl.BlockSpec((None, None, 2, L), lambda b, h, c: (b, h, 0, c)),
        pl.BlockSpec((None, None, L, 2), lambda b, h, c: (b, h, c, 0)),
        pl.BlockSpec((None, L, dh // 2), lambda b, h, c: (b, c, 0)),
        pl.BlockSpec((None, L, dh // 2), lambda b, h, c: (b, c, 0)),
        pl.BlockSpec((CONV_WIDTH, dh), lambda b, h, c: (0, h)),
        pl.BlockSpec((CONV_WIDTH, dh), lambda b, h, c: (0, H + h)),
        pl.BlockSpec((1, dh), lambda b, h, c: (0, h)),
        pl.BlockSpec((1, dh), lambda b, h, c: (0, h)),
        pl.BlockSpec((None, 1, 1), lambda b, h, c: (h, 0, 0)),
    ]
    return pl.pallas_call(
        _recurrent_kernel,
        out_shape=jax.ShapeDtypeStruct((B, S, 2 * H * dh), _BF16),
        grid=(B, H, S // L),
        in_specs=in_specs,
        out_specs=pl.BlockSpec((None, L, 2 * dh), lambda b, h, c: (b, c, h)),
        scratch_shapes=[pltpu.VMEM((dh, dh), _F32), pltpu.VMEM((1, dh), _F32), pltpu.VMEM((1, 1), _F32),
                        pltpu.VMEM((dh, dh), _F32),
                        pltpu.VMEM((L + SUBLANES, dh), _F32), pltpu.VMEM((L + SUBLANES, dh), _F32)],
        compiler_params=_params("parallel", "parallel", "arbitrary"),
        name="recurrent_mixer",
    )(*([proj_m] * 4), *([proj_r] * 4), gt, gc, cos, sin, conv_w, conv_w,
      g_mlstm_out.reshape(1, H * dh), g_ret_out.reshape(1, H * dh), log_gamma.reshape(H, 1, 1))


def _rope_pair(z2, cs, gvec, sumsq=None):
    lane = lax.broadcasted_iota(jnp.int32, z2.shape, 1)
    first = lane < QK_ROPE
    if sumsq is None:
        sumsq = jnp.sum(jnp.where(first, z2 * z2, 0.0), axis=1, keepdims=True)
    t = z2 * lax.rsqrt(sumsq * (1.0 / QK_ROPE) + EPS) * (cs * gvec)
    return jnp.where(first, t + pltpu.roll(t, QK_ROPE, 1), 0.0)


def _group_sumsq(z, ones_ref):
    return jnp.dot((z * z).astype(_BF16), ones_ref[...], preferred_element_type=_F32)


def _group_ones(width, pairs):
    m = np.zeros((width, width), np.float32)
    for (k0, k1), (n0, n1) in pairs:
        m[k0:k1, n0:n1] = 1.0
    return jnp.asarray(m, _BF16)


def _mla_q_kernel(c_ref, gcq_ref, w_ref, ones_ref, gqn_ref, gqr_ref, cs_ref, q_ref, cn_s, *, scale):
    @pl.when(pl.program_id(1) == 0)
    def _():
        cn_s[...] = _rms(c_ref[...], gcq_ref[...]).astype(_BF16)

    dq = q_ref.shape[-1]
    z = jnp.dot(cn_s[...], w_ref[...].astype(_BF16), preferred_element_type=_F32)
    cs = cs_ref[...]
    for h in range(q_ref.shape[0]):
        zh = z[:, h * dq:(h + 1) * dq]
        ssh = _group_sumsq(zh, ones_ref)
        qn = zh[:, :QK_NOPE] * lax.rsqrt(ssh[:, :QK_NOPE] * (1.0 / QK_NOPE) + EPS) * gqn_ref[...]
        qr = _rope_pair(zh[:, QK_NOPE:], cs, gqr_ref[...], ssh[:, QK_NOPE:])
        q_ref[h] = (jnp.concatenate([qn, qr], axis=1) * scale).astype(q_ref.dtype)


def _mla_kv_kernel(c_ref, kr_ref, gckv_ref, w_ref, gkn_ref, gkr_ref, cs_ref, k_ref, v_ref, cn_s):
    @pl.when(pl.program_id(1) == 0)
    def _():
        cn_s[...] = _rms(c_ref[...], gckv_ref[...]).astype(_BF16)

    dk = k_ref.shape[-1]
    z = jnp.dot(cn_s[...], w_ref[...].astype(_BF16), preferred_element_type=_F32)
    kr = _rope_pair(kr_ref[...], cs_ref[...], gkr_ref[...])
    for h in range(k_ref.shape[0]):
        zh = z[:, h * dk:(h + 1) * dk]
        kn = _rms(zh[:, :QK_NOPE], gkn_ref[...])
        k_ref[h] = jnp.concatenate([kn, kr], axis=1).astype(k_ref.dtype)
        v_ref[h] = zh[:, QK_NOPE:].astype(v_ref.dtype)


def _mla_attn_kernel(q_ref, k_ref, v_ref, o_ref, m_s, acc_s):
    qi = pl.program_id(2)
    hb, tq, _ = q_ref.shape
    dv = 2 * v_ref.shape[-1]
    ones = jnp.ones((tq, v_ref.shape[-1]), _BF16)
    for h in range(hb):
        m_s[h] = jnp.full((tq, LANES), -jnp.inf, _F32)
        acc_s[h] = jnp.zeros((tq, dv), _F32)

    def block(h, start, diagonal):
        kb = k_ref[h, pl.ds(start, tq), :]
        vb = jnp.concatenate([v_ref[h, pl.ds(start, tq), :], ones], axis=1)
        s = lax.dot_general(q_ref[h], kb, _NT, preferred_element_type=_F32)
        if diagonal:
            row = lax.broadcasted_iota(jnp.int32, s.shape, 0)
            col = lax.broadcasted_iota(jnp.int32, s.shape, 1)
            s = jnp.where(col <= row, s, -jnp.inf)
        m_prev = m_s[h]
        m_new = jnp.maximum(m_prev, jnp.max(s, axis=1, keepdims=True))
        p = jnp.exp2(s - jnp.concatenate([m_new] * (tq // LANES), axis=1))
        alpha = jnp.exp2(m_prev - m_new)
        acc_s[h] = jnp.concatenate([alpha] * (dv // LANES), axis=1) * acc_s[h] + jnp.dot(
            p.astype(_BF16), vb, preferred_element_type=_F32)
        m_s[h] = m_new

    def pair(j2, carry):
        for u in range(2):
            for h in range(hb):
                block(h, pl.multiple_of((2 * j2 + u) * tq, tq), False)
        return carry

    n_pairs = lax.div(qi, 2)
    lax.fori_loop(0, n_pairs, pair, 0)
    odd = qi - 2 * n_pairs

    @pl.when(odd == 1)
    def _():
        for h in range(hb):
            block(h, pl.multiple_of((qi - 1) * tq, tq), False)
        for h in range(hb):
            block(h, pl.multiple_of(qi * tq, tq), True)

    @pl.when(odd == 0)
    def _():
        for h in range(hb):
            block(h, pl.multiple_of(qi * tq, tq), True)
    for h in range(hb):
        acc = acc_s[h]
        o_ref[:, h * V_HEAD:(h + 1) * V_HEAD] = (acc[:, :V_HEAD] / acc[:, V_HEAD:]).astype(o_ref.dtype)


def mla_attention(c, cs, w_q, w_kv, g_cq, g_ckv, g_qn, gv_q, g_kn, gv_k, B, S):
    T = c.shape[0]
    H, tm, hb = MLA_HEADS, ROW_TILE, MLA_HEAD_GROUP
    dq = QK_NOPE + 2 * QK_ROPE
    dv = 2 * V_HEAD
    scale = (QK_NOPE + QK_ROPE) ** -0.5 * LOG2_E
    nt = S // tm
    vec = lambda n: pl.BlockSpec((1, n), lambda i, h: (0, 0))
    nope, rope = (0, QK_NOPE), (QK_NOPE, QK_NOPE + QK_ROPE)
    ones_q = _group_ones(dq, [(nope, nope), (rope, (QK_NOPE, dq))])
    head_out = lambda d: pl.BlockSpec((None, hb, tm, d), lambda i, h: (i // nt, h, i % nt, 0))
    q = pl.pallas_call(
        functools.partial(_mla_q_kernel, scale=scale),
        out_shape=jax.ShapeDtypeStruct((B, H, S, dq), _BF16),
        grid=(T // tm, H // hb),
        in_specs=[pl.BlockSpec((tm, Q_LORA), lambda i, h: (i, 0)), vec(Q_LORA),
                  pl.BlockSpec((Q_LORA, hb * dq), lambda i, h: (0, h)), pl.BlockSpec((dq, dq), lambda i, h: (0, 0)),
                  vec(QK_NOPE), vec(LANES),
                  pl.BlockSpec((tm, LANES), lambda i, h: (i, 0))],
        out_specs=head_out(dq),
        scratch_shapes=[pltpu.VMEM((tm, Q_LORA), _BF16)],
        compiler_params=_params("parallel", "arbitrary"),
        name="mla_q_proj",
    )(c, g_cq.reshape(1, -1), w_q, ones_q, g_qn.reshape(1, -1), gv_q, cs)
    k, v = pl.pallas_call(
        _mla_kv_kernel,
        out_shape=[jax.ShapeDtypeStruct((B, H, S, dq), _BF16), jax.ShapeDtypeStruct((B, H, S, V_HEAD), _BF16)],
        grid=(T // tm, H // hb),
        in_specs=[pl.BlockSpec((tm, KV_LORA), lambda i, h: (i, 1)),
                  pl.BlockSpec((tm, LANES), lambda i, h: (i, (Q_LORA + KV_LORA) // LANES)),
                  vec(KV_LORA),
                  pl.BlockSpec((KV_LORA, hb * (QK_NOPE + V_HEAD)), lambda i, h: (0, h)),
                  vec(QK_NOPE), vec(LANES),
                  pl.BlockSpec((tm, LANES), lambda i, h: (i, 0))],
        out_specs=[head_out(dq), head_out(V_HEAD)],
        scratch_shapes=[pltpu.VMEM((tm, KV_LORA), _BF16)],
        compiler_params=_params("parallel", "arbitrary"),
        name="mla_kv_proj",
    )(c, c, g_ckv.reshape(1, -1), w_kv, g_kn.reshape(1, -1), gv_k, cs)
    tq = ATTN_TILE
    o = pl.pallas_call(
        _mla_attn_kernel,
        out_shape=jax.ShapeDtypeStruct((B, S, H * V_HEAD), _BF16),
        grid=(B, H // hb, S // tq),
        in_specs=[pl.BlockSpec((None, hb, tq, dq), lambda b, h, i: (b, h, i, 0)),
                  pl.BlockSpec((None, hb, S, dq), lambda b, h, i: (b, h, 0, 0)),
                  pl.BlockSpec((None, hb, S, V_HEAD), lambda b, h, i: (b, h, 0, 0))],
        out_specs=pl.BlockSpec((None, tq, hb * V_HEAD), lambda b, h, i: (b, i, h)),
        scratch_shapes=[pltpu.VMEM((hb, tq, LANES), _F32), pltpu.VMEM((hb, tq, dv), _F32)],
        compiler_params=_params("parallel", "parallel", "arbitrary"),
        name="mla_attention",
    )(q, k, v)
    return o.reshape(T, H * V_HEAD)


def _router_kernel(x_ref, g_ref, w_ref, r_ref, hn_ref):
    hn = _rms(x_ref[...], g_ref[...])
    hn_ref[...] = hn.astype(hn_ref.dtype)
    w = w_ref[...]
    h_hi = hn.astype(_BF16)
    h_lo = (hn - h_hi.astype(_F32)).astype(_BF16)
    w_hi = w.astype(_BF16)
    w_lo = (w - w_hi.astype(_F32)).astype(_BF16)
    logits = (jnp.dot(h_hi, w_hi, preferred_element_type=_F32) + jnp.dot(h_hi, w_lo, preferred_element_type=_F32)
              + jnp.dot(h_lo, w_hi, preferred_element_type=_F32))
    lane = lax.broadcasted_iota(jnp.int32, logits.shape, 1)
    lane_f = lane.astype(_F32)
    neg = -jnp.inf
    lg = jnp.where(lane < N_EXPERTS, logits, neg)
    v1 = jnp.max(lg, axis=1, keepdims=True)
    i1 = jnp.min(jnp.where(lg == v1, lane_f, float(LANES)), axis=1, keepdims=True)
    lg2 = jnp.where(lane_f == i1, neg, lg)
    v2 = jnp.max(lg2, axis=1, keepdims=True)
    i2 = jnp.min(jnp.where(lg2 == v2, lane_f, float(LANES)), axis=1, keepdims=True)
    e2 = jnp.exp(v2 - v1)
    w1 = 1.0 / (1.0 + e2)
    w2 = e2 / (1.0 + e2)
    out = jnp.where(lane < N_EXPERTS, logits, 0.0)
    out = jnp.where(lane == N_EXPERTS, i1, out)
    out = jnp.where(lane == N_EXPERTS + 1, i2, out)
    out = jnp.where(lane == N_EXPERTS + 2, w1, out)
    out = jnp.where(lane == N_EXPERTS + 3, w2, out)
    r_ref[...] = out


def router(x, g, w_router_padded):
    T, d = x.shape
    return pl.pallas_call(
        _router_kernel,
        out_shape=[jax.ShapeDtypeStruct((T, LANES), _F32), jax.ShapeDtypeStruct((T, d), _BF16)],
        grid=(T // ROW_TILE,),
        in_specs=[pl.BlockSpec((ROW_TILE, d), lambda i: (i, 0)),
                  pl.BlockSpec((1, d), lambda i: (0, 0)),
                  pl.BlockSpec((d, LANES), lambda i: (0, 0))],
        out_specs=[pl.BlockSpec((ROW_TILE, LANES), lambda i: (i, 0)), pl.BlockSpec((ROW_TILE, d), lambda i: (i, 0))],
        compiler_params=_params("parallel"),
        name="moe_router",
    )(x, g.reshape(1, d), w_router_padded)


def _row_copy(src_hbm, dst_vmem, sem, src_row, dst_row):
    return pltpu.make_async_copy(src_hbm.at[pl.ds(src_row, 1)], dst_vmem.at[pl.ds(dst_row, 1)], sem)


def _index_spec(n):
    return pl.BlockSpec((None, 1, n), lambda i: (i, 0, 0), memory_space=pltpu.SMEM)


def _gather_kernel(tile_ref, chunk_ref, flag_ref, pos_ref, x_ref, o_ref, acc):
    del chunk_ref
    s = pl.program_id(0)
    flags = flag_ref[s]
    tr, ch = acc.shape[0], x_ref.shape[0]

    @pl.when((flags & _G_VALID) != 0)
    def _():
        rows = tile_ref[s] * tr + lax.broadcasted_iota(jnp.int32, (tr, ch), 0)
        pos = pos_ref[...]
        hit = jnp.logical_or(rows == pos[0:1, :], rows == pos[1:2, :])
        onehot = jnp.where(hit, 1.0, 0.0).astype(_BF16)
        part = jnp.dot(onehot, x_ref[...], preferred_element_type=_F32)

        @pl.when((flags & _G_FIRST) != 0)
        def _():
            acc[...] = part

        @pl.when((flags & _G_FIRST) == 0)
        def _():
            acc[...] = acc[...] + part

        @pl.when((flags & _G_LAST) != 0)
        def _():
            o_ref[...] = acc[...].astype(o_ref.dtype)


_G_VALID, _G_FIRST, _G_LAST = 1, 2, 4


def gather_rows(x, pos, n_tiles, tr):
    T, d = x.shape
    r = n_tiles * tr
    ch = GATHER_CHUNK
    n_steps = N_EXPERTS * (T // ch) + n_tiles
    tile_ids = jnp.arange(n_tiles, dtype=jnp.int32)
    member = (pos.reshape(-1) // tr)[:, None] == tile_ids[None, :]
    token = (jnp.arange(T * TOP_K, dtype=jnp.int32) // TOP_K)[:, None]
    first_tok = jnp.min(jnp.where(member, token, T), axis=0)
    last_tok = jnp.max(jnp.where(member, token, -1), axis=0)
    lo = jnp.where(last_tok >= 0, first_tok, 0) // ch
    hi = jnp.maximum(last_tok, 0) // ch
    count = hi - lo + 1
    end = jnp.cumsum(count)
    steps = jnp.arange(n_steps, dtype=jnp.int32)
    valid = steps < end[-1]
    tile = jnp.minimum(jnp.sum((end[None, :] <= steps[:, None]).astype(jnp.int32), axis=1), n_tiles - 1)
    tile_oh = (tile[:, None] == jnp.arange(n_tiles, dtype=jnp.int32)[None, :]).astype(jnp.int32)
    pick = lambda v: jnp.sum(tile_oh * v[None, :], axis=1)
    start = pick(end - count)
    chunk = jnp.where(valid, pick(lo) + steps - start, hi[-1])
    flags = (valid * _G_VALID + jnp.logical_and(valid, steps == start) * _G_FIRST
             + jnp.logical_and(valid, steps == pick(end) - 1) * _G_LAST)
    return pl.pallas_call(
        _gather_kernel,
        out_shape=jax.ShapeDtypeStruct((r, d), x.dtype),
        grid_spec=pltpu.PrefetchScalarGridSpec(
            num_scalar_prefetch=3,
            grid=(n_steps,),
            in_specs=[pl.BlockSpec((TOP_K, ch), lambda s, tile, chunk, flag: (0, chunk[s])),
                      pl.BlockSpec((ch, d), lambda s, tile, chunk, flag: (chunk[s], 0))],
            out_specs=pl.BlockSpec((tr, d), lambda s, tile, chunk, flag: (tile[s], 0)),
            scratch_shapes=[pltpu.VMEM((tr, d), _F32)],
        ),
        compiler_params=_params("arbitrary"),
        name="moe_gather",
    )(tile.astype(jnp.int32), chunk.astype(jnp.int32), flags.astype(jnp.int32), pos.T, x)


def _combine_kernel(pos_ref, x_ref, w_ref, y_hbm, o_ref, buf_a, buf_b, sem):
    tt = buf_a.shape[0]

    def start(r, carry):
        _row_copy(y_hbm, buf_a, sem, pos_ref[0, TOP_K * r], r).start()
        _row_copy(y_hbm, buf_b, sem, pos_ref[0, TOP_K * r + 1], r).start()
        return carry

    def wait(r, carry):
        _row_copy(y_hbm, buf_a, sem, 0, r).wait()
        _row_copy(y_hbm, buf_b, sem, 0, r).wait()
        return carry

    lax.fori_loop(0, tt, start, 0)
    lax.fori_loop(0, tt, wait, 0)
    w = w_ref[...]
    o_ref[...] = x_ref[...] + w[:, 0:1] * buf_a[...] + w[:, 1:2] * buf_b[...]


def moe_combine(x, y, pos_flat, weights, tt):
    T, d = x.shape
    return pl.pallas_call(
        _combine_kernel,
        out_shape=jax.ShapeDtypeStruct((T, d), _F32),
        grid=(T // tt,),
        in_specs=[_index_spec(tt * TOP_K), pl.BlockSpec((tt, d), lambda i: (i, 0)),
                  pl.BlockSpec((tt, TOP_K), lambda i: (i, 0)), pl.BlockSpec(memory_space=pl.ANY)],
        out_specs=pl.BlockSpec((tt, d), lambda i: (i, 0)),
        scratch_shapes=[pltpu.VMEM((tt, d), _F32), pltpu.VMEM((tt, d), _F32), pltpu.SemaphoreType.DMA],
        compiler_params=_params("arbitrary"),
        name="moe_combine",
    )(pos_flat.reshape(T // tt, 1, tt * TOP_K), x, weights, y)


def _moe_plan(route, tm, n_tiles):
    T = route.shape[0]
    e_flat = route[:, N_EXPERTS:N_EXPERTS + TOP_K].astype(jnp.int32).reshape(-1)
    onehot = (e_flat[:, None] == jnp.arange(N_EXPERTS, dtype=jnp.int32)[None, :]).astype(jnp.int32)
    csum = jnp.cumsum(onehot, axis=0)
    rank = jnp.sum(csum * onehot, axis=1) - 1
    tiles_e = (csum[-1] + tm - 1) // tm
    tile_end = jnp.cumsum(tiles_e)
    tile_start = tile_end - tiles_e
    pos = jnp.sum(onehot * tile_start[None, :], axis=1) * tm + rank
    tile_ids = jnp.arange(n_tiles, dtype=jnp.int32)
    tile_expert = jnp.minimum(jnp.sum((tile_ids[:, None] >= tile_end[None, :]).astype(jnp.int32), axis=1),
                              N_EXPERTS - 1).astype(jnp.int32)
    n_used = tile_end[-1:].astype(jnp.int32)
    experts = jnp.arange(N_EXPERTS, dtype=jnp.int32)
    has = tiles_e > 0
    later = jnp.where(jnp.logical_and(has[None, :], experts[None, :] > experts[:, None]), experts[None, :], N_EXPERTS)
    next_e = jnp.min(later, axis=1)
    last_e = next_e == N_EXPERTS
    next_e = jnp.where(last_e, jnp.min(jnp.where(has, experts, N_EXPERTS)), next_e)
    group_e = jnp.cumsum(has.astype(jnp.int32)) - 1
    tile_onehot = (tile_expert[:, None] == experts[None, :]).astype(jnp.int32)

    def of_tile(per_expert):
        return jnp.sum(tile_onehot * per_expert.astype(jnp.int32)[None, :], axis=1)

    index = tile_ids - of_tile(tile_start)
    groups = (tile_expert, (index == 0).astype(jnp.int32), index, jnp.maximum(of_tile(tiles_e), 1),
              of_tile(next_e), of_tile(last_e), of_tile(group_e),
              jnp.full((n_tiles,), jnp.sum(has.astype(jnp.int32)), jnp.int32))
    return groups, n_used, pos.astype(jnp.int32).reshape(T, TOP_K)


def moe_layer(x, g, w_router, we_gate, we_up, we_down):
    T, d = x.shape
    tm = MOE_ROW_TILE
    n_tiles = (T * TOP_K) // tm + N_EXPERTS
    w_r = jnp.zeros((d, LANES), _F32).at[:, :N_EXPERTS].set(w_router)
    route, hn = router(x, g, w_r)
    plan, n_used, pos = _moe_plan(route, tm, n_tiles)
    xs = gather_rows(hn, pos, n_tiles, tm)
    h = streamed_matmul(xs, [we_gate, we_up], plan, n_used, tm=tm, tn=UP_COLS, n_chunks=UP_CHUNKS, out_dtype=_BF16)
    y = streamed_matmul(h, [we_down], plan, n_used, tm=tm, tn=DOWN_COLS, n_chunks=DOWN_CHUNKS, out_dtype=_F32)
    weights = route[:, N_EXPERTS + TOP_K:N_EXPERTS + 2 * TOP_K]
    return moe_combine(x, y, pos.reshape(-1), weights, tm)


def _rope_tables(positions, half):
    inv = ROPE_BASE ** (-jnp.arange(half, dtype=_F32) / half)
    ang = positions.astype(_F32)[..., None] * inv
    return jnp.cos(ang), jnp.sin(ang)


def _rotate_half_cols(w, width):
    lead = w.shape[:-1]
    w2 = w.reshape(lead + (-1, 2, width // 2))
    return jnp.stack([-w2[..., 1, :], w2[..., 0, :]], axis=-2).reshape(w.shape)


def _swap_halves(g):
    half = g.shape[-1] // 2
    return jnp.concatenate([g[..., half:], g[..., :half]], axis=-1)


def kernel(x, positions, g_mix_norm, g_ffn_norm, w_in, conv_w, b_gates, g_mlstm_out, g_ret_out, w_mix_out,
           ffn_gate, ffn_up, ffn_down, w_dqkv, g_cq, g_ckv, w_uq, w_ukv, g_qn, g_qr, g_kn, g_kr, w_o,
           w_router, we_gate, we_up, we_down):
    B, S, D = x.shape
    T = B * S
    xf = x.reshape(T, D)
    n_gate = 2 * MLSTM_HEADS
    main = 4 * MLSTM_HEADS * HEAD_DIM

    w = w_in[0]
    w_gate = jnp.zeros((D, LANES), _F32).at[:, :n_gate].set(w[:, main:main + n_gate])
    g0 = g_mix_norm[0]
    dense = dict(tm=DENSE_ROW_TILE, vmem_limit_bytes=DENSE_VMEM_LIMIT_BYTES)
    plan, nu = _dense_stream_plan(T, DENSE_ROW_TILE)
    proj_m = streamed_matmul(xf, [w_in], plan, nu, tn=PROJ_COLS, n_chunks=UP_CHUNKS, out_dtype=_F32,
                             gain=g0, n_out=main, **dense)
    proj_r = streamed_matmul(xf, [w[None, :, main + n_gate:]], plan, nu, tn=PROJ_COLS,
                             n_chunks=UP_CHUNKS, out_dtype=_F32, gain=g0, **dense)
    gates = norm_matmul(xf, g0, w_gate[None], tn=LANES)[:, :n_gate] + b_gates[0][None, :]
    cos_r, sin_r = _rope_tables(positions, HEAD_DIM // 2)
    log_gamma = jnp.log1p(-jnp.exp2(-5.0 - jnp.arange(RET_HEADS, dtype=_F32)))
    mix = recurrent_mixer(proj_m.reshape(B, S, -1), proj_r.reshape(B, S, -1), gates.reshape(B, S, n_gate),
                          cos_r, sin_r, conv_w[0], g_mlstm_out[0], g_ret_out[0], log_gamma).reshape(T, -1)
    w_mix = w_mix_out[0].reshape(2, MLSTM_HEADS, HEAD_DIM, D).transpose(1, 0, 2, 3).reshape(-1, D)
    xf = dense_matmul(mix, w_mix, tn=1024, out_dtype=_F32, residual=xf)

    plan_up, nu_up = _dense_stream_plan(T, ROW_TILE)
    hmid = streamed_matmul(xf, [ffn_gate, ffn_up], plan_up, nu_up, tm=ROW_TILE, tn=UP_COLS, n_chunks=UP_CHUNKS,
                           out_dtype=_BF16, gain=g_ffn_norm[0])
    xf = streamed_matmul(hmid, [ffn_down], plan, nu, tn=DENSE_DOWN_COLS, n_chunks=DOWN_CHUNKS,
                         out_dtype=_F32, residual=xf, **dense)

    H = MLA_HEADS
    wd = w_dqkv[0]
    w_kr = wd[:, Q_LORA + KV_LORA:]
    wd_full = jnp.concatenate([wd, _rotate_half_cols(w_kr, QK_ROPE)], axis=1)
    c = norm_matmul(xf, g_mix_norm[1], wd_full[None], tn=wd_full.shape[1] // 3)
    wq = w_uq[0].reshape(Q_LORA, H, QK_NOPE + QK_ROPE)
    wq_r = wq[..., QK_NOPE:]
    w_q = jnp.concatenate([wq, _rotate_half_cols(wq_r, QK_ROPE)], axis=-1).reshape(Q_LORA, -1)
    cos_m, sin_m = _rope_tables(positions, QK_ROPE // 2)
    cs = jnp.concatenate([cos_m, cos_m, sin_m, sin_m], axis=-1).reshape(T, LANES)
    gv_q = jnp.concatenate([g_qr[0], _swap_halves(g_qr[0])]).reshape(1, LANES)
    gv_k = jnp.concatenate([g_kr[0], _swap_halves(g_kr[0])]).reshape(1, LANES)
    attn = mla_attention(c, cs, w_q, w_ukv[0], g_cq[0], g_ckv[0], g_qn[0], gv_q, g_kn[0], gv_k, B, S)
    xf = dense_matmul(attn, w_o[0], tn=1024, out_dtype=_F32, residual=xf)

    xf = moe_layer(xf, g_ffn_norm[1], w_router[0], we_gate[0], we_up[0], we_down[0])
    return xf.reshape(B, S, D)
```
